```python
import jax
import jax.numpy as jnp
from jax import lax
import numpy as np

D_MODEL = 1024
BATCH = 16
SEQ = 2048
DEPTH = 2

N_GROUPS = 4
HEADS = 4
HEAD_DIM = 64
D_GROUP = HEADS * HEAD_DIM
D_MIX = N_GROUPS * D_GROUP
D_FF = 2816
Q_BLOCK = 128
ROPE_THETA = 10000.0
MLA_Q_RANK = 256
MLA_KV_RANK = 128
MLA_D_NOPE = 64
MLA_D_ROPE = 32
MLA_D_V = 64
NSA_CMP_LEN = 32
NSA_CMP_STRIDE = 16
NSA_SEL_LEN = 64
NSA_N_SEL = 8
NSA_N_INIT = 1
NSA_N_LOCAL = 2
NSA_WINDOW = 512
DSA_TOPK = 256
DSA_IDX_HEADS = 8
DSA_IDX_DIM = 32
DN_ALPHA = (2.0 * DEPTH) ** 0.25
DN_BETA = (8.0 * DEPTH) ** -0.25
LN_EPS = 1e-5
RMS_EPS = 1e-6
NEG_INF = -1e30
FORCE_SCORE = 1e4

IN_SPLITS = (
    ('mla_cq', MLA_Q_RANK), ('mla_ckv', MLA_KV_RANK), ('mla_krope', MLA_D_ROPE),
    ('nsa_q', D_GROUP), ('nsa_k_cmp', HEAD_DIM), ('nsa_v_cmp', HEAD_DIM),
    ('nsa_k_slc', HEAD_DIM), ('nsa_v_slc', HEAD_DIM), ('nsa_k_win', HEAD_DIM),
    ('nsa_v_win', HEAD_DIM), ('nsa_gate', 3 * HEADS),
    ('dsa_q', D_GROUP), ('dsa_k', HEAD_DIM), ('dsa_v', HEAD_DIM),
    ('idx_q', DSA_IDX_HEADS * DSA_IDX_DIM), ('idx_k', DSA_IDX_DIM), ('idx_w', DSA_IDX_HEADS),
    ('sb_q', D_GROUP), ('sb_k', D_GROUP), ('sb_v', D_GROUP),
)
D_IN = sum(w for _, w in IN_SPLITS)

kernel_name = 'hybrid_mla_nsa_dsa_stickbreak_block'


def layer_norm(x, g, b):
    xf = x.astype(jnp.float32)
    mu = jnp.mean(xf, -1, keepdims=True)
    var = jnp.mean(jnp.square(xf - mu), -1, keepdims=True)
    return ((xf - mu) * lax.rsqrt(var + LN_EPS) * g + b).astype(x.dtype)


def rms_norm(x, g):
    xf = x.astype(jnp.float32)
    return (xf * lax.rsqrt(jnp.mean(xf * xf, -1, keepdims=True) + RMS_EPS) * g).astype(x.dtype)


def swiglu(x, w1, w3, w2):
    return (jax.nn.silu(x @ w1) * (x @ w3)) @ w2


def rope_tables(seq, dim):
    inv = ROPE_THETA ** (-jnp.arange(0, dim, 2, dtype=jnp.float32) / dim)
    ang = jnp.arange(seq, dtype=jnp.float32)[:, None] * inv[None, :]
    return jnp.cos(ang), jnp.sin(ang)


def apply_rope(z, cos, sin):
    shape = (cos.shape[0],) + (1,) * (z.ndim - 3) + (cos.shape[1],)
    c = cos.reshape(shape).astype(z.dtype)
    s = sin.reshape(shape).astype(z.dtype)
    z1, z2 = jnp.split(z, 2, axis=-1)
    return jnp.concatenate([z1 * c - z2 * s, z1 * s + z2 * c], axis=-1)


def masked_softmax(s, mask):
    s = jnp.where(mask, s, NEG_INF)
    e = jnp.where(mask, jnp.exp(s - jnp.max(s, -1, keepdims=True)), 0.0)
    return e / jnp.maximum(jnp.sum(e, -1, keepdims=True), 1e-30)


def unblock(y):
    nb, b, qb = y.shape[:3]
    return jnp.moveaxis(y, 0, 1).reshape((b, nb * qb) + y.shape[3:])


def split_columns(h):
    parts = {}
    off = 0
    for name, width in IN_SPLITS:
        parts[name] = h[..., off:off + width]
        off += width
    return parts


def mla_mixer(c_q, c_kv, k_rope, q_norm_g, w_uq, kv_norm_g, w_ukv, cos_r, sin_r):
    B, S, _ = c_q.shape
    q = (rms_norm(c_q, q_norm_g) @ w_uq).reshape(B, S, HEADS, MLA_D_NOPE + MLA_D_ROPE)
    q_nope = q[..., :MLA_D_NOPE]
    q_pe = apply_rope(q[..., MLA_D_NOPE:], cos_r, sin_r)
    kv = (rms_norm(c_kv, kv_norm_g) @ w_ukv).reshape(B, S, HEADS, MLA_D_NOPE + MLA_D_V)
    k_nope = kv[..., :MLA_D_NOPE]
    v = kv[..., MLA_D_NOPE:]
    k_pe = apply_rope(k_rope, cos_r, sin_r)
    scale = (MLA_D_NOPE + MLA_D_ROPE) ** -0.5
    k_pos = jnp.arange(S)

    def block(i):
        q0 = i * Q_BLOCK
        qn = lax.dynamic_slice_in_dim(q_nope, q0, Q_BLOCK, 1)
        qp = lax.dynamic_slice_in_dim(q_pe, q0, Q_BLOCK, 1)
        s = jnp.einsum('bqhd,bkhd->bhqk', qn, k_nope) + jnp.einsum('bqhd,bkd->bhqk', qp, k_pe)
        q_pos = q0 + jnp.arange(Q_BLOCK)
        p = masked_softmax(s.astype(jnp.float32) * scale, k_pos[None, :] <= q_pos[:, None])
        return jnp.einsum('bhqk,bkhd->bqhd', p.astype(v.dtype), v)

    return unblock(lax.map(block, jnp.arange(S // Q_BLOCK))).reshape(B, S, D_GROUP)


def nsa_mixer(q, k_cmp, v_cmp, k_slc, v_slc, k_win, v_win, gate_logits,
              cmp_pos, wk1, wk2, wv1, wv2, cos_h, sin_h):
    B, S, _ = q.shape
    q = apply_rope(q.reshape(B, S, HEADS, HEAD_DIM), cos_h, sin_h)
    k_cmp = apply_rope(k_cmp, cos_h, sin_h)
    k_slc = apply_rope(k_slc, cos_h, sin_h)
    k_win = apply_rope(k_win, cos_h, sin_h)
    scale = HEAD_DIM ** -0.5
    t_pos = jnp.arange(S)

    n_cmp = (S - NSA_CMP_LEN) // NSA_CMP_STRIDE + 1
    tok = np.arange(n_cmp)[:, None] * NSA_CMP_STRIDE + np.arange(NSA_CMP_LEN)[None, :]

    def compress(z, w1, w2):
        blk = (z[:, tok] + cmp_pos).reshape(B, n_cmp, NSA_CMP_LEN * HEAD_DIM)
        return jax.nn.gelu(blk @ w1) @ w2

    kc = compress(k_cmp, wk1, wk2)
    vc = compress(v_cmp, wv1, wv2)
    s_c = jnp.einsum('bshd,bnd->bhsn', q, kc).astype(jnp.float32) * scale
    p_c = masked_softmax(s_c, tok[:, -1][None, :] <= t_pos[:, None])
    o_cmp = jnp.einsum('bhsn,bnd->bshd', p_c.astype(vc.dtype), vc)

    n_slc = S // NSA_SEL_LEN
    slc_start = np.arange(n_slc) * NSA_SEL_LEN
    overlap = ((tok[:, :1] <= slc_start[None, :] + NSA_SEL_LEN - 1)
               & (tok[:, -1:] >= slc_start[None, :])).astype(np.float32)
    imp = jnp.einsum('bhsn,nj->bsj', p_c, jnp.asarray(overlap))
    cur = (t_pos // NSA_SEL_LEN)[:, None]
    jb = jnp.arange(n_slc)[None, :]
    forced = (jb < NSA_N_INIT) | ((jb <= cur) & (jb > cur - NSA_N_LOCAL))
    imp = jnp.where(forced, FORCE_SCORE, jnp.where(jb <= cur, imp, NEG_INF))
    n_top = min(NSA_N_SEL, n_slc)
    _, sel = lax.top_k(imp, n_top)
    k_blk = k_slc.reshape(B, n_slc, NSA_SEL_LEN, HEAD_DIM)
    v_blk = v_slc.reshape(B, n_slc, NSA_SEL_LEN, HEAD_DIM)
    kw = jnp.pad(k_win, ((0, 0), (NSA_WINDOW, 0), (0, 0)))
    vw = jnp.pad(v_win, ((0, 0), (NSA_WINDOW, 0), (0, 0)))
    b_idx = jnp.arange(B)[:, None, None]
    n_sel_tok = n_top * NSA_SEL_LEN

    def block(i):
        q0 = i * Q_BLOCK
        qb = lax.dynamic_slice_in_dim(q, q0, Q_BLOCK, 1)
        q_pos = q0 + jnp.arange(Q_BLOCK)
        idx = lax.dynamic_slice_in_dim(sel, q0, Q_BLOCK, 1)
        kg = k_blk[b_idx, idx].reshape(B, Q_BLOCK, n_sel_tok, HEAD_DIM)
        vg = v_blk[b_idx, idx].reshape(B, Q_BLOCK, n_sel_tok, HEAD_DIM)
        g_pos = (idx[..., None] * NSA_SEL_LEN + jnp.arange(NSA_SEL_LEN)).reshape(B, Q_BLOCK, n_sel_tok)
        s = jnp.einsum('bqhd,bqkd->bhqk', qb, kg).astype(jnp.float32) * scale
        p = masked_softmax(s, (g_pos <= q_pos[None, :, None])[:, None])
        o_s = jnp.einsum('bhqk,bqkd->bqhd', p.astype(vg.dtype), vg)
        kb = lax.dynamic_slice_in_dim(kw, q0, NSA_WINDOW + Q_BLOCK, 1)
        vb = lax.dynamic_slice_in_dim(vw, q0, NSA_WINDOW + Q_BLOCK, 1)
        w_pos = q0 - NSA_WINDOW + jnp.arange(NSA_WINDOW + Q_BLOCK)
        dist = q_pos[:, None] - w_pos[None, :]
        s = jnp.einsum('bqhd,bkd->bhqk', qb, kb).astype(jnp.float32) * scale
        p = masked_softmax(s, (dist >= 0) & (dist < NSA_WINDOW) & (w_pos[None, :] >= 0))
        o_w = jnp.einsum('bhqk,bkd->bqhd', p.astype(vb.dtype), vb)
        return o_s, o_w

    o_slc, o_win = lax.map(block, jnp.arange(S // Q_BLOCK))
    o_slc = unblock(o_slc)
    o_win = unblock(o_win)
    g = jax.nn.sigmoid(gate_logits).reshape(B, S, 3, HEADS)[..., None]
    o = g[:, :, 0] * o_cmp + g[:, :, 1] * o_slc + g[:, :, 2] * o_win
    return o.reshape(B, S, D_GROUP)


def dsa_mixer(q, k, v, q_idx, k_idx, w_idx, cos_h, sin_h, cos_i, sin_i):
    B, S, _ = q.shape
    q = apply_rope(q.reshape(B, S, HEADS, HEAD_DIM), cos_h, sin_h)
    k = apply_rope(k, cos_h, sin_h)
    q_idx = apply_rope(q_idx.reshape(B, S, DSA_IDX_HEADS, DSA_IDX_DIM), cos_i, sin_i)
    k_idx = apply_rope(k_idx, cos_i, sin_i)
    w_idx = w_idx.astype(jnp.float32) * (DSA_IDX_HEADS * DSA_IDX_DIM) ** -0.5
    scale = HEAD_DIM ** -0.5
    n_top = min(DSA_TOPK, S // 4)
    k_pos = jnp.arange(S)
    b_idx = jnp.arange(B)[:, None, None]

    def block(i):
        q0 = i * Q_BLOCK
        q_pos = q0 + jnp.arange(Q_BLOCK)
        qi = lax.dynamic_slice_in_dim(q_idx, q0, Q_BLOCK, 1)
        wi = lax.dynamic_slice_in_dim(w_idx, q0, Q_BLOCK, 1)
        rel = jax.nn.relu(jnp.einsum('bqhd,bsd->bqhs', qi, k_idx).astype(jnp.float32))
        score = jnp.einsum('bqh,bqhs->bqs', wi, rel)
        score = jnp.where(k_pos[None, None, :] <= q_pos[None, :, None], score, NEG_INF)
        _, idx = lax.top_k(score, n_top)
        kg = k[b_idx, idx]
        vg = v[b_idx, idx]
        qb = lax.dynamic_slice_in_dim(q, q0, Q_BLOCK, 1)
        s = jnp.einsum('bqhd,bqkd->bhqk', qb, kg).astype(jnp.float32) * scale
        p = masked_softmax(s, (idx <= q_pos[None, :, None])[:, None])
        return jnp.einsum('bhqk,bqkd->bqhd', p.astype(vg.dtype), vg)

    return unblock(lax.map(block, jnp.arange(S // Q_BLOCK))).reshape(B, S, D_GROUP)


def stick_breaking_mixer(q, k, v):
    B, S, _ = q.shape
    q = q.reshape(B, S, HEADS, HEAD_DIM)
    k = k.reshape(B, S, HEADS, HEAD_DIM)
    v = v.reshape(B, S, HEADS, HEAD_DIM)
    scale = HEAD_DIM ** -0.5
    k_pos = jnp.arange(S)

    def block(i):
        q0 = i * Q_BLOCK
        qb = lax.dynamic_slice_in_dim(q, q0, Q_BLOCK, 1)
        q_pos = q0 + jnp.arange(Q_BLOCK)
        z = jnp.einsum('bqhd,bkhd->bhqk', qb, k).astype(jnp.float32) * scale
        mask = k_pos[None, :] < q_pos[:, None]
        log_1m = jnp.where(mask, jax.nn.log_sigmoid(-z), 0.0)
        after = lax.cumsum(log_1m, axis=3, reverse=True) - log_1m
        a = jnp.where(mask, jnp.exp(jax.nn.log_sigmoid(z) + after), 0.0)
        return jnp.einsum('bhqk,bkhd->bqhd', a.astype(v.dtype), v)

    return unblock(lax.map(block, jnp.arange(S // Q_BLOCK))).reshape(B, S, D_GROUP)


def hybrid_mixer(x, w_in, mla_q_norm, mla_w_uq, mla_kv_norm, mla_w_ukv, nsa_cmp_pos,
                 nsa_cmp_wk1, nsa_cmp_wk2, nsa_cmp_wv1, nsa_cmp_wv2, group_norm_g, w_out,
                 cos_h, sin_h, cos_r, sin_r, cos_i, sin_i):
    B, S, _ = x.shape
    h = split_columns(x @ w_in)
    y_a = mla_mixer(h['mla_cq'], h['mla_ckv'], h['mla_krope'], mla_q_norm, mla_w_uq,
                    mla_kv_norm, mla_w_ukv, cos_r, sin_r)
    y_b = nsa_mixer(h['nsa_q'], h['nsa_k_cmp'], h['nsa_v_cmp'], h['nsa_k_slc'], h['nsa_v_slc'],
                    h['nsa_k_win'], h['nsa_v_win'], h['nsa_gate'], nsa_cmp_pos,
                    nsa_cmp_wk1, nsa_cmp_wk2, nsa_cmp_wv1, nsa_cmp_wv2, cos_h, sin_h)
    y_c = dsa_mixer(h['dsa_q'], h['dsa_k'], h['dsa_v'], h['idx_q'], h['idx_k'], h['idx_w'],
                    cos_h, sin_h, cos_i, sin_i)
    y_d = stick_breaking_mixer(h['sb_q'], h['sb_k'], h['sb_v'])
    y = jnp.concatenate([y_a, y_b, y_c, y_d], axis=-1).reshape(B, S, N_GROUPS, D_GROUP)
    y = rms_norm(y, group_norm_g.reshape(N_GROUPS, D_GROUP)).reshape(B, S, D_MIX)
    return y @ w_out


def setup_inputs(seed: int = 0) -> dict:
    key = jax.random.key(seed)
    keys = jax.random.split(key, 32)
    counter = [0]

    def nrm(shape, scale):
        k = keys[counter[0]]
        counter[0] += 1
        return jax.random.normal(k, shape, jnp.float32) * scale

    def gain(shape):
        return 1.0 + nrm(shape, 0.02)

    L, D = DEPTH, D_MODEL
    return {
        'x': nrm((BATCH, SEQ, D), 1.0),
        'ln1_g': gain((L, D)),
        'ln1_b': nrm((L, D), 0.02),
        'ffn1_w1': nrm((L, D, D_FF), D ** -0.5),
        'ffn1_w3': nrm((L, D, D_FF), D ** -0.5),
        'ffn1_w2': nrm((L, D_FF, D), DN_BETA * D_FF ** -0.5),
        'w_in': nrm((L, D, D_IN), D ** -0.5),
        'mla_q_norm': gain((L, MLA_Q_RANK)),
        'mla_w_uq': nrm((L, MLA_Q_RANK, HEADS * (MLA_D_NOPE + MLA_D_ROPE)), MLA_Q_RANK ** -0.5),
        'mla_kv_norm': gain((L, MLA_KV_RANK)),
        'mla_w_ukv': nrm((L, MLA_KV_RANK, HEADS * (MLA_D_NOPE + MLA_D_V)), MLA_KV_RANK ** -0.5),
        'nsa_cmp_pos': nrm((L, NSA_CMP_LEN, HEAD_DIM), 0.1),
        'nsa_cmp_wk1': nrm((L, NSA_CMP_LEN * HEAD_DIM, HEAD_DIM), (NSA_CMP_LEN * HEAD_DIM) ** -0.5),
        'nsa_cmp_wk2': nrm((L, HEAD_DIM, HEAD_DIM), HEAD_DIM ** -0.5),
        'nsa_cmp_wv1': nrm((L, NSA_CMP_LEN * HEAD_DIM, HEAD_DIM), (NSA_CMP_LEN * HEAD_DIM) ** -0.5),
        'nsa_cmp_wv2': nrm((L, HEAD_DIM, HEAD_DIM), HEAD_DIM ** -0.5),
        'group_norm_g': gain((L, D_MIX)),
        'w_out': nrm((L, D_MIX, D), DN_BETA * D_MIX ** -0.5),
        'ln2_g': gain((L, D)),
        'ln2_b': nrm((L, D), 0.02),
        'ffn2_w1': nrm((L, D, D_FF), D ** -0.5),
        'ffn2_w3': nrm((L, D, D_FF), D ** -0.5),
        'ffn2_w2': nrm((L, D_FF, D), DN_BETA * D_FF ** -0.5),
        'ln3_g': gain((L, D)),
        'ln3_b': nrm((L, D), 0.02),
    }


def reference(x, ln1_g, ln1_b, ffn1_w1, ffn1_w3, ffn1_w2, w_in, mla_q_norm, mla_w_uq,
              mla_kv_norm, mla_w_ukv, nsa_cmp_pos, nsa_cmp_wk1, nsa_cmp_wk2, nsa_cmp_wv1,
              nsa_cmp_wv2, group_norm_g, w_out, ln2_g, ln2_b, ffn2_w1, ffn2_w3, ffn2_w2,
              ln3_g, ln3_b):
    S = x.shape[1]
    cos_h, sin_h = rope_tables(S, HEAD_DIM)
    cos_r, sin_r = rope_tables(S, MLA_D_ROPE)
    cos_i, sin_i = rope_tables(S, DSA_IDX_DIM)
    for l in range(DEPTH):
        x = layer_norm(DN_ALPHA * x + 0.5 * swiglu(x, ffn1_w1[l], ffn1_w3[l], ffn1_w2[l]),
                       ln1_g[l], ln1_b[l])
        y = hybrid_mixer(x, w_in[l], mla_q_norm[l], mla_w_uq[l], mla_kv_norm[l], mla_w_ukv[l],
                         nsa_cmp_pos[l], nsa_cmp_wk1[l], nsa_cmp_wk2[l], nsa_cmp_wv1[l],
                         nsa_cmp_wv2[l], group_norm_g[l], w_out[l],
                         cos_h, sin_h, cos_r, sin_r, cos_i, sin_i)
        x = layer_norm(DN_ALPHA * x + y, ln2_g[l], ln2_b[l])
        x = layer_norm(DN_ALPHA * x + 0.5 * swiglu(x, ffn2_w1[l], ffn2_w3[l], ffn2_w2[l]),
                       ln3_g[l], ln3_b[l])
    return x
```

```python
import functools

import numpy as np
import jax
import jax.numpy as jnp
from jax import lax
from jax.experimental import pallas as pl
from jax.experimental.pallas import tpu as pltpu

F32 = jnp.float32
BF16 = jnp.bfloat16

D_MODEL = 1024
DEPTH = 2
HEADS = 4
HEAD_DIM = 64
D_GROUP = HEADS * HEAD_DIM
N_GROUPS = 4
D_FF = 2816
ROPE_THETA = 10000.0
MLA_Q_RANK = 256
MLA_KV_RANK = 128
MLA_D_NOPE = 64
MLA_D_ROPE = 32
MLA_D_V = 64
NSA_CMP_LEN = 32
NSA_CMP_STRIDE = 16
NSA_SEL_LEN = 64
NSA_N_SEL = 8
NSA_N_INIT = 1
NSA_N_LOCAL = 2
NSA_WINDOW = 512
SEL_SHIFT = NSA_SEL_LEN.bit_length() - 1
DSA_TOPK = 256
DSA_IDX_HEADS = 8
DSA_IDX_DIM = 32
DN_ALPHA = (2.0 * DEPTH) ** 0.25
LN_EPS = 1e-5
RMS_EPS = 1e-6
NEG_INF = -1e30
FORCE_SCORE = 1e4

LANES = 128
QB = 128
CK = 256
VMEM_LIMIT = 48 * 1024 * 1024

MISC_KROPE = 0
MISC_IDXK = 32
MISC_IDXW = 64
MISC_GATE = 72

NT_DIMS = (((1,), (1,)), ((), ()))


def _dot(a, b):
    return jnp.dot(a, b, preferred_element_type=F32)


def _dot_nt(a, b):
    return lax.dot_general(a, b, NT_DIMS, preferred_element_type=F32)


def _split_bf16(x):
    hi = x.astype(BF16)
    lo = (x - hi.astype(F32)).astype(BF16)
    return hi, lo


def _layer_norm(y, g, b):
    mu = jnp.mean(y, -1, keepdims=True)
    d = y - mu
    var = jnp.mean(d * d, -1, keepdims=True)
    return d * lax.rsqrt(var + LN_EPS) * g + b


def _rms_norm(y, g):
    return y * lax.rsqrt(jnp.mean(y * y, -1, keepdims=True) + RMS_EPS) * g


def _params(*sem):
    return pltpu.CompilerParams(dimension_semantics=sem, vmem_limit_bytes=VMEM_LIMIT)


def _ffn_kernel(x_ref, w1_ref, w3_ref, w2_ref, g_ref, b_ref, o_ref, xb_ref, acc_ref):
    j = pl.program_id(1)

    @pl.when(j == 0)
    def _():
        xb_ref[...] = x_ref[...].astype(BF16)
        acc_ref[...] = jnp.zeros_like(acc_ref)

    xb = xb_ref[...]
    h = _dot(xb, w1_ref[...])
    u = _dot(xb, w3_ref[...])
    a = h * jax.nn.sigmoid(h) * u
    acc_ref[...] += _dot(a.astype(BF16), w2_ref[...])

    @pl.when(j == pl.num_programs(1) - 1)
    def _():
        y = DN_ALPHA * x_ref[...] + 0.5 * acc_ref[...]
        o_ref[...] = _layer_norm(y, g_ref[...], b_ref[...])


def _ffn_ln(x2, w1, w3, w2, g, b, *, tm, tf):
    n, d = x2.shape
    dff = w1.shape[1]
    return pl.pallas_call(
        _ffn_kernel,
        out_shape=jax.ShapeDtypeStruct((n, d), F32),
        grid=(n // tm, dff // tf),
        in_specs=[
            pl.BlockSpec((tm, d), lambda i, j: (i, 0)),
            pl.BlockSpec((d, tf), lambda i, j: (0, j)),
            pl.BlockSpec((d, tf), lambda i, j: (0, j)),
            pl.BlockSpec((tf, d), lambda i, j: (j, 0)),
            pl.BlockSpec((1, d), lambda i, j: (0, 0)),
            pl.BlockSpec((1, d), lambda i, j: (0, 0)),
        ],
        out_specs=pl.BlockSpec((tm, d), lambda i, j: (i, 0)),
        scratch_shapes=[pltpu.VMEM((tm, d), BF16), pltpu.VMEM((tm, d), F32)],
        compiler_params=_params("parallel", "arbitrary"),
        name="ffn_ln",
    )(x2, w1, w3, w2, g, b)


_MAIN_ORDER = (
    ('mla_cq', 256), ('nsa_q', 256), ('dsa_q', 256), ('idx_q', 256),
    ('sb_q', 256), ('sb_k', 256), ('sb_v', 256),
    ('slab_k', 256),
    ('slab_v', 256),
    ('mla_ckv', 128),
    ('misc', 128),
)
_MAIN_OFF = {}
_o = 0
for _n, _w in _MAIN_ORDER:
    _MAIN_OFF[_n] = (_o, _o + _w)
    _o += _w
N_MAIN = _o
_ROT_ORDER = (('nsa_q', 256), ('dsa_q', 256), ('idx_q', 256), ('slab_k', 256), ('misc', 128))
_ROT_OFF = {}
_o = 0
for _n, _w in _ROT_ORDER:
    _ROT_OFF[_n] = (_o, _o + _w)
    _o += _w
N_ROT = _o

_IN_SPLITS = (
    ('mla_cq', MLA_Q_RANK), ('mla_ckv', MLA_KV_RANK), ('mla_krope', MLA_D_ROPE),
    ('nsa_q', D_GROUP), ('nsa_k_cmp', HEAD_DIM), ('nsa_v_cmp', HEAD_DIM),
    ('nsa_k_slc', HEAD_DIM), ('nsa_v_slc', HEAD_DIM), ('nsa_k_win', HEAD_DIM),
    ('nsa_v_win', HEAD_DIM), ('nsa_gate', 3 * HEADS),
    ('dsa_q', D_GROUP), ('dsa_k', HEAD_DIM), ('dsa_v', HEAD_DIM),
    ('idx_q', DSA_IDX_HEADS * DSA_IDX_DIM), ('idx_k', DSA_IDX_DIM), ('idx_w', DSA_IDX_HEADS),
    ('sb_q', D_GROUP), ('sb_k', D_GROUP), ('sb_v', D_GROUP),
)
_SRC = {}
_o = 0
for _n, _w in _IN_SPLITS:
    _SRC[_n] = np.arange(_o, _o + _w)
    _o += _w
D_IN = _o


def _swap_halves(width, dim):
    idx = np.arange(width)
    return (idx // dim) * dim + (idx % dim + dim // 2) % dim


def _column_maps():
    pieces = {
        'mla_cq': [_SRC['mla_cq']], 'nsa_q': [_SRC['nsa_q']], 'dsa_q': [_SRC['dsa_q']],
        'idx_q': [_SRC['idx_q']], 'sb_q': [_SRC['sb_q']], 'sb_k': [_SRC['sb_k']], 'sb_v': [_SRC['sb_v']],
        'slab_k': [_SRC['nsa_k_cmp'], _SRC['nsa_k_slc'], _SRC['nsa_k_win'], _SRC['dsa_k']],
        'slab_v': [_SRC['nsa_v_cmp'], _SRC['nsa_v_slc'], _SRC['nsa_v_win'], _SRC['dsa_v']],
        'mla_ckv': [_SRC['mla_ckv']],
        'misc': [_SRC['mla_krope'], _SRC['idx_k'], _SRC['idx_w'], _SRC['nsa_gate'],
                 -np.ones(LANES - MISC_GATE - 3 * HEADS, np.int64)],
    }
    main = np.concatenate([np.concatenate(pieces[n]) for n, _ in _MAIN_ORDER])
    rot_dim = {'nsa_q': HEAD_DIM, 'dsa_q': HEAD_DIM, 'idx_q': DSA_IDX_DIM, 'slab_k': HEAD_DIM}
    rot = []
    for n, w in _ROT_ORDER:
        cols = np.concatenate(pieces[n])
        if n == 'misc':
            r = -np.ones(w, np.int64)
            r[:64] = cols[:64][_swap_halves(64, 32)]
        else:
            r = cols[_swap_halves(w, rot_dim[n])]
        rot.append(r)
    return main, np.concatenate(rot)


def _gather_cols(w, cols):
    out = jnp.take(w, np.maximum(cols, 0), axis=1)
    return jnp.where(jnp.asarray(cols >= 0)[None, :], out, 0.0)


def _rope_tables(seq):
    def base(dim):
        inv = ROPE_THETA ** (-jnp.arange(0, dim, 2, dtype=F32) / dim)
        ang = jnp.arange(seq, dtype=F32)[:, None] * inv[None, :]
        c, s = jnp.cos(ang), jnp.sin(ang)
        return jnp.concatenate([c, c], -1), jnp.concatenate([-s, s], -1)

    c64, s64 = base(HEAD_DIM)
    c32, s32 = base(DSA_IDX_DIM)
    t64 = (jnp.tile(c64, (1, 4)), jnp.tile(s64, (1, 4)))
    t32 = (jnp.tile(c32, (1, 8)), jnp.tile(s32, (1, 8)))
    ones = jnp.ones((seq, LANES - 64), F32)
    td = (jnp.concatenate([c32, c32, ones], -1), jnp.concatenate([s32, s32, 0.0 * ones], -1))
    scale = (MLA_D_NOPE + MLA_D_ROPE) ** -0.5
    cq = jnp.concatenate([jnp.ones((seq, 64), F32), c32, jnp.zeros((seq, 32), F32)], -1) * scale
    sq = jnp.concatenate([jnp.zeros((seq, 64), F32), s32, jnp.zeros((seq, 32), F32)], -1) * scale
    tq = (jnp.tile(cq, (1, 4)), jnp.tile(sq, (1, 4)))
    return t64, t32, td, tq


def _placement_constants():
    p_hi = np.zeros((256, 1024), np.float32)
    p_lo = np.zeros((256, 1024), np.float32)
    for h in range(DSA_IDX_HEADS):
        for d in range(DSA_IDX_DIM):
            p_hi[h * 32 + d, h * 128 + d] = 1
            p_hi[h * 32 + d, h * 128 + 64 + d] = 1
            p_lo[h * 32 + d, h * 128 + 32 + d] = 1
    pk_hi = np.zeros((128, 128), np.float32)
    pk_lo = np.zeros((128, 128), np.float32)
    for d in range(DSA_IDX_DIM):
        pk_hi[MISC_IDXK + d, d] = 1
        pk_hi[MISC_IDXK + d, 32 + d] = 1
        pk_lo[MISC_IDXK + d, 64 + d] = 1
    pk_pe = np.zeros((128, 512), np.float32)
    for h in range(HEADS):
        for d in range(MLA_D_ROPE):
            pk_pe[MISC_KROPE + d, h * 128 + 64 + d] = 1
    return tuple(jnp.asarray(a, BF16) for a in (p_hi, p_lo, pk_hi, pk_lo, pk_pe))


def _mla_weights(w_uq, w_ukv):
    dq = MLA_D_NOPE + MLA_D_ROPE
    cols_q = -np.ones(512, np.int64)
    cols_qr = -np.ones(512, np.int64)
    cols_k = -np.ones(512, np.int64)
    cols_v = np.zeros(256, np.int64)
    for h in range(HEADS):
        cols_q[h * 128:h * 128 + dq] = h * dq + np.arange(dq)
        pe = h * dq + MLA_D_NOPE + np.arange(MLA_D_ROPE)
        cols_qr[h * 128 + 64:h * 128 + 96] = pe[_swap_halves(32, 32)]
        cols_k[h * 128:h * 128 + 64] = h * 128 + np.arange(64)
        cols_v[h * 64:(h + 1) * 64] = h * 128 + 64 + np.arange(64)
    return (_gather_cols(w_uq, cols_q).astype(BF16), _gather_cols(w_uq, cols_qr).astype(BF16),
            _gather_cols(w_ukv, cols_k).astype(BF16), jnp.take(w_ukv, cols_v, axis=1).astype(BF16))


def _inproj_kernel(x_ref, wm_ref, wr_ref, c64_ref, s64_ref, c32_ref, s32_ref, cd_ref, sd_ref,
                   cq_ref, sq_ref, qg_ref, kvg_ref, wuq_ref, wuqr_ref, wuk_ref, wuv_ref,
                   phi_ref, plo_ref, pkhi_ref, pklo_ref, pkpe_ref,
                   mq_ref, mk_ref, mv_ref, nq_ref, dq_ref, sbq_ref, sbk_ref, sbv_ref,
                   kcmp_ref, kslc_ref, kwin_ref, dk_ref, vcmp_ref, vslc_ref, vwin_ref, dv_ref,
                   qi_ref, ki_ref, misc_ref):
    xb = x_ref[0].astype(BF16)

    def main(name):
        a, b = _MAIN_OFF[name]
        return _dot(xb, wm_ref[:, a:b])

    def roped(name, c_ref, s_ref):
        a, b = _ROT_OFF[name]
        return main(name) * c_ref[...] + _dot(xb, wr_ref[:, a:b]) * s_ref[...]

    def store_heads(ref, val, width):
        for h in range(HEADS):
            ref[0, h] = val[:, h * width:(h + 1) * width].astype(ref.dtype)

    cqn = _rms_norm(main('mla_cq'), qg_ref[...]).astype(BF16)
    q = _dot(cqn, wuq_ref[...]) * cq_ref[...] + _dot(cqn, wuqr_ref[...]) * sq_ref[...]
    store_heads(mq_ref, q, 128)
    misc = roped('misc', cd_ref, sd_ref)
    misc_ref[0] = misc
    m_hi, m_lo = _split_bf16(misc)
    ckvn = _rms_norm(main('mla_ckv'), kvg_ref[...]).astype(BF16)
    k = _dot(ckvn, wuk_ref[...]) + _dot(m_hi, pkpe_ref[...])
    store_heads(mk_ref, k, 128)
    store_heads(mv_ref, _dot(ckvn, wuv_ref[...]), 64)
    qscale = HEAD_DIM ** -0.5
    store_heads(nq_ref, roped('nsa_q', c64_ref, s64_ref) * qscale, 64)
    store_heads(dq_ref, roped('dsa_q', c64_ref, s64_ref) * qscale, 64)
    store_heads(sbq_ref, main('sb_q') * qscale, 64)
    store_heads(sbk_ref, main('sb_k'), 64)
    store_heads(sbv_ref, main('sb_v'), 64)
    sk = roped('slab_k', c64_ref, s64_ref)
    for j, ref in enumerate((kcmp_ref, kslc_ref, kwin_ref, dk_ref)):
        ref[0] = sk[:, j * 64:(j + 1) * 64].astype(ref.dtype)
    sv = main('slab_v')
    for j, ref in enumerate((vcmp_ref, vslc_ref, vwin_ref, dv_ref)):
        ref[0] = sv[:, j * 64:(j + 1) * 64].astype(ref.dtype)
    qi_hi, qi_lo = _split_bf16(roped('idx_q', c32_ref, s32_ref))
    qi_ref[0] = (_dot(qi_hi, phi_ref[...]) + _dot(qi_lo, plo_ref[...])).astype(BF16)
    ki_ref[0] = (_dot(m_hi, pkhi_ref[...]) + _dot(m_lo, pklo_ref[...])).astype(BF16)


def _inproj(x, wm, wr, tabs, qg, kvg, mla_w, place, *, tm):
    B, S, D = x.shape
    (c64, s64), (c32, s32), (cd, sd), (cq, sq) = tabs
    wuq, wuqr, wuk, wuv = mla_w

    def const(a):
        return pl.BlockSpec(a.shape, lambda b, s: (0,) * a.ndim)

    def tab(a):
        return pl.BlockSpec((tm, a.shape[1]), lambda b, s: (s, 0))

    def heads_out(w, dt=BF16):
        return (jax.ShapeDtypeStruct((B, HEADS, S, w), dt),
                pl.BlockSpec((1, HEADS, tm, w), lambda b, s: (b, 0, s, 0)))

    def flat_out(w, dt):
        return (jax.ShapeDtypeStruct((B, S, w), dt), pl.BlockSpec((1, tm, w), lambda b, s: (b, s, 0)))

    outs = [heads_out(128), heads_out(128), heads_out(64),
            heads_out(64), heads_out(64),
            heads_out(64), heads_out(64), heads_out(64),
            flat_out(64, F32), flat_out(64, BF16), flat_out(64, BF16), flat_out(64, BF16),
            flat_out(64, F32), flat_out(64, BF16), flat_out(64, BF16), flat_out(64, BF16),
            flat_out(1024, BF16), flat_out(128, BF16), flat_out(128, F32)]
    ins = [x, wm, wr, c64, s64, c32, s32, cd, sd, cq, sq, qg, kvg, wuq, wuqr, wuk, wuv, *place]
    in_specs = [pl.BlockSpec((1, tm, D), lambda b, s: (b, s, 0)), const(wm), const(wr),
                tab(c64), tab(s64), tab(c32), tab(s32), tab(cd), tab(sd), tab(cq), tab(sq),
                const(qg), const(kvg), const(wuq), const(wuqr), const(wuk), const(wuv),
                *[const(p) for p in place]]
    return pl.pallas_call(
        _inproj_kernel,
        out_shape=[o[0] for o in outs],
        grid=(B, S // tm),
        in_specs=in_specs,
        out_specs=[o[1] for o in outs],
        compiler_params=_params("parallel", "arbitrary"),
        name="in_proj",
    )(*ins)


def _softmax_init(n, dv):
    return (jnp.full((n, 1), NEG_INF, F32), jnp.zeros((n, 1), F32), jnp.zeros((n, dv), F32))


def _softmax_step(state, s, mask, v):
    m, l, acc = state
    s = jnp.where(mask, s, NEG_INF)
    m_new = jnp.maximum(m, jnp.max(s, -1, keepdims=True))
    alpha = jnp.exp(m - m_new)
    e = jnp.where(mask, jnp.exp(s - m_new), 0.0)
    l = alpha * l + jnp.sum(e, -1, keepdims=True)
    acc = alpha * acc + _dot(e.astype(BF16), v)
    return m_new, l, acc


def _softmax_finish(state):
    _, l, acc = state
    return acc / jnp.maximum(l, 1e-30)


def _n_chunks(q0):
    return (q0 + QB + CK - 1) // CK


def _mla_kernel(q_ref, k_ref, v_ref, o_ref):
    q0 = pl.program_id(1) * QB
    tpos = q0 + lax.broadcasted_iota(jnp.int32, (QB, 1), 0)
    for h in range(HEADS):
        q = q_ref[0, h]

        def body(c, state, h=h, q=q):
            ks = pl.multiple_of(c * CK, CK)
            s = _dot_nt(q, k_ref[0, h, pl.ds(ks, CK), :])
            kpos = ks + lax.broadcasted_iota(jnp.int32, (1, CK), 1)
            return _softmax_step(state, s, kpos <= tpos, v_ref[0, h, pl.ds(ks, CK), :])

        state = lax.fori_loop(0, _n_chunks(q0), body, _softmax_init(QB, MLA_D_V))
        o_ref[0, :, h * 64:(h + 1) * 64] = _softmax_finish(state)


def _mla(q, k, v):
    B, H, S, _ = q.shape
    return pl.pallas_call(
        _mla_kernel,
        out_shape=jax.ShapeDtypeStruct((B, S, D_GROUP), F32),
        grid=(B, S // QB),
        in_specs=[pl.BlockSpec((1, H, QB, 128), lambda b, i: (b, 0, i, 0)),
                  pl.BlockSpec((1, H, S, 128), lambda b, i: (b, 0, 0, 0)),
                  pl.BlockSpec((1, H, S, 64), lambda b, i: (b, 0, 0, 0))],
        out_specs=pl.BlockSpec((1, QB, D_GROUP), lambda b, i: (b, i, 0)),
        compiler_params=_params("parallel", "arbitrary"),
        name="mla_attn",
    )(q, k, v)


def _sb_kernel(q_ref, k_ref, v_ref, tri_ref, o_ref):
    i = pl.program_id(1)
    q0 = i * QB
    tpos = q0 + lax.broadcasted_iota(jnp.int32, (QB, 1), 0)
    tri = tri_ref[...]
    for h in range(HEADS):
        q = q_ref[0, h]

        def body(r, carry, h=h, q=q):
            run, acc = carry
            c = i - r
            ks = pl.multiple_of(c * QB, QB)
            z = _dot_nt(q, k_ref[0, h, pl.ds(ks, QB), :])
            kpos = ks + lax.broadcasted_iota(jnp.int32, (1, QB), 1)
            mask = kpos < tpos
            sp = jnp.maximum(z, 0.0) + jnp.log(1.0 + jnp.exp(-jnp.abs(z)))
            log_1m = jnp.where(mask, -sp, 0.0)
            hi, lo = _split_bf16(log_1m)
            after = _dot(jnp.concatenate([hi, lo], axis=1), tri) + run
            a = jnp.where(mask, jnp.exp(z - sp + after), 0.0)
            acc = acc + _dot(a.astype(BF16), v_ref[0, h, pl.ds(ks, QB), :])
            return run + jnp.sum(log_1m, -1, keepdims=True), acc

        _, acc = lax.fori_loop(0, i + 1, body, (jnp.zeros((QB, 1), F32), jnp.zeros((QB, HEAD_DIM), F32)))
        o_ref[0, :, h * 64:(h + 1) * 64] = acc


def _sb(q, k, v):
    B, H, S, _ = q.shape
    u = np.triu(np.ones((QB, QB), np.float32), 1).T
    tri = jnp.asarray(np.concatenate([u, u], 0), BF16)
    return pl.pallas_call(
        _sb_kernel,
        out_shape=jax.ShapeDtypeStruct((B, S, D_GROUP), F32),
        grid=(B, S // QB),
        in_specs=[pl.BlockSpec((1, H, QB, 64), lambda b, i: (b, 0, i, 0)),
                  pl.BlockSpec((1, H, S, 64), lambda b, i: (b, 0, 0, 0)),
                  pl.BlockSpec((1, H, S, 64), lambda b, i: (b, 0, 0, 0)),
                  pl.BlockSpec((2 * QB, QB), lambda b, i: (0, 0))],
        out_specs=pl.BlockSpec((1, QB, D_GROUP), lambda b, i: (b, i, 0)),
        compiler_params=_params("parallel", "arbitrary"),
        name="sb_attn",
    )(q, k, v, tri)


def _ordered_to_f32(c):
    bits = c ^ ((c >> 31) & jnp.int32(0x7FFFFFFF))
    return pltpu.bitcast(bits, F32)


def _dsa_kernel(q_ref, k_ref, v_ref, qi_ref, ki_ref, misc_ref, o_ref, sc_ref, *, seq, n_top):
    q0 = pl.program_id(1) * QB
    nck = _n_chunks(q0)
    tpos = q0 + lax.broadcasted_iota(jnp.int32, (QB, 1), 0)
    w = misc_ref[0][:, MISC_IDXW:MISC_IDXW + DSA_IDX_HEADS] * (DSA_IDX_HEADS * DSA_IDX_DIM) ** -0.5

    def kpos_of(ks):
        return ks + lax.broadcasted_iota(jnp.int32, (1, CK), 1)

    def score_body(c, _):
        ks = pl.multiple_of(c * CK, CK)
        kc = ki_ref[0, pl.ds(ks, CK), :]
        sc = jnp.zeros((QB, CK), F32)
        for h in range(DSA_IDX_HEADS):
            r = _dot_nt(qi_ref[0, :, h * 128:(h + 1) * 128], kc)
            sc = sc + w[:, h:h + 1] * jnp.maximum(r, 0.0)
        sc = jnp.where(sc == 0.0, 0.0, sc)
        sc_ref[:, pl.ds(ks, CK)] = jnp.where(kpos_of(ks) <= tpos, sc, NEG_INF)
        return 0

    lax.fori_loop(0, nck, score_body, 0)

    n_beyond = (seq - nck * CK).astype(F32)

    def count(pred_fn):
        def cb(c, acc):
            ks = pl.multiple_of(c * CK, CK)
            hit = pred_fn(sc_ref[:, pl.ds(ks, CK)], kpos_of(ks))
            return acc + jnp.sum(jnp.where(hit, 1.0, 0.0), -1, keepdims=True)
        return lax.fori_loop(0, nck, cb, jnp.zeros((QB, 1), F32))

    def count_ge(thr):
        return count(lambda s, kp: s >= thr) + jnp.where(thr <= NEG_INF, n_beyond, 0.0)

    def thr_body(it, t_int):
        cand = t_int ^ lax.shift_left(jnp.int32(1), 31 - it)
        ok = count_ge(_ordered_to_f32(cand)) >= n_top
        return jnp.where(ok, cand, t_int)

    t_int = lax.fori_loop(0, 32, thr_body, jnp.full((QB, 1), -2 ** 31, jnp.int32))
    thr = _ordered_to_f32(t_int)
    thr_up = _ordered_to_f32(t_int + 1)
    need = n_top - count_ge(thr_up)
    tie_beyond = thr <= NEG_INF

    def idx_body(it, m):
        cand = m + lax.shift_left(jnp.int32(1), (seq.bit_length() - 2) - it)
        below = count(lambda s, kp: (s >= thr) & (s < thr_up) & (kp < cand))
        below = below + jnp.where(tie_beyond, jnp.maximum(cand - nck * CK, 0).astype(F32), 0.0)
        return jnp.where(below < need, cand, m)

    m_idx = lax.fori_loop(0, seq.bit_length() - 1, idx_body, jnp.zeros((QB, 1), jnp.int32))

    def attn_body(c, states):
        ks = pl.multiple_of(c * CK, CK)
        kp = kpos_of(ks)
        s_idx = sc_ref[:, pl.ds(ks, CK)]
        sel = (s_idx >= thr_up) | ((s_idx >= thr) & (kp <= m_idx))
        mask = sel & (kp <= tpos)
        kc = k_ref[0, pl.ds(ks, CK), :]
        vc = v_ref[0, pl.ds(ks, CK), :]
        return tuple(_softmax_step(states[h], _dot_nt(q_ref[0, h], kc), mask, vc) for h in range(HEADS))

    states = lax.fori_loop(0, nck, attn_body, tuple(_softmax_init(QB, HEAD_DIM) for _ in range(HEADS)))
    for h in range(HEADS):
        o_ref[0, :, h * 64:(h + 1) * 64] = _softmax_finish(states[h])


def _dsa(q, k, v, qi, ki, misc):
    B, H, S, _ = q.shape
    n_top = min(DSA_TOPK, S // 4)
    return pl.pallas_call(
        functools.partial(_dsa_kernel, seq=S, n_top=n_top),
        out_shape=jax.ShapeDtypeStruct((B, S, D_GROUP), F32),
        grid=(B, S // QB),
        in_specs=[pl.BlockSpec((1, H, QB, 64), lambda b, i: (b, 0, i, 0)),
                  pl.BlockSpec((1, S, 64), lambda b, i: (b, 0, 0)),
                  pl.BlockSpec((1, S, 64), lambda b, i: (b, 0, 0)),
                  pl.BlockSpec((1, QB, 1024), lambda b, i: (b, i, 0)),
                  pl.BlockSpec((1, S, 128), lambda b, i: (b, 0, 0)),
                  pl.BlockSpec((1, QB, 128), lambda b, i: (b, i, 0))],
        out_specs=pl.BlockSpec((1, QB, D_GROUP), lambda b, i: (b, i, 0)),
        scratch_shapes=[pltpu.VMEM((QB, S), F32)],
        compiler_params=_params("parallel", "arbitrary"),
        name="dsa_attn",
    )(q, k, v, qi, ki, misc)


def _gelu_tanh(x):
    return 0.5 * x * (1.0 + jnp.tanh(np.sqrt(2.0 / np.pi) * (x + 0.044715 * (x * x * x))))


def _nsa_cmp_kernel(kg_ref, vg_ref, pos_ref, wk1_ref, wk2_ref, wv1_ref, wv2_ref, kc_ref, vc_ref):
    def compress(g_ref, w1_ref, w2_ref):
        top = _dot((g_ref[0] + pos_ref[0:1, :]).astype(BF16), w1_ref[0])
        bot = _dot((g_ref[0] + pos_ref[1:2, :]).astype(BF16), w1_ref[1])
        pre = top + pltpu.roll(bot, bot.shape[0] - 1, 0)
        return _dot(_gelu_tanh(pre).astype(BF16), w2_ref[...])

    kc_ref[0] = compress(kg_ref, wk1_ref, wk2_ref).astype(BF16)
    vc_ref[0] = compress(vg_ref, wv1_ref, wv2_ref).astype(BF16)


def _nsa_compress(kcmp, vcmp, pos, wk1, wk2, wv1, wv2):
    B, S, _ = kcmp.shape
    ng = S // NSA_CMP_STRIDE
    gw = NSA_CMP_STRIDE * HEAD_DIM
    kg = kcmp.reshape(B, ng, gw)
    vg = vcmp.reshape(B, ng, gw)

    def const(a):
        return pl.BlockSpec(a.shape, lambda b: (0,) * a.ndim)

    ins = [kg, vg, pos.reshape(2, gw), wk1.reshape(2, gw, HEAD_DIM).astype(BF16), wk2.astype(BF16),
           wv1.reshape(2, gw, HEAD_DIM).astype(BF16), wv2.astype(BF16)]
    blk = pl.BlockSpec((1, ng, gw), lambda b: (b, 0, 0))
    out = pl.BlockSpec((1, ng, HEAD_DIM), lambda b: (b, 0, 0))
    return pl.pallas_call(
        _nsa_cmp_kernel,
        out_shape=[jax.ShapeDtypeStruct((B, ng, HEAD_DIM), BF16)] * 2,
        grid=(B,),
        in_specs=[blk, blk] + [const(a) for a in ins[2:]],
        out_specs=[out, out],
        compiler_params=_params("parallel"),
        name="nsa_compress",
    )(*ins)


def _nsa_kernel(q_ref, kc_ref, vc_ref, ks_ref, vs_ref, kw_ref, vw_ref, misc_ref, ovl_ref, o_ref,
                *, seq):
    i = pl.program_id(1)
    q0 = i * QB
    n_slc = seq // NSA_SEL_LEN
    n_cmp = kc_ref.shape[1]
    tpos = q0 + lax.broadcasted_iota(jnp.int32, (QB, 1), 0)
    qs = [q_ref[0, h] for h in range(HEADS)]

    kc = kc_ref[0]
    vc = vc_ref[0]
    last_tok = lax.broadcasted_iota(jnp.int32, (1, n_cmp), 1) * NSA_CMP_STRIDE + (NSA_CMP_LEN - 1)
    cmask = last_tok <= tpos
    o_cmp = []
    p_sum = jnp.zeros((QB, n_cmp), F32)
    for h in range(HEADS):
        s = jnp.where(cmask, _dot_nt(qs[h], kc), NEG_INF)
        e = jnp.where(cmask, jnp.exp(s - jnp.max(s, -1, keepdims=True)), 0.0)
        p = e / jnp.maximum(jnp.sum(e, -1, keepdims=True), 1e-30)
        p_sum = p_sum + p
        o_cmp.append(_dot(p.astype(BF16), vc))

    hi, lo = _split_bf16(p_sum)
    imp = _dot_nt(ovl_ref[...], jnp.concatenate([hi, lo], axis=1))
    jb = lax.broadcasted_iota(jnp.int32, (n_slc, 1), 0)
    cur = (q0 + lax.broadcasted_iota(jnp.int32, (1, QB), 1)) >> SEL_SHIFT
    forced = (jb < NSA_N_INIT) | ((jb <= cur) & (jb > cur - NSA_N_LOCAL))
    imp = jnp.where(forced, FORCE_SCORE, jnp.where(jb <= cur, imp, NEG_INF))
    rank = jnp.zeros((n_slc, QB), F32)
    for r in range(n_slc):
        row = imp[r:r + 1, :]
        beats = (row > imp) | ((row == imp) & (r < jb))
        rank = rank + jnp.where(beats, 1.0, 0.0)
    sel_t = jnp.where(rank < min(NSA_N_SEL, n_slc), 1.0, 0.0)
    sel_t = jnp.concatenate([sel_t, jnp.zeros((LANES - n_slc, QB), F32)], axis=0)
    sel = sel_t.T.astype(BF16)

    def slc_body(c, states):
        ks = pl.multiple_of(c * CK, CK)
        kp = ks + lax.broadcasted_iota(jnp.int32, (1, CK), 1)
        expand = jnp.where(lax.broadcasted_iota(jnp.int32, (LANES, 1), 0) == (kp >> SEL_SHIFT), 1.0, 0.0)
        chosen = _dot(sel, expand.astype(BF16)) > 0.5
        mask = chosen & (kp <= tpos)
        kk = ks_ref[0, pl.ds(ks, CK), :]
        vv = vs_ref[0, pl.ds(ks, CK), :]
        return tuple(_softmax_step(states[h], _dot_nt(qs[h], kk), mask, vv) for h in range(HEADS))

    states = lax.fori_loop(0, _n_chunks(q0), slc_body,
                           tuple(_softmax_init(QB, HEAD_DIM) for _ in range(HEADS)))
    o_slc = [_softmax_finish(states[h]) for h in range(HEADS)]

    n_win = NSA_WINDOW + QB
    ws = pl.multiple_of(jnp.maximum(q0 - NSA_WINDOW, 0), QB)
    wpos = ws + lax.broadcasted_iota(jnp.int32, (1, n_win), 1)
    dist = tpos - wpos
    wmask = (dist >= 0) & (dist < NSA_WINDOW)
    kw = kw_ref[0, pl.ds(ws, n_win), :]
    vw = vw_ref[0, pl.ds(ws, n_win), :]
    o_win = []
    for h in range(HEADS):
        state = _softmax_step(_softmax_init(QB, HEAD_DIM), _dot_nt(qs[h], kw), wmask, vw)
        o_win.append(_softmax_finish(state))

    gate = jax.nn.sigmoid(misc_ref[0][:, MISC_GATE:MISC_GATE + 3 * HEADS])
    for h in range(HEADS):
        o_ref[0, :, h * 64:(h + 1) * 64] = (gate[:, h:h + 1] * o_cmp[h]
                                             + gate[:, HEADS + h:HEADS + h + 1] * o_slc[h]
                                             + gate[:, 2 * HEADS + h:2 * HEADS + h + 1] * o_win[h])


def _nsa_overlap(seq):
    ng = seq // NSA_CMP_STRIDE
    n_slc = seq // NSA_SEL_LEN
    n = np.arange(ng)
    first, last = n * NSA_CMP_STRIDE, n * NSA_CMP_STRIDE + NSA_CMP_LEN - 1
    start = np.arange(n_slc) * NSA_SEL_LEN
    ovl = ((first[None, :] <= start[:, None] + NSA_SEL_LEN - 1) & (last[None, :] >= start[:, None]))
    ovl = ovl & (last[None, :] < seq)
    ovl = ovl.astype(np.float32)
    return jnp.asarray(np.concatenate([ovl, ovl], 1), BF16)


def _nsa(q, kc, vc, k_slc, v_slc, k_win, v_win, misc):
    B, H, S, _ = q.shape
    ng = kc.shape[1]
    ovl = _nsa_overlap(S)
    full = lambda w: pl.BlockSpec((1, S, w), lambda b, i: (b, 0, 0))
    return pl.pallas_call(
        functools.partial(_nsa_kernel, seq=S),
        out_shape=jax.ShapeDtypeStruct((B, S, D_GROUP), F32),
        grid=(B, S // QB),
        in_specs=[pl.BlockSpec((1, H, QB, 64), lambda b, i: (b, 0, i, 0)),
                  pl.BlockSpec((1, ng, 64), lambda b, i: (b, 0, 0)),
                  pl.BlockSpec((1, ng, 64), lambda b, i: (b, 0, 0)),
                  full(64), full(64), full(64), full(64),
                  pl.BlockSpec((1, QB, 128), lambda b, i: (b, i, 0)),
                  pl.BlockSpec(ovl.shape, lambda b, i: (0, 0))],
        out_specs=pl.BlockSpec((1, QB, D_GROUP), lambda b, i: (b, i, 0)),
        compiler_params=_params("parallel", "arbitrary"),
        name="nsa_attn",
    )(q, kc, vc, k_slc, v_slc, k_win, v_win, misc, ovl)


def _outproj_kernel(x_ref, ya_ref, yb_ref, yc_ref, yd_ref, gg_ref, wo_ref, g_ref, b_ref, o_ref):
    acc = None
    for gi, y_ref in enumerate((ya_ref, yb_ref, yc_ref, yd_ref)):
        lo, hi = gi * D_GROUP, (gi + 1) * D_GROUP
        yn = _rms_norm(y_ref[...], gg_ref[:, lo:hi]).astype(BF16)
        part = _dot(yn, wo_ref[lo:hi, :])
        acc = part if acc is None else acc + part
    o_ref[...] = _layer_norm(DN_ALPHA * x_ref[...] + acc, g_ref[...], b_ref[...])


def _outproj(x2, ys, gg, wo, g, b, *, tm):
    n, d = x2.shape
    row = lambda w: pl.BlockSpec((tm, w), lambda i: (i, 0))
    const = lambda a: pl.BlockSpec(a.shape, lambda i: (0, 0))
    return pl.pallas_call(
        _outproj_kernel,
        out_shape=jax.ShapeDtypeStruct((n, d), F32),
        grid=(n // tm,),
        in_specs=[row(d)] + [row(D_GROUP)] * 4 + [const(gg), const(wo), const(g), const(b)],
        out_specs=row(d),
        compiler_params=_params("parallel"),
        name="out_proj",
    )(x2, *ys, gg, wo, g, b)


def _mixer(x, w_in, q_norm, w_uq, kv_norm, w_ukv, cmp_pos, wk1, wk2, wv1, wv2, tabs, place, maps):
    B, S, D = x.shape
    main_cols, rot_cols = maps
    wm = _gather_cols(w_in, main_cols).astype(BF16)
    wr = _gather_cols(w_in, rot_cols).astype(BF16)
    (mq, mk, mv, nq, dq, sbq, sbk, sbv, kcmp, kslc, kwin, dk, vcmp, vslc, vwin, dv,
     qi, ki, misc) = _inproj(x, wm, wr, tabs, q_norm[None, :], kv_norm[None, :],
                             _mla_weights(w_uq, w_ukv), place, tm=min(512, S))
    y_a = _mla(mq, mk, mv)
    kc, vc = _nsa_compress(kcmp, vcmp, cmp_pos, wk1, wk2, wv1, wv2)
    y_b = _nsa(nq, kc, vc, kslc, vslc, kwin, vwin, misc)
    y_c = _dsa(dq, dk, dv, qi, ki, misc)
    y_d = _sb(sbq, sbk, sbv)
    return y_a, y_b, y_c, y_d


def kernel(x, ln1_g, ln1_b, ffn1_w1, ffn1_w3, ffn1_w2, w_in, mla_q_norm, mla_w_uq, mla_kv_norm,
           mla_w_ukv, nsa_cmp_pos, nsa_cmp_wk1, nsa_cmp_wk2, nsa_cmp_wv1, nsa_cmp_wv2, group_norm_g,
           w_out, ln2_g, ln2_b, ffn2_w1, ffn2_w3, ffn2_w2, ln3_g, ln3_b):
    B, S, D = x.shape
    n = B * S
    tabs = _rope_tables(S)
    place = _placement_constants()
    maps = _column_maps()
    tm_ffn = min(1024, n)
    x2 = x.reshape(n, D)
    for l in range(DEPTH):
        x2 = _ffn_ln(x2, ffn1_w1[l].astype(BF16), ffn1_w3[l].astype(BF16), ffn1_w2[l].astype(BF16),
                     ln1_g[l][None, :], ln1_b[l][None, :], tm=tm_ffn, tf=256)
        ys = _mixer(x2.reshape(B, S, D), w_in[l], mla_q_norm[l], mla_w_uq[l], mla_kv_norm[l],
                    mla_w_ukv[l], nsa_cmp_pos[l], nsa_cmp_wk1[l], nsa_cmp_wk2[l], nsa_cmp_wv1[l],
                    nsa_cmp_wv2[l], tabs, place, maps)
        x2 = _outproj(x2, [y.reshape(n, D_GROUP) for y in ys], group_norm_g[l][None, :],
                      w_out[l].astype(BF16), ln2_g[l][None, :], ln2_b[l][None, :], tm=min(512, n))
        x2 = _ffn_ln(x2, ffn2_w1[l].astype(BF16), ffn2_w3[l].astype(BF16), ffn2_w2[l].astype(BF16),
                     ln3_g[l][None, :], ln3_b[l][None, :], tm=tm_ffn, tf=256)
    return x2.reshape(B, S, D)
```

```python
import functools

import numpy as np
import jax
import jax.numpy as jnp
from jax import lax
from jax.experimental import pallas as pl
from jax.experimental.pallas import tpu as pltpu

F32 = jnp.float32
BF16 = jnp.bfloat16

D_MODEL = 1024
DEPTH = 2
HEADS = 4
HEAD_DIM = 64
D_GROUP = HEADS * HEAD_DIM
N_GROUPS = 4
D_FF = 2816
ROPE_THETA = 10000.0
MLA_Q_RANK = 256
MLA_KV_RANK = 128
MLA_D_NOPE = 64
MLA_D_ROPE = 32
MLA_D_V = 64
NSA_CMP_LEN = 32
NSA_CMP_STRIDE = 16
NSA_SEL_LEN = 64
NSA_N_SEL = 8
NSA_N_INIT = 1
NSA_N_LOCAL = 2
NSA_WINDOW = 512
SEL_SHIFT = NSA_SEL_LEN.bit_length() - 1
DSA_TOPK = 256
DSA_IDX_HEADS = 8
DSA_IDX_DIM = 32
DN_ALPHA = (2.0 * DEPTH) ** 0.25
LN_EPS = 1e-5
RMS_EPS = 1e-6
NEG_INF = -1e30
FORCE_SCORE = 1e4

LANES = 128
QB = 128
CK = 256
VMEM_LIMIT = 48 * 1024 * 1024

MISC_KROPE = 0
MISC_IDXK = 32
MISC_IDXW = 64
MISC_GATE = 72

NT_DIMS = (((1,), (1,)), ((), ()))


def _dot(a, b):
    return jnp.dot(a, b, preferred_element_type=F32)


def _dot_nt(a, b):
    return lax.dot_general(a, b, NT_DIMS, preferred_element_type=F32)


def _split_bf16(x):
    hi = x.astype(BF16)
    lo = (x - hi.astype(F32)).astype(BF16)
    return hi, lo


def _layer_norm(y, g, b):
    mu = jnp.mean(y, -1, keepdims=True)
    d = y - mu
    var = jnp.mean(d * d, -1, keepdims=True)
    return d * lax.rsqrt(var + LN_EPS) * g + b


def _rms_norm(y, g):
    return y * lax.rsqrt(jnp.mean(y * y, -1, keepdims=True) + RMS_EPS) * g


def _params(*sem):
    return pltpu.CompilerParams(dimension_semantics=sem, vmem_limit_bytes=VMEM_LIMIT)


def _ffn_kernel(x_ref, w1_ref, w3_ref, w2_ref, g_ref, b_ref, o_ref, xb_ref, acc_ref):
    j = pl.program_id(1)

    @pl.when(j == 0)
    def _():
        xb_ref[...] = x_ref[...].astype(BF16)
        acc_ref[...] = jnp.zeros_like(acc_ref)

    xb = xb_ref[...]
    h = _dot(xb, w1_ref[...])
    u = _dot(xb, w3_ref[...])
    a = h * jax.nn.sigmoid(h) * u
    acc_ref[...] += _dot(a.astype(BF16), w2_ref[...])

    @pl.when(j == pl.num_programs(1) - 1)
    def _():
        y = DN_ALPHA * x_ref[...] + 0.5 * acc_ref[...]
        o_ref[...] = _layer_norm(y, g_ref[...], b_ref[...])


def _ffn_ln(x2, w1, w3, w2, g, b, *, tm, tf):
    n, d = x2.shape
    dff = w1.shape[1]
    return pl.pallas_call(
        _ffn_kernel,
        out_shape=jax.ShapeDtypeStruct((n, d), F32),
        grid=(n // tm, dff // tf),
        in_specs=[
            pl.BlockSpec((tm, d), lambda i, j: (i, 0)),
            pl.BlockSpec((d, tf), lambda i, j: (0, j)),
            pl.BlockSpec((d, tf), lambda i, j: (0, j)),
            pl.BlockSpec((tf, d), lambda i, j: (j, 0)),
            pl.BlockSpec((1, d), lambda i, j: (0, 0)),
            pl.BlockSpec((1, d), lambda i, j: (0, 0)),
        ],
        out_specs=pl.BlockSpec((tm, d), lambda i, j: (i, 0)),
        scratch_shapes=[pltpu.VMEM((tm, d), BF16), pltpu.VMEM((tm, d), F32)],
        compiler_params=_params("parallel", "arbitrary"),
        name="ffn_ln",
    )(x2, w1, w3, w2, g, b)


_MAIN_ORDER = (
    ('mla_cq', 256), ('nsa_q', 256), ('dsa_q', 256), ('idx_q', 256),
    ('sb_q', 256), ('sb_k', 256), ('sb_v', 256),
    ('slab_k', 256),
    ('slab_v', 256),
    ('mla_ckv', 128),
    ('misc', 128),
)
_MAIN_OFF = {}
_o = 0
for _n, _w in _MAIN_ORDER:
    _MAIN_OFF[_n] = (_o, _o + _w)
    _o += _w
N_MAIN = _o
_ROT_ORDER = (('nsa_q', 256), ('dsa_q', 256), ('idx_q', 256), ('slab_k', 256), ('misc', 128))
_ROT_OFF = {}
_o = 0
for _n, _w in _ROT_ORDER:
    _ROT_OFF[_n] = (_o, _o + _w)
    _o += _w
N_ROT = _o

_IN_SPLITS = (
    ('mla_cq', MLA_Q_RANK), ('mla_ckv', MLA_KV_RANK), ('mla_krope', MLA_D_ROPE),
    ('nsa_q', D_GROUP), ('nsa_k_cmp', HEAD_DIM), ('nsa_v_cmp', HEAD_DIM),
    ('nsa_k_slc', HEAD_DIM), ('nsa_v_slc', HEAD_DIM), ('nsa_k_win', HEAD_DIM),
    ('nsa_v_win', HEAD_DIM), ('nsa_gate', 3 * HEADS),
    ('dsa_q', D_GROUP), ('dsa_k', HEAD_DIM), ('dsa_v', HEAD_DIM),
    ('idx_q', DSA_IDX_HEADS * DSA_IDX_DIM), ('idx_k', DSA_IDX_DIM), ('idx_w', DSA_IDX_HEADS),
    ('sb_q', D_GROUP), ('sb_k', D_GROUP), ('sb_v', D_GROUP),
)
_SRC = {}
_o = 0
for _n, _w in _IN_SPLITS:
    _SRC[_n] = np.arange(_o, _o + _w)
    _o += _w
D_IN = _o


def _swap_halves(width, dim):
    idx = np.arange(width)
    return (idx // dim) * dim + (idx % dim + dim // 2) % dim


def _column_maps():
    pieces = {
        'mla_cq': [_SRC['mla_cq']], 'nsa_q': [_SRC['nsa_q']], 'dsa_q': [_SRC['dsa_q']],
        'idx_q': [_SRC['idx_q']], 'sb_q': [_SRC['sb_q']], 'sb_k': [_SRC['sb_k']], 'sb_v': [_SRC['sb_v']],
        'slab_k': [_SRC['nsa_k_cmp'], _SRC['nsa_k_slc'], _SRC['nsa_k_win'], _SRC['dsa_k']],
        'slab_v': [_SRC['nsa_v_cmp'], _SRC['nsa_v_slc'], _SRC['nsa_v_win'], _SRC['dsa_v']],
        'mla_ckv': [_SRC['mla_ckv']],
        'misc': [_SRC['mla_krope'], _SRC['idx_k'], _SRC['idx_w'], _SRC['nsa_gate'],
                 -np.ones(LANES - MISC_GATE - 3 * HEADS, np.int64)],
    }
    main = np.concatenate([np.concatenate(pieces[n]) for n, _ in _MAIN_ORDER])
    rot_dim = {'nsa_q': HEAD_DIM, 'dsa_q': HEAD_DIM, 'idx_q': DSA_IDX_DIM, 'slab_k': HEAD_DIM}
    rot = []
    for n, w in _ROT_ORDER:
        cols = np.concatenate(pieces[n])
        if n == 'misc':
            r = -np.ones(w, np.int64)
            r[:64] = cols[:64][_swap_halves(64, 32)]
        else:
            r = cols[_swap_halves(w, rot_dim[n])]
        rot.append(r)
    return main, np.concatenate(rot)


def _gather_cols(w, cols):
    out = jnp.take(w, np.maximum(cols, 0), axis=1)
    return jnp.where(jnp.asarray(cols >= 0)[None, :], out, 0.0)


def _rope_tables(seq):
    def base(dim):
        inv = ROPE_THETA ** (-jnp.arange(0, dim, 2, dtype=F32) / dim)
        ang = jnp.arange(seq, dtype=F32)[:, None] * inv[None, :]
        c, s = jnp.cos(ang), jnp.sin(ang)
        return jnp.concatenate([c, c], -1), jnp.concatenate([-s, s], -1)

    c64, s64 = base(HEAD_DIM)
    c32, s32 = base(DSA_IDX_DIM)
    t64 = (jnp.tile(c64, (1, 4)), jnp.tile(s64, (1, 4)))
    t32 = (jnp.tile(c32, (1, 8)), jnp.tile(s32, (1, 8)))
    ones = jnp.ones((seq, LANES - 64), F32)
    td = (jnp.concatenate([c32, c32, ones], -1), jnp.concatenate([s32, s32, 0.0 * ones], -1))
    scale = (MLA_D_NOPE + MLA_D_ROPE) ** -0.5
    cq = jnp.concatenate([jnp.ones((seq, 64), F32), c32, jnp.zeros((seq, 32), F32)], -1) * scale
    sq = jnp.concatenate([jnp.zeros((seq, 64), F32), s32, jnp.zeros((seq, 32), F32)], -1) * scale
    tq = (jnp.tile(cq, (1, 4)), jnp.tile(sq, (1, 4)))
    return t64, t32, td, tq


def _placement_constants():
    p_hi = np.zeros((256, 1024), np.float32)
    p_lo = np.zeros((256, 1024), np.float32)
    for h in range(DSA_IDX_HEADS):
        for d in range(DSA_IDX_DIM):
            p_hi[h * 32 + d, h * 128 + d] = 1
            p_hi[h * 32 + d, h * 128 + 64 + d] = 1
            p_lo[h * 32 + d, h * 128 + 32 + d] = 1
    pk_hi = np.zeros((128, 128), np.float32)
    pk_lo = np.zeros((128, 128), np.float32)
    for d in range(DSA_IDX_DIM):
        pk_hi[MISC_IDXK + d, d] = 1
        pk_hi[MISC_IDXK + d, 32 + d] = 1
        pk_lo[MISC_IDXK + d, 64 + d] = 1
    pk_pe = np.zeros((128, 512), np.float32)
    for h in range(HEADS):
        for d in range(MLA_D_ROPE):
            pk_pe[MISC_KROPE + d, h * 128 + 64 + d] = 1
    return tuple(jnp.asarray(a, BF16) for a in (p_hi, p_lo, pk_hi, pk_lo, pk_pe))


def _mla_weights(w_uq, w_ukv):
    dq = MLA_D_NOPE + MLA_D_ROPE
    cols_q = -np.ones(512, np.int64)
    cols_qr = -np.ones(512, np.int64)
    cols_k = -np.ones(512, np.int64)
    cols_v = np.zeros(256, np.int64)
    for h in range(HEADS):
        cols_q[h * 128:h * 128 + dq] = h * dq + np.arange(dq)
        pe = h * dq + MLA_D_NOPE + np.arange(MLA_D_ROPE)
        cols_qr[h * 128 + 64:h * 128 + 96] = pe[_swap_halves(32, 32)]
        cols_k[h * 128:h * 128 + 64] = h * 128 + np.arange(64)
        cols_v[h * 64:(h + 1) * 64] = h * 128 + 64 + np.arange(64)
    return (_gather_cols(w_uq, cols_q).astype(BF16), _gather_cols(w_uq, cols_qr).astype(BF16),
            _gather_cols(w_ukv, cols_k).astype(BF16), jnp.take(w_ukv, cols_v, axis=1).T.astype(BF16))


def _inproj_kernel(x_ref, wm_ref, wr_ref, wt_ref, c64_ref, s64_ref, c32_ref, s32_ref, cd_ref, sd_ref,
                   cq_ref, sq_ref, qg_ref, kvg_ref, wuq_ref, wuqr_ref, wuk_ref, wuvt_ref,
                   phi_ref, plo_ref, pkhi_ref, pklo_ref, pkpe_ref,
                   mq_ref, mk_ref, mvt_ref, nq_ref, dq_ref, sbq_ref, sbk_ref, sbvt_ref,
                   kcmp_ref, kslc_ref, kwin_ref, dk_ref, vcmp_ref, vslct_ref, vwint_ref, dvt_ref,
                   qi_ref, ki_ref, misc_ref):
    xb = x_ref[0].astype(BF16)

    def main(name):
        a, b = _MAIN_OFF[name]
        return _dot(xb, wm_ref[:, a:b])

    def roped(name, c_ref, s_ref):
        a, b = _ROT_OFF[name]
        return main(name) * c_ref[...] + _dot(xb, wr_ref[:, a:b]) * s_ref[...]

    def store_heads(ref, val, width):
        for h in range(HEADS):
            ref[0, h] = val[:, h * width:(h + 1) * width].astype(ref.dtype)

    cqn = _rms_norm(main('mla_cq'), qg_ref[...]).astype(BF16)
    q = _dot(cqn, wuq_ref[...]) * cq_ref[...] + _dot(cqn, wuqr_ref[...]) * sq_ref[...]
    store_heads(mq_ref, q, 128)
    misc = roped('misc', cd_ref, sd_ref)
    misc_ref[0] = misc
    m_hi, m_lo = _split_bf16(misc)
    ckvn = _rms_norm(main('mla_ckv'), kvg_ref[...]).astype(BF16)
    k = _dot(ckvn, wuk_ref[...]) + _dot(m_hi, pkpe_ref[...])
    store_heads(mk_ref, k, 128)
    mvt = _dot_nt(wuvt_ref[...], ckvn)
    for h in range(HEADS):
        mvt_ref[0, h] = mvt[h * 64:(h + 1) * 64, :].astype(BF16)
    qscale = HEAD_DIM ** -0.5
    store_heads(nq_ref, roped('nsa_q', c64_ref, s64_ref) * qscale, 64)
    store_heads(dq_ref, roped('dsa_q', c64_ref, s64_ref) * qscale, 64)
    store_heads(sbq_ref, main('sb_q') * qscale, 64)
    store_heads(sbk_ref, main('sb_k'), 64)
    sk = roped('slab_k', c64_ref, s64_ref)
    for j, ref in enumerate((kcmp_ref, kslc_ref, kwin_ref, dk_ref)):
        ref[0] = sk[:, j * 64:(j + 1) * 64].astype(ref.dtype)
    vcmp_ref[0] = main('slab_v')[:, 0:64]
    vt = _dot_nt(wt_ref[...], xb)
    for j, ref in ((1, vslct_ref), (2, vwint_ref), (3, dvt_ref)):
        ref[0] = vt[j * 64:(j + 1) * 64, :].astype(BF16)
    for h in range(HEADS):
        sbvt_ref[0, h] = vt[256 + h * 64:256 + (h + 1) * 64, :].astype(BF16)
    qi_hi, qi_lo = _split_bf16(roped('idx_q', c32_ref, s32_ref))
    qi = _dot(qi_hi, phi_ref[...]) + _dot(qi_lo, plo_ref[...])
    for h in range(DSA_IDX_HEADS):
        qi_ref[0, h] = qi[:, h * 128:(h + 1) * 128].astype(BF16)
    ki_ref[0] = (_dot(m_hi, pkhi_ref[...]) + _dot(m_lo, pklo_ref[...])).astype(BF16)


def _inproj(x, wm, wr, wt, tabs, qg, kvg, mla_w, place, *, tm):
    B, S, D = x.shape
    (c64, s64), (c32, s32), (cd, sd), (cq, sq) = tabs
    wuq, wuqr, wuk, wuvt = mla_w

    def const(a):
        return pl.BlockSpec(a.shape, lambda b, s: (0,) * a.ndim)

    def tab(a):
        return pl.BlockSpec((tm, a.shape[1]), lambda b, s: (s, 0))

    def heads_out(w, dt=BF16):
        return (jax.ShapeDtypeStruct((B, HEADS, S, w), dt),
                pl.BlockSpec((1, HEADS, tm, w), lambda b, s: (b, 0, s, 0)))

    def flat_out(w, dt):
        return (jax.ShapeDtypeStruct((B, S, w), dt), pl.BlockSpec((1, tm, w), lambda b, s: (b, s, 0)))

    heads_t = (jax.ShapeDtypeStruct((B, HEADS, HEAD_DIM, S), BF16),
               pl.BlockSpec((1, HEADS, HEAD_DIM, tm), lambda b, s: (b, 0, 0, s)))
    flat_t = (jax.ShapeDtypeStruct((B, HEAD_DIM, S), BF16),
              pl.BlockSpec((1, HEAD_DIM, tm), lambda b, s: (b, 0, s)))
    qi_out = (jax.ShapeDtypeStruct((B, DSA_IDX_HEADS, S, 128), BF16),
              pl.BlockSpec((1, DSA_IDX_HEADS, tm, 128), lambda b, s: (b, 0, s, 0)))
    outs = [heads_out(128), heads_out(128), heads_t,
            heads_out(64), heads_out(64),
            heads_out(64), heads_out(64), heads_t,
            flat_out(64, F32), flat_out(64, BF16), flat_out(64, BF16), flat_out(64, BF16),
            flat_out(64, F32), flat_t, flat_t, flat_t,
            qi_out, flat_out(128, BF16), flat_out(128, F32)]
    ins = [x, wm, wr, wt, c64, s64, c32, s32, cd, sd, cq, sq, qg, kvg, wuq, wuqr, wuk, wuvt, *place]
    in_specs = [pl.BlockSpec((1, tm, D), lambda b, s: (b, s, 0)), const(wm), const(wr), const(wt),
                tab(c64), tab(s64), tab(c32), tab(s32), tab(cd), tab(sd), tab(cq), tab(sq),
                const(qg), const(kvg), const(wuq), const(wuqr), const(wuk), const(wuvt),
                *[const(p) for p in place]]
    return pl.pallas_call(
        _inproj_kernel,
        out_shape=[o[0] for o in outs],
        grid=(B, S // tm),
        in_specs=in_specs,
        out_specs=[o[1] for o in outs],
        compiler_params=_params("parallel", "arbitrary"),
        name="in_proj",
    )(*ins)


V_ROWS = HEAD_DIM


def _acc_init(n_heads):
    return jnp.zeros((2 * V_ROWS, n_heads * QB), F32)


def _m_init():
    return jnp.full((1, QB), NEG_INF, F32)


def _with_ones(vt):
    return jnp.concatenate([vt, jnp.ones((V_ROWS, vt.shape[1]), vt.dtype)], axis=0)


def _softmax_step_t(ms, s_all, mask):
    new_ms, alphas, es = [], [], []
    for h, m in enumerate(ms):
        s = jnp.where(mask, s_all[:, h * QB:(h + 1) * QB], NEG_INF)
        m_new = jnp.maximum(m, jnp.max(s, axis=0, keepdims=True))
        alphas.append(jnp.exp(m - m_new))
        es.append(jnp.where(mask, jnp.exp(s - m_new), 0.0).astype(BF16))
        new_ms.append(m_new)
    return new_ms, jnp.concatenate(alphas, axis=1), jnp.concatenate(es, axis=1)


def _finish_t(acc, h):
    blk = acc[:, h * QB:(h + 1) * QB]
    return blk[:V_ROWS] / jnp.maximum(blk[V_ROWS:V_ROWS + 1], 1e-30)


def _store_heads_t(o_ref, outs_t):
    for p in range(HEADS // 2):
        pair = jnp.concatenate([outs_t[2 * p], outs_t[2 * p + 1]], axis=0)
        o_ref[0, :, p * 128:(p + 1) * 128] = pair.T


def _n_chunks(q0):
    return (q0 + QB + CK - 1) // CK


def _key_pos(ks, n):
    return ks + lax.broadcasted_iota(jnp.int32, (n, QB), 0)


def _query_pos(q0, n):
    return q0 + lax.broadcasted_iota(jnp.int32, (n, QB), 1)


def _mla_kernel(q_ref, k_ref, vt_ref, o_ref):
    q0 = pl.program_id(1) * QB
    tpos = _query_pos(q0, CK)

    def body(c, carry):
        ms, accs = carry
        ks = pl.multiple_of(c * CK, CK)
        mask = _key_pos(ks, CK) <= tpos
        new_ms, new_accs = [], []
        for h in range(HEADS):
            s = _dot_nt(k_ref[0, h, pl.ds(ks, CK), :], q_ref[0, h])
            m_new, alpha, p = _softmax_step_t([ms[h]], s, mask)
            new_ms.append(m_new[0])
            new_accs.append(alpha * accs[h] + _dot(_with_ones(vt_ref[0, h, :, pl.ds(ks, CK)]), p))
        return tuple(new_ms), tuple(new_accs)

    init = (tuple(_m_init() for _ in range(HEADS)), tuple(_acc_init(1) for _ in range(HEADS)))
    _, accs = lax.fori_loop(0, _n_chunks(q0), body, init)
    _store_heads_t(o_ref, [_finish_t(accs[h], 0) for h in range(HEADS)])


def _mla(q, k, vt):
    B, H, S, _ = q.shape
    return pl.pallas_call(
        _mla_kernel,
        out_shape=jax.ShapeDtypeStruct((B, S, D_GROUP), F32),
        grid=(B, S // QB),
        in_specs=[pl.BlockSpec((1, H, QB, 128), lambda b, i: (b, 0, i, 0)),
                  pl.BlockSpec((1, H, S, 128), lambda b, i: (b, 0, 0, 0)),
                  pl.BlockSpec((1, H, HEAD_DIM, S), lambda b, i: (b, 0, 0, 0))],
        out_specs=pl.BlockSpec((1, QB, D_GROUP), lambda b, i: (b, i, 0)),
        compiler_params=_params("parallel", "arbitrary"),
        name="mla_attn",
    )(q, k, vt)


SB_TOT_ROWS = 16


def _sb_kernel(q_ref, k_ref, vt_ref, tri_ref, o_ref):
    i = pl.program_id(1)
    q0 = i * QB
    tpos = _query_pos(q0, QB)
    tri = tri_ref[...]

    def body(r, carry):
        runs, accs = carry
        ks = pl.multiple_of((i - r) * QB, QB)
        mask = _key_pos(ks, QB) < tpos
        zs, sps, his, los = [], [], [], []
        for h in range(HEADS):
            z = _dot_nt(k_ref[0, h, pl.ds(ks, QB), :], q_ref[0, h])
            sp = jnp.maximum(z, 0.0) + jnp.log(1.0 + jnp.exp(-jnp.abs(z)))
            hi, lo = _split_bf16(jnp.where(mask, -sp, 0.0))
            zs.append(z), sps.append(sp), his.append(hi), los.append(lo)
        stacked = jnp.concatenate([jnp.concatenate(his, axis=1), jnp.concatenate(los, axis=1)], axis=0)
        res = _dot(tri, stacked)
        new_runs, new_accs = [], []
        for h in range(HEADS):
            after = res[:QB, h * QB:(h + 1) * QB] + runs[h]
            a = jnp.where(mask, jnp.exp(zs[h] - sps[h] + after), 0.0)
            new_accs.append(accs[h] + _dot(vt_ref[0, h, :, pl.ds(ks, QB)], a.astype(BF16)))
            new_runs.append(runs[h] + res[QB:QB + 1, h * QB:(h + 1) * QB])
        return tuple(new_runs), tuple(new_accs)

    init = (tuple(jnp.zeros((1, QB), F32) for _ in range(HEADS)),
            tuple(jnp.zeros((HEAD_DIM, QB), F32) for _ in range(HEADS)))
    _, accs = lax.fori_loop(0, i + 1, body, init)
    _store_heads_t(o_ref, accs)


def _sb(q, k, vt):
    B, H, S, _ = q.shape
    u = np.triu(np.ones((QB, QB), np.float32), 1)
    u = np.concatenate([u, np.ones((SB_TOT_ROWS, QB), np.float32)], 0)
    tri = jnp.asarray(np.concatenate([u, u], 1), BF16)
    return pl.pallas_call(
        _sb_kernel,
        out_shape=jax.ShapeDtypeStruct((B, S, D_GROUP), F32),
        grid=(B, S // QB),
        in_specs=[pl.BlockSpec((1, H, QB, 64), lambda b, i: (b, 0, i, 0)),
                  pl.BlockSpec((1, H, S, 64), lambda b, i: (b, 0, 0, 0)),
                  pl.BlockSpec((1, H, HEAD_DIM, S), lambda b, i: (b, 0, 0, 0)),
                  pl.BlockSpec(tri.shape, lambda b, i: (0, 0))],
        out_specs=pl.BlockSpec((1, QB, D_GROUP), lambda b, i: (b, i, 0)),
        compiler_params=_params("parallel", "arbitrary"),
        name="sb_attn",
    )(q, k, vt, tri)


def _ordered_to_f32(c):
    bits = c ^ ((c >> 31) & jnp.int32(0x7FFFFFFF))
    return pltpu.bitcast(bits, F32)


def _dsa_kernel(q_ref, k_ref, vt_ref, qi_ref, ki_ref, misc_ref, o_ref, sc_ref, m_ref, *, seq, n_top):
    q0 = pl.program_id(1) * QB
    nck = _n_chunks(q0)
    tpos = _query_pos(q0, CK)
    w_t = misc_ref[0].T[MISC_IDXW:MISC_IDXW + DSA_IDX_HEADS, :] * (DSA_IDX_HEADS * DSA_IDX_DIM) ** -0.5
    qi_all = qi_ref[0].reshape(DSA_IDX_HEADS * QB, 128)
    q_all = q_ref[0].reshape(HEADS * QB, HEAD_DIM)

    def score_body(c, _):
        ks = pl.multiple_of(c * CK, CK)
        r = _dot_nt(ki_ref[0, pl.ds(ks, CK), :], qi_all)
        sc = jnp.zeros((CK, QB), F32)
        for h in range(DSA_IDX_HEADS):
            sc = sc + w_t[h:h + 1, :] * jnp.maximum(r[:, h * QB:(h + 1) * QB], 0.0)
        sc = jnp.where(sc == 0.0, 0.0, sc)
        sc_ref[pl.ds(ks, CK), :] = jnp.where(_key_pos(ks, CK) <= tpos, sc, NEG_INF)
        return 0

    lax.fori_loop(0, nck, score_body, 0)

    n_beyond = (seq - nck * CK).astype(F32)

    def count(pred_fn):
        def cb(c, acc):
            ks = pl.multiple_of(c * CK, CK)
            hit = jnp.where(pred_fn(sc_ref[pl.ds(ks, CK), :], ks), 1.0, 0.0)
            return acc + jnp.sum(hit.reshape(CK // 8, 8, QB), axis=0)
        acc = lax.fori_loop(0, nck, cb, jnp.zeros((8, QB), F32))
        return jnp.sum(acc, axis=0, keepdims=True)

    def count_ge(thr):
        return count(lambda s, ks: s >= thr) + jnp.where(thr <= NEG_INF, n_beyond, 0.0)

    def thr_body(it, carry):
        t_int, cnt_t = carry
        cand = t_int ^ lax.shift_left(jnp.int32(1), 31 - it)
        cnt = count_ge(_ordered_to_f32(cand))
        ok = cnt >= n_top
        return jnp.where(ok, cand, t_int), jnp.where(ok, cnt, cnt_t)

    t_int, cnt_t = lax.fori_loop(0, 32, thr_body, (jnp.full((1, QB), -2 ** 31, jnp.int32),
                                                   jnp.full((1, QB), float(seq), F32)))
    thr = _ordered_to_f32(t_int)
    thr_up = _ordered_to_f32(t_int + 1)
    m_ref[...] = jnp.full(m_ref.shape, seq, jnp.int32)

    @pl.when(jnp.max(cnt_t) > n_top)
    def _():
        need = n_top - count_ge(thr_up)
        tie_beyond = thr <= NEG_INF

        def idx_body(it, m):
            cand = m + lax.shift_left(jnp.int32(1), (seq.bit_length() - 2) - it)
            below = count(lambda s, ks: (s >= thr) & (s < thr_up) & (_key_pos(ks, CK) < cand))
            below = below + jnp.where(tie_beyond, jnp.maximum(cand - nck * CK, 0).astype(F32), 0.0)
            return jnp.where(below < need, cand, m)

        m_idx = lax.fori_loop(0, seq.bit_length() - 1, idx_body, jnp.zeros((1, QB), jnp.int32))
        m_ref[...] = jnp.broadcast_to(m_idx, m_ref.shape)

    m_idx = m_ref[0:1, :]

    def attn_body(c, carry):
        ms, acc = carry
        ks = pl.multiple_of(c * CK, CK)
        kp = _key_pos(ks, CK)
        s_idx = sc_ref[pl.ds(ks, CK), :]
        sel = (s_idx >= thr_up) | ((s_idx >= thr) & (kp <= m_idx))
        mask = sel & (kp <= tpos)
        s_all = _dot_nt(k_ref[0, pl.ds(ks, CK), :], q_all)
        ms, alpha, p = _softmax_step_t(ms, s_all, mask)
        return tuple(ms), alpha * acc + _dot(_with_ones(vt_ref[0, :, pl.ds(ks, CK)]), p)

    _, acc = lax.fori_loop(0, nck, attn_body, (tuple(_m_init() for _ in range(HEADS)), _acc_init(HEADS)))
    _store_heads_t(o_ref, [_finish_t(acc, h) for h in range(HEADS)])


def _dsa(q, k, vt, qi, ki, misc):
    B, H, S, _ = q.shape
    n_top = min(DSA_TOPK, S // 4)
    return pl.pallas_call(
        functools.partial(_dsa_kernel, seq=S, n_top=n_top),
        out_shape=jax.ShapeDtypeStruct((B, S, D_GROUP), F32),
        grid=(B, S // QB),
        in_specs=[pl.BlockSpec((1, H, QB, 64), lambda b, i: (b, 0, i, 0)),
                  pl.BlockSpec((1, S, 64), lambda b, i: (b, 0, 0)),
                  pl.BlockSpec((1, HEAD_DIM, S), lambda b, i: (b, 0, 0)),
                  pl.BlockSpec((1, DSA_IDX_HEADS, QB, 128), lambda b, i: (b, 0, i, 0)),
                  pl.BlockSpec((1, S, 128), lambda b, i: (b, 0, 0)),
                  pl.BlockSpec((1, QB, 128), lambda b, i: (b, i, 0))],
        out_specs=pl.BlockSpec((1, QB, D_GROUP), lambda b, i: (b, i, 0)),
        scratch_shapes=[pltpu.VMEM((S, QB), F32), pltpu.VMEM((8, QB), jnp.int32)],
        compiler_params=_params("parallel", "arbitrary"),
        name="dsa_attn",
    )(q, k, vt, qi, ki, misc)


def _gelu_tanh(x):
    return 0.5 * x * (1.0 + jnp.tanh(np.sqrt(2.0 / np.pi) * (x + 0.044715 * (x * x * x))))


def _nsa_cmp_kernel(kg_ref, vg_ref, pos_ref, wk1_ref, wk2_ref, wv1_ref, wv2_ref, kc_ref, vct_ref):
    def compress(g_ref, w1_ref, w2_ref):
        top = _dot((g_ref[0] + pos_ref[0:1, :]).astype(BF16), w1_ref[0])
        bot = _dot((g_ref[0] + pos_ref[1:2, :]).astype(BF16), w1_ref[1])
        pre = top + pltpu.roll(bot, bot.shape[0] - 1, 0)
        return _dot(_gelu_tanh(pre).astype(BF16), w2_ref[...])

    kc_ref[0] = compress(kg_ref, wk1_ref, wk2_ref).astype(BF16)
    vct_ref[0] = compress(vg_ref, wv1_ref, wv2_ref).T[:HEAD_DIM].astype(BF16)


def _nsa_compress(kcmp, vcmp, pos, wk1, wk2, wv1, wv2):
    B, S, _ = kcmp.shape
    ng = S // NSA_CMP_STRIDE
    gw = NSA_CMP_STRIDE * HEAD_DIM
    kg = kcmp.reshape(B, ng, gw)
    vg = vcmp.reshape(B, ng, gw)

    def const(a):
        return pl.BlockSpec(a.shape, lambda b: (0,) * a.ndim)

    wv2p = jnp.concatenate([wv2, jnp.zeros((HEAD_DIM, LANES - HEAD_DIM), wv2.dtype)], axis=1)
    ins = [kg, vg, pos.reshape(2, gw), wk1.reshape(2, gw, HEAD_DIM).astype(BF16), wk2.astype(BF16),
           wv1.reshape(2, gw, HEAD_DIM).astype(BF16), wv2p.astype(BF16)]
    blk = pl.BlockSpec((1, ng, gw), lambda b: (b, 0, 0))
    return pl.pallas_call(
        _nsa_cmp_kernel,
        out_shape=[jax.ShapeDtypeStruct((B, ng, HEAD_DIM), BF16), jax.ShapeDtypeStruct((B, HEAD_DIM, ng), BF16)],
        grid=(B,),
        in_specs=[blk, blk] + [const(a) for a in ins[2:]],
        out_specs=[pl.BlockSpec((1, ng, HEAD_DIM), lambda b: (b, 0, 0)),
                   pl.BlockSpec((1, HEAD_DIM, ng), lambda b: (b, 0, 0))],
        compiler_params=_params("parallel"),
        name="nsa_compress",
    )(*ins)


def _nsa_kernel(q_ref, kc_ref, vct_ref, ks_ref, vst_ref, kw_ref, vwt_ref, misc_ref, ovl_ref, o_ref,
                *, seq):
    i = pl.program_id(1)
    q0 = i * QB
    n_slc = seq // NSA_SEL_LEN
    n_cmp = kc_ref.shape[1]
    q_all = q_ref[0].reshape(HEADS * QB, HEAD_DIM)

    last_tok = lax.broadcasted_iota(jnp.int32, (n_cmp, QB), 0) * NSA_CMP_STRIDE + (NSA_CMP_LEN - 1)
    cmask = last_tok <= _query_pos(q0, n_cmp)
    s_cmp = _dot_nt(kc_ref[0], q_all)
    ps = []
    p_sum = jnp.zeros((n_cmp, QB), F32)
    for h in range(HEADS):
        s = jnp.where(cmask, s_cmp[:, h * QB:(h + 1) * QB], NEG_INF)
        e = jnp.where(cmask, jnp.exp(s - jnp.max(s, axis=0, keepdims=True)), 0.0)
        p = e / jnp.maximum(jnp.sum(e, axis=0, keepdims=True), 1e-30)
        p_sum = p_sum + p
        ps.append(p.astype(BF16))
    o_cmp = _dot(vct_ref[0], jnp.concatenate(ps, axis=1))

    hi, lo = _split_bf16(p_sum)
    imp = _dot(ovl_ref[...], jnp.concatenate([hi, lo], axis=0))
    jb = lax.broadcasted_iota(jnp.int32, (n_slc, 1), 0)
    cur = (q0 + lax.broadcasted_iota(jnp.int32, (1, QB), 1)) >> SEL_SHIFT
    forced = (jb < NSA_N_INIT) | ((jb <= cur) & (jb > cur - NSA_N_LOCAL))
    imp = jnp.where(forced, FORCE_SCORE, jnp.where(jb <= cur, imp, NEG_INF))
    rank = jnp.zeros((n_slc, QB), F32)
    for r in range(n_slc):
        row = imp[r:r + 1, :]
        beats = (row > imp) | ((row == imp) & (r < jb))
        rank = rank + jnp.where(beats, 1.0, 0.0)
    sel_t = jnp.where(rank < min(NSA_N_SEL, n_slc), 1.0, 0.0)
    sel_t = jnp.concatenate([sel_t, jnp.zeros((LANES - n_slc, QB), F32)], axis=0).astype(BF16)

    tpos = _query_pos(q0, CK)

    def slc_body(c, carry):
        ms, acc = carry
        ks = pl.multiple_of(c * CK, CK)
        kp = _key_pos(ks, CK)
        blk = (ks + lax.broadcasted_iota(jnp.int32, (CK, LANES), 0)) >> SEL_SHIFT
        expand = jnp.where(blk == lax.broadcasted_iota(jnp.int32, (CK, LANES), 1), 1.0, 0.0)
        chosen = _dot(expand.astype(BF16), sel_t) > 0.5
        mask = chosen & (kp <= tpos)
        s_all = _dot_nt(ks_ref[0, pl.ds(ks, CK), :], q_all)
        ms, alpha, p = _softmax_step_t(ms, s_all, mask)
        return tuple(ms), alpha * acc + _dot(_with_ones(vst_ref[0, :, pl.ds(ks, CK)]), p)

    _, acc_slc = lax.fori_loop(0, _n_chunks(q0), slc_body,
                               (tuple(_m_init() for _ in range(HEADS)), _acc_init(HEADS)))

    n_win = NSA_WINDOW + QB
    ws = pl.multiple_of(jnp.maximum(q0 - NSA_WINDOW, 0), QB)
    dist = _query_pos(q0, n_win) - _key_pos(ws, n_win)
    wmask = (dist >= 0) & (dist < NSA_WINDOW)
    s_win = _dot_nt(kw_ref[0, pl.ds(ws, n_win), :], q_all)
    _, _, p_win = _softmax_step_t([_m_init() for _ in range(HEADS)], s_win, wmask)
    acc_win = _dot(_with_ones(vwt_ref[0, :, pl.ds(ws, n_win)]), p_win)

    gate_rows = 16
    gate = jax.nn.sigmoid(misc_ref[0].T[MISC_GATE:MISC_GATE + gate_rows, :])
    outs = []
    for h in range(HEADS):
        outs.append(gate[h:h + 1] * o_cmp[:, h * QB:(h + 1) * QB]
                    + gate[HEADS + h:HEADS + h + 1] * _finish_t(acc_slc, h)
                    + gate[2 * HEADS + h:2 * HEADS + h + 1] * _finish_t(acc_win, h))
    _store_heads_t(o_ref, outs)


def _nsa_overlap(seq):
    ng = seq // NSA_CMP_STRIDE
    n_slc = seq // NSA_SEL_LEN
    n = np.arange(ng)
    first, last = n * NSA_CMP_STRIDE, n * NSA_CMP_STRIDE + NSA_CMP_LEN - 1
    start = np.arange(n_slc) * NSA_SEL_LEN
    ovl = ((first[None, :] <= start[:, None] + NSA_SEL_LEN - 1) & (last[None, :] >= start[:, None]))
    ovl = ovl & (last[None, :] < seq)
    ovl = ovl.astype(np.float32)
    return jnp.asarray(np.concatenate([ovl, ovl], 1), BF16)


def _nsa(q, kc, vct, k_slc, vt_slc, k_win, vt_win, misc):
    B, H, S, _ = q.shape
    ng = kc.shape[1]
    ovl = _nsa_overlap(S)
    keys = pl.BlockSpec((1, S, HEAD_DIM), lambda b, i: (b, 0, 0))
    vals_t = pl.BlockSpec((1, HEAD_DIM, S), lambda b, i: (b, 0, 0))
    return pl.pallas_call(
        functools.partial(_nsa_kernel, seq=S),
        out_shape=jax.ShapeDtypeStruct((B, S, D_GROUP), F32),
        grid=(B, S // QB),
        in_specs=[pl.BlockSpec((1, H, QB, 64), lambda b, i: (b, 0, i, 0)),
                  pl.BlockSpec((1, ng, HEAD_DIM), lambda b, i: (b, 0, 0)),
                  pl.BlockSpec((1, HEAD_DIM, ng), lambda b, i: (b, 0, 0)),
                  keys, vals_t, keys, vals_t,
                  pl.BlockSpec((1, QB, 128), lambda b, i: (b, i, 0)),
                  pl.BlockSpec(ovl.shape, lambda b, i: (0, 0))],
        out_specs=pl.BlockSpec((1, QB, D_GROUP), lambda b, i: (b, i, 0)),
        compiler_params=_params("parallel", "arbitrary"),
        name="nsa_attn",
    )(q, kc, vct, k_slc, vt_slc, k_win, vt_win, misc, ovl)


def _outproj_kernel(x_ref, ya_ref, yb_ref, yc_ref, yd_ref, gg_ref, wo_ref, g_ref, b_ref, o_ref):
    acc = None
    for gi, y_ref in enumerate((ya_ref, yb_ref, yc_ref, yd_ref)):
        lo, hi = gi * D_GROUP, (gi + 1) * D_GROUP
        yn = _rms_norm(y_ref[...], gg_ref[:, lo:hi]).astype(BF16)
        part = _dot(yn, wo_ref[lo:hi, :])
        acc = part if acc is None else acc + part
    o_ref[...] = _layer_norm(DN_ALPHA * x_ref[...] + acc, g_ref[...], b_ref[...])


def _outproj(x2, ys, gg, wo, g, b, *, tm):
    n, d = x2.shape
    row = lambda w: pl.BlockSpec((tm, w), lambda i: (i, 0))
    const = lambda a: pl.BlockSpec(a.shape, lambda i: (0, 0))
    return pl.pallas_call(
        _outproj_kernel,
        out_shape=jax.ShapeDtypeStruct((n, d), F32),
        grid=(n // tm,),
        in_specs=[row(d)] + [row(D_GROUP)] * 4 + [const(gg), const(wo), const(g), const(b)],
        out_specs=row(d),
        compiler_params=_params("parallel"),
        name="out_proj",
    )(x2, *ys, gg, wo, g, b)


def _mixer(x, w_in, q_norm, w_uq, kv_norm, w_ukv, cmp_pos, wk1, wk2, wv1, wv2, tabs, place, maps):
    B, S, D = x.shape
    main_cols, rot_cols = maps
    wm = _gather_cols(w_in, main_cols).astype(BF16)
    wr = _gather_cols(w_in, rot_cols).astype(BF16)
    wt = jnp.concatenate([wm[:, slice(*_MAIN_OFF['slab_v'])], wm[:, slice(*_MAIN_OFF['sb_v'])]], axis=1).T
    (mq, mk, mvt, nq, dq, sbq, sbk, sbvt, kcmp, kslc, kwin, dk, vcmp, vslct, vwint, dvt,
     qi, ki, misc) = _inproj(x, wm, wr, wt, tabs, q_norm[None, :], kv_norm[None, :],
                             _mla_weights(w_uq, w_ukv), place, tm=min(512, S))
    y_a = _mla(mq, mk, mvt)
    kc, vct = _nsa_compress(kcmp, vcmp, cmp_pos, wk1, wk2, wv1, wv2)
    y_b = _nsa(nq, kc, vct, kslc, vslct, kwin, vwint, misc)
    y_c = _dsa(dq, dk, dvt, qi, ki, misc)
    y_d = _sb(sbq, sbk, sbvt)
    return y_a, y_b, y_c, y_d


def kernel(x, ln1_g, ln1_b, ffn1_w1, ffn1_w3, ffn1_w2, w_in, mla_q_norm, mla_w_uq, mla_kv_norm,
           mla_w_ukv, nsa_cmp_pos, nsa_cmp_wk1, nsa_cmp_wk2, nsa_cmp_wv1, nsa_cmp_wv2, group_norm_g,
           w_out, ln2_g, ln2_b, ffn2_w1, ffn2_w3, ffn2_w2, ln3_g, ln3_b):
    B, S, D = x.shape
    n = B * S
    tabs = _rope_tables(S)
    place = _placement_constants()
    maps = _column_maps()
    tm_ffn = min(1024, n)
    x2 = x.reshape(n, D)
    for l in range(DEPTH):
        x2 = _ffn_ln(x2, ffn1_w1[l].astype(BF16), ffn1_w3[l].astype(BF16), ffn1_w2[l].astype(BF16),
                     ln1_g[l][None, :], ln1_b[l][None, :], tm=tm_ffn, tf=256)
        ys = _mixer(x2.reshape(B, S, D), w_in[l], mla_q_norm[l], mla_w_uq[l], mla_kv_norm[l],
                    mla_w_ukv[l], nsa_cmp_pos[l], nsa_cmp_wk1[l], nsa_cmp_wk2[l], nsa_cmp_wv1[l],
                    nsa_cmp_wv2[l], tabs, place, maps)
        x2 = _outproj(x2, [y.reshape(n, D_GROUP) for y in ys], group_norm_g[l][None, :],
                      w_out[l].astype(BF16), ln2_g[l][None, :], ln2_b[l][None, :], tm=min(512, n))
        x2 = _ffn_ln(x2, ffn2_w1[l].astype(BF16), ffn2_w3[l].astype(BF16), ffn2_w2[l].astype(BF16),
                     ln3_g[l][None, :], ln3_b[l][None, :], tm=tm_ffn, tf=256)
    return x2.reshape(B, S, D)
```

```python
import functools

import numpy as np
import jax
import jax.numpy as jnp
from jax import lax
from jax.experimental import pallas as pl
from jax.experimental.pallas import tpu as pltpu

F32 = jnp.float32
BF16 = jnp.bfloat16

D_MODEL = 1024
DEPTH = 2
HEADS = 4
HEAD_DIM = 64
D_GROUP = HEADS * HEAD_DIM
N_GROUPS = 4
D_FF = 2816
ROPE_THETA = 10000.0
MLA_Q_RANK = 256
MLA_KV_RANK = 128
MLA_D_NOPE = 64
MLA_D_ROPE = 32
MLA_D_V = 64
NSA_CMP_LEN = 32
NSA_CMP_STRIDE = 16
NSA_SEL_LEN = 64
NSA_N_SEL = 8
NSA_N_INIT = 1
NSA_N_LOCAL = 2
NSA_WINDOW = 512
SEL_SHIFT = NSA_SEL_LEN.bit_length() - 1
DSA_TOPK = 256
DSA_IDX_HEADS = 8
DSA_IDX_DIM = 32
DN_ALPHA = (2.0 * DEPTH) ** 0.25
LN_EPS = 1e-5
RMS_EPS = 1e-6
NEG_INF = -1e30
FORCE_SCORE = 1e4

LANES = 128
QB = 128
CK = 256
DSA_CK = 512
VMEM_LIMIT = 48 * 1024 * 1024

MISC_KROPE = 0
MISC_IDXK = 32
MISC_IDXW = 64
MISC_GATE = 72

NT_DIMS = (((1,), (1,)), ((), ()))


def _dot(a, b):
    return jnp.dot(a, b, preferred_element_type=F32)


def _dot_nt(a, b):
    return lax.dot_general(a, b, NT_DIMS, preferred_element_type=F32)


def _split_bf16(x):
    hi = x.astype(BF16)
    lo = (x - hi.astype(F32)).astype(BF16)
    return hi, lo


def _layer_norm(y, g, b):
    mu = jnp.mean(y, -1, keepdims=True)
    d = y - mu
    var = jnp.mean(d * d, -1, keepdims=True)
    return d * lax.rsqrt(var + LN_EPS) * g + b


def _rms_norm(y, g):
    return y * lax.rsqrt(jnp.mean(y * y, -1, keepdims=True) + RMS_EPS) * g


def _params(*sem):
    return pltpu.CompilerParams(dimension_semantics=sem, vmem_limit_bytes=VMEM_LIMIT)


def _ffn_kernel(x_ref, w1_ref, w3_ref, w2_ref, g_ref, b_ref, o_ref, a_ref, *, tf):
    xb = x_ref[...].astype(BF16)
    n_slabs = w1_ref.shape[1] // tf

    def up(j):
        return _dot(xb, w1_ref[:, j * tf:(j + 1) * tf]), _dot(xb, w3_ref[:, j * tf:(j + 1) * tf])

    hu = up(0)
    for j in range(n_slabs):
        nxt = up(j + 1) if j + 1 < n_slabs else None
        h, u = hu
        a_ref[:, j * tf:(j + 1) * tf] = (h * jax.nn.sigmoid(h) * u).astype(BF16)
        hu = nxt
    y = DN_ALPHA * x_ref[...] + 0.5 * _dot(a_ref[...], w2_ref[...])
    o_ref[...] = _layer_norm(y, g_ref[...], b_ref[...])


def _ffn_ln(x2, w1, w3, w2, g, b, *, tm, tf):
    n, d = x2.shape
    dff = w1.shape[1]
    const = lambda a: pl.BlockSpec(a.shape, lambda i: (0, 0))
    return pl.pallas_call(
        functools.partial(_ffn_kernel, tf=tf),
        out_shape=jax.ShapeDtypeStruct((n, d), F32),
        grid=(n // tm,),
        in_specs=[pl.BlockSpec((tm, d), lambda i: (i, 0)), const(w1), const(w3), const(w2), const(g), const(b)],
        out_specs=pl.BlockSpec((tm, d), lambda i: (i, 0)),
        scratch_shapes=[pltpu.VMEM((tm, dff), BF16)],
        compiler_params=_params("parallel"),
        name="ffn_ln",
    )(x2, w1, w3, w2, g, b)


_MAIN_ORDER = (
    ('mla_cq', 256), ('nsa_q', 256), ('dsa_q', 256), ('idx_q', 256),
    ('sb_q', 256), ('sb_k', 256), ('sb_v', 256),
    ('slab_k', 256),
    ('slab_v', 256),
    ('mla_ckv', 128),
    ('misc', 128),
)
_MAIN_OFF = {}
_o = 0
for _n, _w in _MAIN_ORDER:
    _MAIN_OFF[_n] = (_o, _o + _w)
    _o += _w
N_MAIN = _o
_ROT_ORDER = (('nsa_q', 256), ('dsa_q', 256), ('idx_q', 256), ('slab_k', 256), ('misc', 128))
_ROT_OFF = {}
_o = 0
for _n, _w in _ROT_ORDER:
    _ROT_OFF[_n] = (_o, _o + _w)
    _o += _w
N_ROT = _o

_IN_SPLITS = (
    ('mla_cq', MLA_Q_RANK), ('mla_ckv', MLA_KV_RANK), ('mla_krope', MLA_D_ROPE),
    ('nsa_q', D_GROUP), ('nsa_k_cmp', HEAD_DIM), ('nsa_v_cmp', HEAD_DIM),
    ('nsa_k_slc', HEAD_DIM), ('nsa_v_slc', HEAD_DIM), ('nsa_k_win', HEAD_DIM),
    ('nsa_v_win', HEAD_DIM), ('nsa_gate', 3 * HEADS),
    ('dsa_q', D_GROUP), ('dsa_k', HEAD_DIM), ('dsa_v', HEAD_DIM),
    ('idx_q', DSA_IDX_HEADS * DSA_IDX_DIM), ('idx_k', DSA_IDX_DIM), ('idx_w', DSA_IDX_HEADS),
    ('sb_q', D_GROUP), ('sb_k', D_GROUP), ('sb_v', D_GROUP),
)
_SRC = {}
_o = 0
for _n, _w in _IN_SPLITS:
    _SRC[_n] = np.arange(_o, _o + _w)
    _o += _w
D_IN = _o


def _swap_halves(width, dim):
    idx = np.arange(width)
    return (idx // dim) * dim + (idx % dim + dim // 2) % dim


def _column_maps():
    pieces = {
        'mla_cq': [_SRC['mla_cq']], 'nsa_q': [_SRC['nsa_q']], 'dsa_q': [_SRC['dsa_q']],
        'idx_q': [_SRC['idx_q']], 'sb_q': [_SRC['sb_q']], 'sb_k': [_SRC['sb_k']], 'sb_v': [_SRC['sb_v']],
        'slab_k': [_SRC['nsa_k_cmp'], _SRC['nsa_k_slc'], _SRC['nsa_k_win'], _SRC['dsa_k']],
        'slab_v': [_SRC['nsa_v_cmp'], _SRC['nsa_v_slc'], _SRC['nsa_v_win'], _SRC['dsa_v']],
        'mla_ckv': [_SRC['mla_ckv']],
        'misc': [_SRC['mla_krope'], _SRC['idx_k'], _SRC['idx_w'], _SRC['nsa_gate'],
                 -np.ones(LANES - MISC_GATE - 3 * HEADS, np.int64)],
    }
    main = np.concatenate([np.concatenate(pieces[n]) for n, _ in _MAIN_ORDER])
    rot_dim = {'nsa_q': HEAD_DIM, 'dsa_q': HEAD_DIM, 'idx_q': DSA_IDX_DIM, 'slab_k': HEAD_DIM}
    rot = []
    for n, w in _ROT_ORDER:
        cols = np.concatenate(pieces[n])
        if n == 'misc':
            r = -np.ones(w, np.int64)
            r[:64] = cols[:64][_swap_halves(64, 32)]
        else:
            r = cols[_swap_halves(w, rot_dim[n])]
        rot.append(r)
    return main, np.concatenate(rot)


def _gather_cols(w, cols):
    out = jnp.take(w, np.maximum(cols, 0), axis=1)
    return jnp.where(jnp.asarray(cols >= 0)[None, :], out, 0.0)


def _rope_tables(seq):
    def base(dim):
        inv = ROPE_THETA ** (-jnp.arange(0, dim, 2, dtype=F32) / dim)
        ang = jnp.arange(seq, dtype=F32)[:, None] * inv[None, :]
        c, s = jnp.cos(ang), jnp.sin(ang)
        return jnp.concatenate([c, c], -1), jnp.concatenate([-s, s], -1)

    c64, s64 = base(HEAD_DIM)
    c32, s32 = base(DSA_IDX_DIM)
    t64 = (jnp.tile(c64, (1, 4)), jnp.tile(s64, (1, 4)))
    t32 = (jnp.tile(c32, (1, 8)), jnp.tile(s32, (1, 8)))
    ones = jnp.ones((seq, LANES - 64), F32)
    td = (jnp.concatenate([c32, c32, ones], -1), jnp.concatenate([s32, s32, 0.0 * ones], -1))
    scale = (MLA_D_NOPE + MLA_D_ROPE) ** -0.5
    cq = jnp.concatenate([jnp.ones((seq, 64), F32), c32, jnp.zeros((seq, 32), F32)], -1) * scale
    sq = jnp.concatenate([jnp.zeros((seq, 64), F32), s32, jnp.zeros((seq, 32), F32)], -1) * scale
    tq = (jnp.tile(cq, (1, 4)), jnp.tile(sq, (1, 4)))
    return t64, t32, td, tq


def _placement_constants():
    p_hi = np.zeros((256, 1024), np.float32)
    p_lo = np.zeros((256, 1024), np.float32)
    for h in range(DSA_IDX_HEADS):
        for d in range(DSA_IDX_DIM):
            p_hi[h * 32 + d, h * 128 + d] = 1
            p_hi[h * 32 + d, h * 128 + 64 + d] = 1
            p_lo[h * 32 + d, h * 128 + 32 + d] = 1
    pk_hi = np.zeros((128, 128), np.float32)
    pk_lo = np.zeros((128, 128), np.float32)
    for d in range(DSA_IDX_DIM):
        pk_hi[MISC_IDXK + d, d] = 1
        pk_hi[MISC_IDXK + d, 32 + d] = 1
        pk_lo[MISC_IDXK + d, 64 + d] = 1
    pk_pe = np.zeros((128, 512), np.float32)
    for h in range(HEADS):
        for d in range(MLA_D_ROPE):
            pk_pe[MISC_KROPE + d, h * 128 + 64 + d] = 1
    return tuple(jnp.asarray(a, BF16) for a in (p_hi, p_lo, pk_hi, pk_lo, pk_pe))


def _mla_weights(w_uq, w_ukv):
    dq = MLA_D_NOPE + MLA_D_ROPE
    cols_q = -np.ones(512, np.int64)
    cols_qr = -np.ones(512, np.int64)
    cols_k = -np.ones(512, np.int64)
    cols_v = np.zeros(256, np.int64)
    for h in range(HEADS):
        cols_q[h * 128:h * 128 + dq] = h * dq + np.arange(dq)
        pe = h * dq + MLA_D_NOPE + np.arange(MLA_D_ROPE)
        cols_qr[h * 128 + 64:h * 128 + 96] = pe[_swap_halves(32, 32)]
        cols_k[h * 128:h * 128 + 64] = h * 128 + np.arange(64)
        cols_v[h * 64:(h + 1) * 64] = h * 128 + 64 + np.arange(64)
    return (_gather_cols(w_uq, cols_q).astype(BF16), _gather_cols(w_uq, cols_qr).astype(BF16),
            _gather_cols(w_ukv, cols_k).astype(BF16), jnp.take(w_ukv, cols_v, axis=1).T.astype(BF16))


def _inproj_kernel(x_ref, wm_ref, wr_ref, wt_ref, c64_ref, s64_ref, c32_ref, s32_ref, cd_ref, sd_ref,
                   cq_ref, sq_ref, qg_ref, kvg_ref, wuq_ref, wuqr_ref, wuk_ref, wuvt_ref,
                   phi_ref, plo_ref, pkhi_ref, pklo_ref, pkpe_ref,
                   mq_ref, mk_ref, mvt_ref, nq_ref, dq_ref, sbq_ref, sbk_ref, sbvt_ref,
                   kcmp_ref, kslc_ref, kwin_ref, dk_ref, vcmp_ref, vslct_ref, vwint_ref, dvt_ref,
                   qi_ref, ki_ref, misc_ref):
    xb = x_ref[0].astype(BF16)

    def main(name):
        a, b = _MAIN_OFF[name]
        return _dot(xb, wm_ref[:, a:b])

    def roped(name, c_ref, s_ref):
        a, b = _ROT_OFF[name]
        return main(name) * c_ref[...] + _dot(xb, wr_ref[:, a:b]) * s_ref[...]

    def store_heads(ref, val, width):
        for h in range(HEADS):
            ref[0, h] = val[:, h * width:(h + 1) * width].astype(ref.dtype)

    cqn = _rms_norm(main('mla_cq'), qg_ref[...]).astype(BF16)
    q = _dot(cqn, wuq_ref[...]) * cq_ref[...] + _dot(cqn, wuqr_ref[...]) * sq_ref[...]
    store_heads(mq_ref, q, 128)
    misc = roped('misc', cd_ref, sd_ref)
    misc_ref[0] = misc
    m_hi, m_lo = _split_bf16(misc)
    ckvn = _rms_norm(main('mla_ckv'), kvg_ref[...]).astype(BF16)
    k = _dot(ckvn, wuk_ref[...]) + _dot(m_hi, pkpe_ref[...])
    store_heads(mk_ref, k, 128)
    mvt = _dot_nt(wuvt_ref[...], ckvn)
    for h in range(HEADS):
        mvt_ref[0, h] = mvt[h * 64:(h + 1) * 64, :].astype(BF16)
    qscale = HEAD_DIM ** -0.5
    store_heads(nq_ref, roped('nsa_q', c64_ref, s64_ref) * qscale, 64)
    store_heads(dq_ref, roped('dsa_q', c64_ref, s64_ref) * qscale, 64)
    store_heads(sbq_ref, main('sb_q') * qscale, 64)
    store_heads(sbk_ref, main('sb_k'), 64)
    sk = roped('slab_k', c64_ref, s64_ref)
    for j, ref in enumerate((kcmp_ref, kslc_ref, kwin_ref, dk_ref)):
        ref[0] = sk[:, j * 64:(j + 1) * 64].astype(ref.dtype)
    vcmp_ref[0] = main('slab_v')[:, 0:64]
    vt = _dot_nt(wt_ref[...], xb)
    for j, ref in ((1, vslct_ref), (2, vwint_ref), (3, dvt_ref)):
        ref[0] = vt[j * 64:(j + 1) * 64, :].astype(BF16)
    for h in range(HEADS):
        sbvt_ref[0, h] = vt[256 + h * 64:256 + (h + 1) * 64, :].astype(BF16)
    qi_hi, qi_lo = _split_bf16(roped('idx_q', c32_ref, s32_ref))
    qi = _dot(qi_hi, phi_ref[...]) + _dot(qi_lo, plo_ref[...])
    for h in range(DSA_IDX_HEADS):
        qi_ref[0, h] = qi[:, h * 128:(h + 1) * 128].astype(BF16)
    ki_ref[0] = (_dot(m_hi, pkhi_ref[...]) + _dot(m_lo, pklo_ref[...])).astype(BF16)


def _inproj(x, wm, wr, wt, tabs, qg, kvg, mla_w, place, *, tm):
    B, S, D = x.shape
    (c64, s64), (c32, s32), (cd, sd), (cq, sq) = tabs
    wuq, wuqr, wuk, wuvt = mla_w

    def const(a):
        return pl.BlockSpec(a.shape, lambda b, s: (0,) * a.ndim)

    def tab(a):
        return pl.BlockSpec((tm, a.shape[1]), lambda b, s: (s, 0))

    def heads_out(w, dt=BF16):
        return (jax.ShapeDtypeStruct((B, HEADS, S, w), dt),
                pl.BlockSpec((1, HEADS, tm, w), lambda b, s: (b, 0, s, 0)))

    def flat_out(w, dt):
        return (jax.ShapeDtypeStruct((B, S, w), dt), pl.BlockSpec((1, tm, w), lambda b, s: (b, s, 0)))

    heads_t = (jax.ShapeDtypeStruct((B, HEADS, HEAD_DIM, S), BF16),
               pl.BlockSpec((1, HEADS, HEAD_DIM, tm), lambda b, s: (b, 0, 0, s)))
    flat_t = (jax.ShapeDtypeStruct((B, HEAD_DIM, S), BF16),
              pl.BlockSpec((1, HEAD_DIM, tm), lambda b, s: (b, 0, s)))
    qi_out = (jax.ShapeDtypeStruct((B, DSA_IDX_HEADS, S, 128), BF16),
              pl.BlockSpec((1, DSA_IDX_HEADS, tm, 128), lambda b, s: (b, 0, s, 0)))
    outs = [heads_out(128), heads_out(128), heads_t,
            heads_out(64), heads_out(64),
            heads_out(64), heads_out(64), heads_t,
            flat_out(64, F32), flat_out(64, BF16), flat_out(64, BF16), flat_out(64, BF16),
            flat_out(64, F32), flat_t, flat_t, flat_t,
            qi_out, flat_out(128, BF16), flat_out(128, F32)]
    ins = [x, wm, wr, wt, c64, s64, c32, s32, cd, sd, cq, sq, qg, kvg, wuq, wuqr, wuk, wuvt, *place]
    in_specs = [pl.BlockSpec((1, tm, D), lambda b, s: (b, s, 0)), const(wm), const(wr), const(wt),
                tab(c64), tab(s64), tab(c32), tab(s32), tab(cd), tab(sd), tab(cq), tab(sq),
                const(qg), const(kvg), const(wuq), const(wuqr), const(wuk), const(wuvt),
                *[const(p) for p in place]]
    return pl.pallas_call(
        _inproj_kernel,
        out_shape=[o[0] for o in outs],
        grid=(B, S // tm),
        in_specs=in_specs,
        out_specs=[o[1] for o in outs],
        compiler_params=_params("parallel", "arbitrary"),
        name="in_proj",
    )(*ins)


V_ROWS = HEAD_DIM


def _acc_init(n_heads, qb=QB):
    return jnp.zeros((2 * V_ROWS, n_heads * qb), F32)


def _m_init(qb=QB):
    return jnp.full((1, qb), NEG_INF, F32)


def _with_ones(vt):
    return jnp.concatenate([vt, jnp.ones((V_ROWS, vt.shape[1]), vt.dtype)], axis=0)


def _softmax_step_t(ms, s_all, mask):
    new_ms, alphas, es = [], [], []
    qb = mask.shape[1]
    for h, m in enumerate(ms):
        s = jnp.where(mask, s_all[:, h * qb:(h + 1) * qb], NEG_INF)
        m_new = jnp.maximum(m, jnp.max(s, axis=0, keepdims=True))
        alphas.append(jnp.exp(m - m_new))
        es.append(jnp.where(mask, jnp.exp(s - m_new), 0.0).astype(BF16))
        new_ms.append(m_new)
    return new_ms, jnp.concatenate(alphas, axis=1), jnp.concatenate(es, axis=1)


def _finish_t(acc, h, qb=QB):
    blk = acc[:, h * qb:(h + 1) * qb]
    return blk[:V_ROWS] / jnp.maximum(blk[V_ROWS:V_ROWS + 1], 1e-30)


def _store_heads_t(o_ref, outs_t):
    for p in range(HEADS // 2):
        pair = jnp.concatenate([outs_t[2 * p], outs_t[2 * p + 1]], axis=0)
        o_ref[0, :, p * 128:(p + 1) * 128] = pair.T


def _n_chunks(q0, qb=QB, ck=CK):
    return (q0 + qb + ck - 1) // ck


def _key_pos(ks, n, qb=QB):
    return ks + lax.broadcasted_iota(jnp.int32, (n, qb), 0)


def _query_pos(q0, n, qb=QB):
    return q0 + lax.broadcasted_iota(jnp.int32, (n, qb), 1)


QB_WIDE = 256


def _mla_kernel(q_ref, k_ref, vt_ref, o_ref):
    qb = QB_WIDE
    q0 = pl.program_id(1) * qb
    tpos = _query_pos(q0, CK, qb)

    def body(c, carry):
        ms, accs = carry
        ks = pl.multiple_of(c * CK, CK)
        mask = _key_pos(ks, CK, qb) <= tpos
        ss = [_dot_nt(k_ref[0, h, pl.ds(ks, CK), :], q_ref[0, h]) for h in range(HEADS)]
        steps = [_softmax_step_t([ms[h]], ss[h], mask) for h in range(HEADS)]
        new_accs = [steps[h][1] * accs[h] + _dot(_with_ones(vt_ref[0, h, :, pl.ds(ks, CK)]), steps[h][2])
                    for h in range(HEADS)]
        return tuple(st[0][0] for st in steps), tuple(new_accs)

    init = (tuple(_m_init(qb) for _ in range(HEADS)), tuple(_acc_init(1, qb) for _ in range(HEADS)))
    _, accs = lax.fori_loop(0, _n_chunks(q0, qb), body, init)
    _store_heads_t(o_ref, [_finish_t(accs[h], 0, qb) for h in range(HEADS)])


def _mla(q, k, vt):
    B, H, S, _ = q.shape
    qb = QB_WIDE
    return pl.pallas_call(
        _mla_kernel,
        out_shape=jax.ShapeDtypeStruct((B, S, D_GROUP), F32),
        grid=(B, S // qb),
        in_specs=[pl.BlockSpec((1, H, qb, 128), lambda b, i: (b, 0, i, 0)),
                  pl.BlockSpec((1, H, S, 128), lambda b, i: (b, 0, 0, 0)),
                  pl.BlockSpec((1, H, HEAD_DIM, S), lambda b, i: (b, 0, 0, 0))],
        out_specs=pl.BlockSpec((1, qb, D_GROUP), lambda b, i: (b, i, 0)),
        compiler_params=_params("parallel", "arbitrary"),
        name="mla_attn",
    )(q, k, vt)


SB_TOT_ROWS = 16


SB_CK = 128


def _sb_kernel(q_ref, k_ref, vt_ref, tri_ref, o_ref):
    qb, kc = QB_WIDE, SB_CK
    n_sub = 2
    step = n_sub * kc
    q0 = pl.program_id(1) * qb
    tpos = _query_pos(q0, step, qb)
    tri = tri_ref[...]

    def body(r, carry):
        runs, accs = carry
        ks = pl.multiple_of(q0 + qb - (r + 1) * step, step)
        mask = _key_pos(ks, step, qb) < tpos
        zs, sps, parts = [], [], []
        for h in range(HEADS):
            z = _dot_nt(k_ref[0, h, pl.ds(ks, step), :], q_ref[0, h])
            sp = jnp.maximum(z, 0.0) + jnp.log(1.0 + jnp.exp(-jnp.abs(z)))
            zs.append(z), sps.append(sp)
            parts.append(_split_bf16(jnp.where(mask, -sp, 0.0)))
        stacked = jnp.concatenate(
            [jnp.concatenate([parts[h][j][b * kc:(b + 1) * kc] for j in range(2)], axis=0)
             for b in range(n_sub) for h in range(HEADS)], axis=1)
        res = _dot(tri, stacked)
        new_runs, new_accs = [], []
        for h in range(HEADS):
            run = runs[h]
            afters = [None] * n_sub
            for b in reversed(range(n_sub)):
                col = (b * HEADS + h) * qb
                afters[b] = res[:kc, col:col + qb] + run
                run = run + res[kc:kc + 1, col:col + qb]
            a = jnp.where(mask, jnp.exp(zs[h] - sps[h] + jnp.concatenate(afters, axis=0)), 0.0)
            new_accs.append(accs[h] + _dot(vt_ref[0, h, :, pl.ds(ks, step)], a.astype(BF16)))
            new_runs.append(run)
        return tuple(new_runs), tuple(new_accs)

    init = (tuple(jnp.zeros((1, qb), F32) for _ in range(HEADS)),
            tuple(jnp.zeros((HEAD_DIM, qb), F32) for _ in range(HEADS)))
    _, accs = lax.fori_loop(0, (q0 + qb) // step, body, init)
    _store_heads_t(o_ref, accs)


def _sb(q, k, vt):
    B, H, S, _ = q.shape
    qb, kc = QB_WIDE, SB_CK
    u = np.triu(np.ones((kc, kc), np.float32), 1)
    u = np.concatenate([u, np.ones((SB_TOT_ROWS, kc), np.float32)], 0)
    tri = jnp.asarray(np.concatenate([u, u], 1), BF16)
    return pl.pallas_call(
        _sb_kernel,
        out_shape=jax.ShapeDtypeStruct((B, S, D_GROUP), F32),
        grid=(B, S // qb),
        in_specs=[pl.BlockSpec((1, H, qb, 64), lambda b, i: (b, 0, i, 0)),
                  pl.BlockSpec((1, H, S, 64), lambda b, i: (b, 0, 0, 0)),
                  pl.BlockSpec((1, H, HEAD_DIM, S), lambda b, i: (b, 0, 0, 0)),
                  pl.BlockSpec(tri.shape, lambda b, i: (0, 0))],
        out_specs=pl.BlockSpec((1, qb, D_GROUP), lambda b, i: (b, i, 0)),
        compiler_params=_params("parallel", "arbitrary"),
        name="sb_attn",
    )(q, k, vt, tri)


def _ordered_to_f32(c):
    bits = c ^ ((c >> 31) & jnp.int32(0x7FFFFFFF))
    return pltpu.bitcast(bits, F32)


def _dsa_kernel(q_ref, k_ref, vt_ref, qi_ref, ki_ref, misc_ref, o_ref, sc_ref, m_ref, *, seq, n_top):
    q0 = pl.program_id(1) * QB
    ck = DSA_CK
    hk = ck // 2
    nck = _n_chunks(q0, QB, ck)
    w_t = misc_ref[0].T[MISC_IDXW:MISC_IDXW + DSA_IDX_HEADS, :] * (DSA_IDX_HEADS * DSA_IDX_DIM) ** -0.5
    qi_all = qi_ref[0].reshape(DSA_IDX_HEADS * QB, 128)
    q_all = q_ref[0].reshape(HEADS * QB, HEAD_DIM)

    def halves(c):
        ks = c * ck
        return [pl.multiple_of(ks + j * hk, hk) for j in range(2)]

    def score_body(c, _):
        starts = halves(c)
        rs = [_dot_nt(ki_ref[0, pl.ds(ks, hk), :], qi_all) for ks in starts]
        for ks, r in zip(starts, rs):
            sc = jnp.zeros((hk, QB), F32)
            for h in range(DSA_IDX_HEADS):
                sc = sc + w_t[h:h + 1, :] * jnp.maximum(r[:, h * QB:(h + 1) * QB], 0.0)
            sc = jnp.where(sc == 0.0, 0.0, sc)
            sc_ref[pl.ds(ks, hk), :] = jnp.where(_key_pos(ks, hk) <= _query_pos(q0, hk), sc, NEG_INF)
        return 0

    lax.fori_loop(0, nck, score_body, 0)

    n_beyond = (seq - nck * ck).astype(F32)
    n_part = 32

    def count(pred_fn):
        def cb(c, acc):
            ks = pl.multiple_of(c * ck, ck)
            hit = jnp.where(pred_fn(sc_ref[pl.ds(ks, ck), :], ks), 1.0, 0.0)
            return acc + jnp.sum(hit.reshape(ck // n_part, n_part, QB), axis=0)
        acc = lax.fori_loop(0, nck, cb, jnp.zeros((n_part, QB), F32))
        return jnp.sum(acc, axis=0, keepdims=True)

    def count_ge(thr):
        return count(lambda s, ks: s >= thr) + jnp.where(thr <= NEG_INF, n_beyond, 0.0)

    def thr_body(it, carry):
        t_int, cnt_t = carry
        cand = t_int ^ lax.shift_left(jnp.int32(1), 31 - it)
        cnt = count_ge(_ordered_to_f32(cand))
        ok = cnt >= n_top
        return jnp.where(ok, cand, t_int), jnp.where(ok, cnt, cnt_t)

    t_int, cnt_t = lax.fori_loop(0, 32, thr_body, (jnp.full((1, QB), -2 ** 31, jnp.int32),
                                                   jnp.full((1, QB), float(seq), F32)))
    thr = _ordered_to_f32(t_int)
    thr_up = _ordered_to_f32(t_int + 1)
    m_ref[...] = jnp.full(m_ref.shape, seq, jnp.int32)

    @pl.when(jnp.max(cnt_t) > n_top)
    def _():
        need = n_top - count_ge(thr_up)
        tie_beyond = thr <= NEG_INF

        def idx_body(it, m):
            cand = m + lax.shift_left(jnp.int32(1), (seq.bit_length() - 2) - it)
            below = count(lambda s, ks: (s >= thr) & (s < thr_up) & (_key_pos(ks, ck) < cand))
            below = below + jnp.where(tie_beyond, jnp.maximum(cand - nck * ck, 0).astype(F32), 0.0)
            return jnp.where(below < need, cand, m)

        m_idx = lax.fori_loop(0, seq.bit_length() - 1, idx_body, jnp.zeros((1, QB), jnp.int32))
        m_ref[...] = jnp.broadcast_to(m_idx, m_ref.shape)

    m_idx = m_ref[0:1, :]

    tpos = _query_pos(q0, hk)

    def attn_body(c, carry):
        ms, acc = carry
        starts = halves(c)
        ss = [_dot_nt(k_ref[0, pl.ds(ks, hk), :], q_all) for ks in starts]
        for ks, s_all in zip(starts, ss):
            kp = _key_pos(ks, hk)
            s_idx = sc_ref[pl.ds(ks, hk), :]
            sel = (s_idx >= thr_up) | ((s_idx >= thr) & (kp <= m_idx))
            mask = sel & (kp <= tpos)
            ms, alpha, p = _softmax_step_t(ms, s_all, mask)
            acc = alpha * acc + _dot(_with_ones(vt_ref[0, :, pl.ds(ks, hk)]), p)
        return tuple(ms), acc

    _, acc = lax.fori_loop(0, nck, attn_body, (tuple(_m_init() for _ in range(HEADS)), _acc_init(HEADS)))
    _store_heads_t(o_ref, [_finish_t(acc, h) for h in range(HEADS)])


def _dsa(q, k, vt, qi, ki, misc):
    B, H, S, _ = q.shape
    n_top = min(DSA_TOPK, S // 4)
    return pl.pallas_call(
        functools.partial(_dsa_kernel, seq=S, n_top=n_top),
        out_shape=jax.ShapeDtypeStruct((B, S, D_GROUP), F32),
        grid=(B, S // QB),
        in_specs=[pl.BlockSpec((1, H, QB, 64), lambda b, i: (b, 0, i, 0)),
                  pl.BlockSpec((1, S, 64), lambda b, i: (b, 0, 0)),
                  pl.BlockSpec((1, HEAD_DIM, S), lambda b, i: (b, 0, 0)),
                  pl.BlockSpec((1, DSA_IDX_HEADS, QB, 128), lambda b, i: (b, 0, i, 0)),
                  pl.BlockSpec((1, S, 128), lambda b, i: (b, 0, 0)),
                  pl.BlockSpec((1, QB, 128), lambda b, i: (b, i, 0))],
        out_specs=pl.BlockSpec((1, QB, D_GROUP), lambda b, i: (b, i, 0)),
        scratch_shapes=[pltpu.VMEM((S, QB), F32), pltpu.VMEM((8, QB), jnp.int32)],
        compiler_params=_params("parallel", "arbitrary"),
        name="dsa_attn",
    )(q, k, vt, qi, ki, misc)


def _gelu_tanh(x):
    return 0.5 * x * (1.0 + jnp.tanh(np.sqrt(2.0 / np.pi) * (x + 0.044715 * (x * x * x))))


def _nsa_cmp_kernel(kg_ref, vg_ref, pos_ref, wk1_ref, wk2_ref, wv1_ref, wv2_ref, kc_ref, vct_ref):
    def compress(g_ref, w1_ref, w2_ref):
        top = _dot((g_ref[0] + pos_ref[0:1, :]).astype(BF16), w1_ref[0])
        bot = _dot((g_ref[0] + pos_ref[1:2, :]).astype(BF16), w1_ref[1])
        pre = top + pltpu.roll(bot, bot.shape[0] - 1, 0)
        return _dot(_gelu_tanh(pre).astype(BF16), w2_ref[...])

    kc_ref[0] = compress(kg_ref, wk1_ref, wk2_ref).astype(BF16)
    vct_ref[0] = compress(vg_ref, wv1_ref, wv2_ref).T[:HEAD_DIM].astype(BF16)


def _nsa_compress(kcmp, vcmp, pos, wk1, wk2, wv1, wv2):
    B, S, _ = kcmp.shape
    ng = S // NSA_CMP_STRIDE
    gw = NSA_CMP_STRIDE * HEAD_DIM
    kg = kcmp.reshape(B, ng, gw)
    vg = vcmp.reshape(B, ng, gw)

    def const(a):
        return pl.BlockSpec(a.shape, lambda b: (0,) * a.ndim)

    wv2p = jnp.concatenate([wv2, jnp.zeros((HEAD_DIM, LANES - HEAD_DIM), wv2.dtype)], axis=1)
    ins = [kg, vg, pos.reshape(2, gw), wk1.reshape(2, gw, HEAD_DIM).astype(BF16), wk2.astype(BF16),
           wv1.reshape(2, gw, HEAD_DIM).astype(BF16), wv2p.astype(BF16)]
    blk = pl.BlockSpec((1, ng, gw), lambda b: (b, 0, 0))
    return pl.pallas_call(
        _nsa_cmp_kernel,
        out_shape=[jax.ShapeDtypeStruct((B, ng, HEAD_DIM), BF16), jax.ShapeDtypeStruct((B, HEAD_DIM, ng), BF16)],
        grid=(B,),
        in_specs=[blk, blk] + [const(a) for a in ins[2:]],
        out_specs=[pl.BlockSpec((1, ng, HEAD_DIM), lambda b: (b, 0, 0)),
                   pl.BlockSpec((1, HEAD_DIM, ng), lambda b: (b, 0, 0))],
        compiler_params=_params("parallel"),
        name="nsa_compress",
    )(*ins)


def _nsa_kernel(q_ref, kc_ref, vct_ref, ks_ref, vst_ref, kw_ref, vwt_ref, misc_ref, ovl_ref, o_ref,
                *, seq):
    i = pl.program_id(1)
    q0 = i * QB
    n_slc = seq // NSA_SEL_LEN
    n_cmp = kc_ref.shape[1]
    q_all = q_ref[0].reshape(HEADS * QB, HEAD_DIM)

    last_tok = lax.broadcasted_iota(jnp.int32, (n_cmp, QB), 0) * NSA_CMP_STRIDE + (NSA_CMP_LEN - 1)
    cmask = last_tok <= _query_pos(q0, n_cmp)
    s_cmp = _dot_nt(kc_ref[0], q_all)
    n_win = NSA_WINDOW + QB
    ws = pl.multiple_of(jnp.maximum(q0 - NSA_WINDOW, 0), QB)
    s_win = _dot_nt(kw_ref[0, pl.ds(ws, n_win), :], q_all)
    ps = []
    p_sum = jnp.zeros((n_cmp, QB), F32)
    for h in range(HEADS):
        s = jnp.where(cmask, s_cmp[:, h * QB:(h + 1) * QB], NEG_INF)
        e = jnp.where(cmask, jnp.exp(s - jnp.max(s, axis=0, keepdims=True)), 0.0)
        p = e / jnp.maximum(jnp.sum(e, axis=0, keepdims=True), 1e-30)
        p_sum = p_sum + p
        ps.append(p.astype(BF16))
    o_cmp = _dot(vct_ref[0], jnp.concatenate(ps, axis=1))

    hi, lo = _split_bf16(p_sum)
    imp = _dot(ovl_ref[...], jnp.concatenate([hi, lo], axis=0))
    jb = lax.broadcasted_iota(jnp.int32, (n_slc, 1), 0)
    cur = (q0 + lax.broadcasted_iota(jnp.int32, (1, QB), 1)) >> SEL_SHIFT
    forced = (jb < NSA_N_INIT) | ((jb <= cur) & (jb > cur - NSA_N_LOCAL))
    imp = jnp.where(forced, FORCE_SCORE, jnp.where(jb <= cur, imp, NEG_INF))
    rank = jnp.zeros((n_slc, QB), F32)
    for r in range(n_slc):
        row = imp[r:r + 1, :]
        beats = (row > imp) | ((row == imp) & (r < jb))
        rank = rank + jnp.where(beats, 1.0, 0.0)
    sel_t = jnp.where(rank < min(NSA_N_SEL, n_slc), 1.0, 0.0)
    sel_t = jnp.concatenate([sel_t, jnp.zeros((LANES - n_slc, QB), F32)], axis=0).astype(BF16)

    dist = _query_pos(q0, n_win) - _key_pos(ws, n_win)
    wmask = (dist >= 0) & (dist < NSA_WINDOW)
    _, _, p_win = _softmax_step_t([_m_init() for _ in range(HEADS)], s_win, wmask)
    acc_win = _dot(_with_ones(vwt_ref[0, :, pl.ds(ws, n_win)]), p_win)

    ck = DSA_CK
    hk = ck // 2
    tpos = _query_pos(q0, hk)

    def slc_body(c, carry):
        ms, acc = carry
        starts = [pl.multiple_of(c * ck + j * hk, hk) for j in range(2)]
        ss = [_dot_nt(ks_ref[0, pl.ds(ks, hk), :], q_all) for ks in starts]
        for ks, s_all in zip(starts, ss):
            blk = (ks + lax.broadcasted_iota(jnp.int32, (hk, LANES), 0)) >> SEL_SHIFT
            expand = jnp.where(blk == lax.broadcasted_iota(jnp.int32, (hk, LANES), 1), 1.0, 0.0)
            chosen = _dot(expand.astype(BF16), sel_t) > 0.5
            mask = chosen & (_key_pos(ks, hk) <= tpos)
            ms, alpha, p = _softmax_step_t(ms, s_all, mask)
            acc = alpha * acc + _dot(_with_ones(vst_ref[0, :, pl.ds(ks, hk)]), p)
        return tuple(ms), acc

    _, acc_slc = lax.fori_loop(0, _n_chunks(q0, QB, ck), slc_body,
                               (tuple(_m_init() for _ in range(HEADS)), _acc_init(HEADS)))

    gate_rows = 16
    gate = jax.nn.sigmoid(misc_ref[0].T[MISC_GATE:MISC_GATE + gate_rows, :])
    outs = []
    for h in range(HEADS):
        outs.append(gate[h:h + 1] * o_cmp[:, h * QB:(h + 1) * QB]
                    + gate[HEADS + h:HEADS + h + 1] * _finish_t(acc_slc, h)
                    + gate[2 * HEADS + h:2 * HEADS + h + 1] * _finish_t(acc_win, h))
    _store_heads_t(o_ref, outs)


def _nsa_overlap(seq):
    ng = seq // NSA_CMP_STRIDE
    n_slc = seq // NSA_SEL_LEN
    n = np.arange(ng)
    first, last = n * NSA_CMP_STRIDE, n * NSA_CMP_STRIDE + NSA_CMP_LEN - 1
    start = np.arange(n_slc) * NSA_SEL_LEN
    ovl = ((first[None, :] <= start[:, None] + NSA_SEL_LEN - 1) & (last[None, :] >= start[:, None]))
    ovl = ovl & (last[None, :] < seq)
    ovl = ovl.astype(np.float32)
    return jnp.asarray(np.concatenate([ovl, ovl], 1), BF16)


def _nsa(q, kc, vct, k_slc, vt_slc, k_win, vt_win, misc):
    B, H, S, _ = q.shape
    ng = kc.shape[1]
    ovl = _nsa_overlap(S)
    keys = pl.BlockSpec((1, S, HEAD_DIM), lambda b, i: (b, 0, 0))
    vals_t = pl.BlockSpec((1, HEAD_DIM, S), lambda b, i: (b, 0, 0))
    return pl.pallas_call(
        functools.partial(_nsa_kernel, seq=S),
        out_shape=jax.ShapeDtypeStruct((B, S, D_GROUP), F32),
        grid=(B, S // QB),
        in_specs=[pl.BlockSpec((1, H, QB, 64), lambda b, i: (b, 0, i, 0)),
                  pl.BlockSpec((1, ng, HEAD_DIM), lambda b, i: (b, 0, 0)),
                  pl.BlockSpec((1, HEAD_DIM, ng), lambda b, i: (b, 0, 0)),
                  keys, vals_t, keys, vals_t,
                  pl.BlockSpec((1, QB, 128), lambda b, i: (b, i, 0)),
                  pl.BlockSpec(ovl.shape, lambda b, i: (0, 0))],
        out_specs=pl.BlockSpec((1, QB, D_GROUP), lambda b, i: (b, i, 0)),
        compiler_params=_params("parallel", "arbitrary"),
        name="nsa_attn",
    )(q, kc, vct, k_slc, vt_slc, k_win, vt_win, misc, ovl)


def _outproj_kernel(x_ref, ya_ref, yb_ref, yc_ref, yd_ref, gg_ref, wo_ref, g_ref, b_ref, o_ref):
    acc = None
    for gi, y_ref in enumerate((ya_ref, yb_ref, yc_ref, yd_ref)):
        lo, hi = gi * D_GROUP, (gi + 1) * D_GROUP
        yn = _rms_norm(y_ref[...], gg_ref[:, lo:hi]).astype(BF16)
        part = _dot(yn, wo_ref[lo:hi, :])
        acc = part if acc is None else acc + part
    o_ref[...] = _layer_norm(DN_ALPHA * x_ref[...] + acc, g_ref[...], b_ref[...])


def _outproj(x2, ys, gg, wo, g, b, *, tm):
    n, d = x2.shape
    row = lambda w: pl.BlockSpec((tm, w), lambda i: (i, 0))
    const = lambda a: pl.BlockSpec(a.shape, lambda i: (0, 0))
    return pl.pallas_call(
        _outproj_kernel,
        out_shape=jax.ShapeDtypeStruct((n, d), F32),
        grid=(n // tm,),
        in_specs=[row(d)] + [row(D_GROUP)] * 4 + [const(gg), const(wo), const(g), const(b)],
        out_specs=row(d),
        compiler_params=_params("parallel"),
        name="out_proj",
    )(x2, *ys, gg, wo, g, b)


def _mixer(x, w_in, q_norm, w_uq, kv_norm, w_ukv, cmp_pos, wk1, wk2, wv1, wv2, tabs, place, maps):
    B, S, D = x.shape
    main_cols, rot_cols = maps
    wm = _gather_cols(w_in, main_cols).astype(BF16)
    wr = _gather_cols(w_in, rot_cols).astype(BF16)
    wt = jnp.concatenate([wm[:, slice(*_MAIN_OFF['slab_v'])], wm[:, slice(*_MAIN_OFF['sb_v'])]], axis=1).T
    (mq, mk, mvt, nq, dq, sbq, sbk, sbvt, kcmp, kslc, kwin, dk, vcmp, vslct, vwint, dvt,
     qi, ki, misc) = _inproj(x, wm, wr, wt, tabs, q_norm[None, :], kv_norm[None, :],
                             _mla_weights(w_uq, w_ukv), place, tm=min(512, S))
    y_a = _mla(mq, mk, mvt)
    kc, vct = _nsa_compress(kcmp, vcmp, cmp_pos, wk1, wk2, wv1, wv2)
    y_b = _nsa(nq, kc, vct, kslc, vslct, kwin, vwint, misc)
    y_c = _dsa(dq, dk, dvt, qi, ki, misc)
    y_d = _sb(sbq, sbk, sbvt)
    return y_a, y_b, y_c, y_d


def kernel(x, ln1_g, ln1_b, ffn1_w1, ffn1_w3, ffn1_w2, w_in, mla_q_norm, mla_w_uq, mla_kv_norm,
           mla_w_ukv, nsa_cmp_pos, nsa_cmp_wk1, nsa_cmp_wk2, nsa_cmp_wv1, nsa_cmp_wv2, group_norm_g,
           w_out, ln2_g, ln2_b, ffn2_w1, ffn2_w3, ffn2_w2, ln3_g, ln3_b):
    B, S, D = x.shape
    n = B * S
    tabs = _rope_tables(S)
    place = _placement_constants()
    maps = _column_maps()
    tm_ffn = min(512, n)
    x2 = x.reshape(n, D)
    for l in range(DEPTH):
        x2 = _ffn_ln(x2, ffn1_w1[l].astype(BF16), ffn1_w3[l].astype(BF16), ffn1_w2[l].astype(BF16),
                     ln1_g[l][None, :], ln1_b[l][None, :], tm=tm_ffn, tf=256)
        ys = _mixer(x2.reshape(B, S, D), w_in[l], mla_q_norm[l], mla_w_uq[l], mla_kv_norm[l],
                    mla_w_ukv[l], nsa_cmp_pos[l], nsa_cmp_wk1[l], nsa_cmp_wk2[l], nsa_cmp_wv1[l],
                    nsa_cmp_wv2[l], tabs, place, maps)
        x2 = _outproj(x2, [y.reshape(n, D_GROUP) for y in ys], group_norm_g[l][None, :],
                      w_out[l].astype(BF16), ln2_g[l][None, :], ln2_b[l][None, :], tm=min(512, n))
        x2 = _ffn_ln(x2, ffn2_w1[l].astype(BF16), ffn2_w3[l].astype(BF16), ffn2_w2[l].astype(BF16),
                     ln3_g[l][None, :], ln3_b[l][None, :], tm=tm_ffn, tf=256)
    return x2.reshape(B, S, D)
```

```python
import functools

import numpy as np
import jax
import jax.numpy as jnp
from jax import lax
from jax.experimental import pallas as pl
from jax.experimental.pallas import tpu as pltpu

F32 = jnp.float32
BF16 = jnp.bfloat16

D_MODEL = 1024
DEPTH = 2
HEADS = 4
HEAD_DIM = 64
D_GROUP = HEADS * HEAD_DIM
N_GROUPS = 4
D_FF = 2816
ROPE_THETA = 10000.0
MLA_Q_RANK = 256
MLA_KV_RANK = 128
MLA_D_NOPE = 64
MLA_D_ROPE = 32
MLA_D_V = 64
NSA_CMP_LEN = 32
NSA_CMP_STRIDE = 16
NSA_SEL_LEN = 64
NSA_N_SEL = 8
NSA_N_INIT = 1
NSA_N_LOCAL = 2
NSA_WINDOW = 512
SEL_SHIFT = NSA_SEL_LEN.bit_length() - 1
DSA_TOPK = 256
DSA_IDX_HEADS = 8
DSA_IDX_DIM = 32
DN_ALPHA = (2.0 * DEPTH) ** 0.25
LN_EPS = 1e-5
RMS_EPS = 1e-6
NEG_INF = -1e30
FORCE_SCORE = 1e4
LOG2_E = 1.4426950408889634

LANES = 128
QB = 128
CK = 256
DSA_CK = 512
NSA_WIN_PARTS = ((0, NSA_WINDOW + QB),)
VMEM_LIMIT = 48 * 1024 * 1024

MISC_KROPE = 0
MISC_IDXK = 32
MISC_IDXW = 64
MISC_GATE = 72

NT_DIMS = (((1,), (1,)), ((), ()))


def _dot(a, b):
    return jnp.dot(a, b, preferred_element_type=F32)


def _dot_nt(a, b):
    return lax.dot_general(a, b, NT_DIMS, preferred_element_type=F32)


def _split_bf16(x):
    hi = x.astype(BF16)
    lo = (x - hi.astype(F32)).astype(BF16)
    return hi, lo


def _layer_norm(y, g, b):
    mu = jnp.mean(y, -1, keepdims=True)
    d = y - mu
    var = jnp.mean(d * d, -1, keepdims=True)
    return d * lax.rsqrt(var + LN_EPS) * g + b


def _rms_norm(y, g):
    return y * lax.rsqrt(jnp.mean(y * y, -1, keepdims=True) + RMS_EPS) * g


def _params(*sem):
    return pltpu.CompilerParams(dimension_semantics=sem, vmem_limit_bytes=VMEM_LIMIT)


def _ffn_kernel(x_ref, w1_ref, w3_ref, w2_ref, g_ref, b_ref, o_ref, a_ref, *, tf):
    xb = x_ref[...].astype(BF16)
    n_slabs = w1_ref.shape[1] // tf

    def up(j):
        return _dot(xb, w1_ref[:, j * tf:(j + 1) * tf]), _dot(xb, w3_ref[:, j * tf:(j + 1) * tf])

    hu = up(0)
    for j in range(n_slabs):
        nxt = up(j + 1) if j + 1 < n_slabs else None
        h, u = hu
        a_ref[:, j * tf:(j + 1) * tf] = (h * jax.nn.sigmoid(h) * u).astype(BF16)
        hu = nxt
    y = DN_ALPHA * x_ref[...] + 0.5 * _dot(a_ref[...], w2_ref[...])
    o_ref[...] = _layer_norm(y, g_ref[...], b_ref[...])


def _ffn_ln(x2, w1, w3, w2, g, b, *, tm, tf):
    n, d = x2.shape
    dff = w1.shape[1]
    const = lambda a: pl.BlockSpec(a.shape, lambda i: (0, 0))
    return pl.pallas_call(
        functools.partial(_ffn_kernel, tf=tf),
        out_shape=jax.ShapeDtypeStruct((n, d), F32),
        grid=(n // tm,),
        in_specs=[pl.BlockSpec((tm, d), lambda i: (i, 0)), const(w1), const(w3), const(w2), const(g), const(b)],
        out_specs=pl.BlockSpec((tm, d), lambda i: (i, 0)),
        scratch_shapes=[pltpu.VMEM((tm, dff), BF16)],
        compiler_params=_params("parallel"),
        name="ffn_ln",
    )(x2, w1, w3, w2, g, b)


_MAIN_ORDER = (
    ('mla_cq', 256), ('nsa_q', 256), ('dsa_q', 256), ('idx_q', 256),
    ('sb_q', 256), ('sb_k', 256), ('sb_v', 256),
    ('slab_k', 256),
    ('slab_v', 256),
    ('mla_ckv', 128),
    ('misc', 128),
)
_MAIN_OFF = {}
_o = 0
for _n, _w in _MAIN_ORDER:
    _MAIN_OFF[_n] = (_o, _o + _w)
    _o += _w
N_MAIN = _o

_IN_SPLITS = (
    ('mla_cq', MLA_Q_RANK), ('mla_ckv', MLA_KV_RANK), ('mla_krope', MLA_D_ROPE),
    ('nsa_q', D_GROUP), ('nsa_k_cmp', HEAD_DIM), ('nsa_v_cmp', HEAD_DIM),
    ('nsa_k_slc', HEAD_DIM), ('nsa_v_slc', HEAD_DIM), ('nsa_k_win', HEAD_DIM),
    ('nsa_v_win', HEAD_DIM), ('nsa_gate', 3 * HEADS),
    ('dsa_q', D_GROUP), ('dsa_k', HEAD_DIM), ('dsa_v', HEAD_DIM),
    ('idx_q', DSA_IDX_HEADS * DSA_IDX_DIM), ('idx_k', DSA_IDX_DIM), ('idx_w', DSA_IDX_HEADS),
    ('sb_q', D_GROUP), ('sb_k', D_GROUP), ('sb_v', D_GROUP),
)
_SRC = {}
_o = 0
for _n, _w in _IN_SPLITS:
    _SRC[_n] = np.arange(_o, _o + _w)
    _o += _w
D_IN = _o


def _swap_halves(width, dim):
    idx = np.arange(width)
    return (idx // dim) * dim + (idx % dim + dim // 2) % dim


def _column_maps():
    pieces = {
        'mla_cq': [_SRC['mla_cq']], 'nsa_q': [_SRC['nsa_q']], 'dsa_q': [_SRC['dsa_q']],
        'idx_q': [_SRC['idx_q']], 'sb_q': [_SRC['sb_q']], 'sb_k': [_SRC['sb_k']], 'sb_v': [_SRC['sb_v']],
        'slab_k': [_SRC['nsa_k_cmp'], _SRC['nsa_k_slc'], _SRC['nsa_k_win'], _SRC['dsa_k']],
        'slab_v': [_SRC['nsa_v_cmp'], _SRC['nsa_v_slc'], _SRC['nsa_v_win'], _SRC['dsa_v']],
        'mla_ckv': [_SRC['mla_ckv']],
        'misc': [_SRC['mla_krope'], _SRC['idx_k'], _SRC['idx_w'], _SRC['nsa_gate'],
                 -np.ones(LANES - MISC_GATE - 3 * HEADS, np.int64)],
    }
    return np.concatenate([np.concatenate(pieces[n]) for n, _ in _MAIN_ORDER])


def _gather_cols(w, cols):
    out = jnp.take(w, np.maximum(cols, 0), axis=1)
    return jnp.where(jnp.asarray(cols >= 0)[None, :], out, 0.0)


def _rope_tables(seq):
    def base(dim):
        inv = ROPE_THETA ** (-jnp.arange(0, dim, 2, dtype=F32) / dim)
        ang = jnp.arange(seq, dtype=F32)[:, None] * inv[None, :]
        c, s = jnp.cos(ang), jnp.sin(ang)
        return jnp.concatenate([c, c], -1), jnp.concatenate([-s, s], -1)

    c64, s64 = base(HEAD_DIM)
    c32, s32 = base(DSA_IDX_DIM)
    t64 = (jnp.tile(c64, (1, 4)), jnp.tile(s64, (1, 4)))
    t32 = (jnp.tile(c32, (1, 8)), jnp.tile(s32, (1, 8)))
    ones = jnp.ones((seq, LANES - 64), F32)
    td = (jnp.concatenate([c32, c32, ones], -1), jnp.concatenate([s32, s32, 0.0 * ones], -1))
    scale = (MLA_D_NOPE + MLA_D_ROPE) ** -0.5 * LOG2_E
    cq = jnp.concatenate([jnp.ones((seq, 64), F32), c32, jnp.zeros((seq, 32), F32)], -1) * scale
    sq = jnp.concatenate([jnp.zeros((seq, 64), F32), s32, jnp.zeros((seq, 32), F32)], -1) * scale
    tq = (jnp.tile(cq, (1, 4)), jnp.tile(sq, (1, 4)))
    return t64, t32, td, tq


def _placement_constants():
    p_hi = np.zeros((256, 1024), np.float32)
    p_lo = np.zeros((256, 1024), np.float32)
    for h in range(DSA_IDX_HEADS):
        for d in range(DSA_IDX_DIM):
            p_hi[h * 32 + d, h * 128 + d] = 1
            p_hi[h * 32 + d, h * 128 + 64 + d] = 1
            p_lo[h * 32 + d, h * 128 + 32 + d] = 1
    pk_hi = np.zeros((128, 128), np.float32)
    pk_lo = np.zeros((128, 128), np.float32)
    for d in range(DSA_IDX_DIM):
        pk_hi[MISC_IDXK + d, d] = 1
        pk_hi[MISC_IDXK + d, 32 + d] = 1
        pk_lo[MISC_IDXK + d, 64 + d] = 1
    pk_pe = np.zeros((128, 512), np.float32)
    for h in range(HEADS):
        for d in range(MLA_D_ROPE):
            pk_pe[MISC_KROPE + d, h * 128 + 64 + d] = 1
    return tuple(jnp.asarray(a, BF16) for a in (p_hi, p_lo, pk_hi, pk_lo, pk_pe))


def _mla_weights(w_uq, w_ukv):
    dq = MLA_D_NOPE + MLA_D_ROPE
    cols_q = -np.ones(512, np.int64)
    cols_qr = -np.ones(512, np.int64)
    cols_k = -np.ones(512, np.int64)
    cols_v = np.zeros(256, np.int64)
    for h in range(HEADS):
        cols_q[h * 128:h * 128 + dq] = h * dq + np.arange(dq)
        pe = h * dq + MLA_D_NOPE + np.arange(MLA_D_ROPE)
        cols_qr[h * 128 + 64:h * 128 + 96] = pe[_swap_halves(32, 32)]
        cols_k[h * 128:h * 128 + 64] = h * 128 + np.arange(64)
        cols_v[h * 64:(h + 1) * 64] = h * 128 + 64 + np.arange(64)
    return (_gather_cols(w_uq, cols_q).astype(BF16), _gather_cols(w_uq, cols_qr).astype(BF16),
            _gather_cols(w_ukv, cols_k).astype(BF16), jnp.take(w_ukv, cols_v, axis=1).T.astype(BF16))


def _rotate_half(z, dim):
    n = z.shape[1]
    half = dim // 2
    lane = lax.broadcasted_iota(jnp.int32, z.shape, 1)
    in_first_half = (lane & (dim - 1)) < half
    return jnp.where(in_first_half, pltpu.roll(z, n - half, 1), pltpu.roll(z, half, 1))


def _inproj_kernel(x_ref, wm_ref, wt_ref, c64_ref, s64_ref, c32_ref, s32_ref, cd_ref, sd_ref,
                   cq_ref, sq_ref, qg_ref, kvg_ref, wuq_ref, wuqr_ref, wuk_ref, wuvt_ref,
                   phi_ref, plo_ref, pkhi_ref, pklo_ref, pkpe_ref,
                   mq_ref, mk_ref, mvt_ref, nq_ref, dq_ref, sbq_ref, sbk_ref, sbvt_ref,
                   kcmp_ref, kslc_ref, kwin_ref, dk_ref, vcmp_ref, vslct_ref, vwint_ref, dvt_ref,
                   qi_ref, ki_ref, misc_ref):
    xb = x_ref[0].astype(BF16)

    def main(name):
        a, b = _MAIN_OFF[name]
        return _dot(xb, wm_ref[:, a:b])

    def roped(name, c_ref, s_ref, dim):
        z = main(name)
        return z * c_ref[...] + _rotate_half(z, dim) * s_ref[...]

    def store_heads(ref, val, width):
        for h in range(HEADS):
            ref[0, h] = val[:, h * width:(h + 1) * width].astype(ref.dtype)

    cqn = _rms_norm(main('mla_cq'), qg_ref[...]).astype(BF16)
    q = _dot(cqn, wuq_ref[...]) * cq_ref[...] + _dot(cqn, wuqr_ref[...]) * sq_ref[...]
    store_heads(mq_ref, q, 128)
    ckv_misc = _dot(xb, wm_ref[:, _MAIN_OFF['mla_ckv'][0]:_MAIN_OFF['misc'][1]])
    z_misc = ckv_misc[:, LANES:]
    misc = z_misc * cd_ref[...] + _rotate_half(z_misc, DSA_IDX_DIM) * sd_ref[...]
    misc_ref[0] = misc
    m_hi, m_lo = _split_bf16(misc)
    ckvn = _rms_norm(ckv_misc[:, :LANES], kvg_ref[...]).astype(BF16)
    k = _dot(ckvn, wuk_ref[...]) + _dot(m_hi, pkpe_ref[...])
    store_heads(mk_ref, k, 128)
    mvt = _dot_nt(wuvt_ref[...], ckvn)
    for h in range(HEADS):
        mvt_ref[0, h] = mvt[h * 64:(h + 1) * 64, :].astype(BF16)
    qscale = HEAD_DIM ** -0.5 * LOG2_E
    store_heads(nq_ref, roped('nsa_q', c64_ref, s64_ref, HEAD_DIM) * qscale, 64)
    store_heads(dq_ref, roped('dsa_q', c64_ref, s64_ref, HEAD_DIM) * qscale, 64)
    store_heads(sbq_ref, main('sb_q') * qscale, 64)
    store_heads(sbk_ref, main('sb_k'), 64)
    sk = roped('slab_k', c64_ref, s64_ref, HEAD_DIM)
    for j, ref in enumerate((kcmp_ref, kslc_ref, kwin_ref, dk_ref)):
        ref[0] = sk[:, j * 64:(j + 1) * 64].astype(ref.dtype)
    vcmp_ref[0] = main('slab_v')[:, 0:64]
    vt = _dot_nt(wt_ref[...], xb)
    for j, ref in ((1, vslct_ref), (2, vwint_ref), (3, dvt_ref)):
        ref[0] = vt[j * 64:(j + 1) * 64, :].astype(BF16)
    for h in range(HEADS):
        sbvt_ref[0, h] = vt[256 + h * 64:256 + (h + 1) * 64, :].astype(BF16)
    qi_hi, qi_lo = _split_bf16(roped('idx_q', c32_ref, s32_ref, DSA_IDX_DIM))
    qi = _dot(qi_hi, phi_ref[...]) + _dot(qi_lo, plo_ref[...])
    for h in range(DSA_IDX_HEADS):
        qi_ref[0, h] = qi[:, h * 128:(h + 1) * 128].astype(BF16)
    ki_ref[0] = (_dot(m_hi, pkhi_ref[...]) + _dot(m_lo, pklo_ref[...])).astype(BF16)


def _inproj(x, wm, wt, tabs, qg, kvg, mla_w, place, *, tm):
    B, S, D = x.shape
    (c64, s64), (c32, s32), (cd, sd), (cq, sq) = tabs
    wuq, wuqr, wuk, wuvt = mla_w

    def const(a):
        return pl.BlockSpec(a.shape, lambda b, s: (0,) * a.ndim)

    def tab(a):
        return pl.BlockSpec((tm, a.shape[1]), lambda b, s: (s, 0))

    def heads_out(w, dt=BF16):
        return (jax.ShapeDtypeStruct((B, HEADS, S, w), dt),
                pl.BlockSpec((1, HEADS, tm, w), lambda b, s: (b, 0, s, 0)))

    def flat_out(w, dt):
        return (jax.ShapeDtypeStruct((B, S, w), dt), pl.BlockSpec((1, tm, w), lambda b, s: (b, s, 0)))

    heads_t = (jax.ShapeDtypeStruct((B, HEADS, HEAD_DIM, S), BF16),
               pl.BlockSpec((1, HEADS, HEAD_DIM, tm), lambda b, s: (b, 0, 0, s)))
    flat_t = (jax.ShapeDtypeStruct((B, HEAD_DIM, S), BF16),
              pl.BlockSpec((1, HEAD_DIM, tm), lambda b, s: (b, 0, s)))
    qi_out = (jax.ShapeDtypeStruct((B, DSA_IDX_HEADS, S, 128), BF16),
              pl.BlockSpec((1, DSA_IDX_HEADS, tm, 128), lambda b, s: (b, 0, s, 0)))
    outs = [heads_out(128), heads_out(128), heads_t,
            heads_out(64), heads_out(64),
            heads_out(64), heads_out(64), heads_t,
            flat_out(64, F32), flat_out(64, BF16), flat_out(64, BF16), flat_out(64, BF16),
            flat_out(64, F32), flat_t, flat_t, flat_t,
            qi_out, flat_out(128, BF16), flat_out(128, F32)]
    ins = [x, wm, wt, c64, s64, c32, s32, cd, sd, cq, sq, qg, kvg, wuq, wuqr, wuk, wuvt, *place]
    in_specs = [pl.BlockSpec((1, tm, D), lambda b, s: (b, s, 0)), const(wm), const(wt),
                tab(c64), tab(s64), tab(c32), tab(s32), tab(cd), tab(sd), tab(cq), tab(sq),
                const(qg), const(kvg), const(wuq), const(wuqr), const(wuk), const(wuvt),
                *[const(p) for p in place]]
    return pl.pallas_call(
        _inproj_kernel,
        out_shape=[o[0] for o in outs],
        grid=(B, S // tm),
        in_specs=in_specs,
        out_specs=[o[1] for o in outs],
        compiler_params=_params("parallel", "arbitrary"),
        name="in_proj",
    )(*ins)


V_ROWS = HEAD_DIM


def _acc_init(n_heads, qb=QB):
    return jnp.zeros((2 * V_ROWS, n_heads * qb), F32)


def _m_init(qb=QB):
    return jnp.full((1, qb), NEG_INF, F32)


def _with_ones(vt):
    return jnp.concatenate([vt, jnp.ones((V_ROWS, vt.shape[1]), vt.dtype)], axis=0)


def _softmax_step_t(ms, s_all, mask):
    new_ms, alphas, es = [], [], []
    qb = mask.shape[1]
    for h, m in enumerate(ms):
        s = jnp.where(mask, s_all[:, h * qb:(h + 1) * qb], NEG_INF)
        m_new = jnp.maximum(m, jnp.max(s, axis=0, keepdims=True))
        alphas.append(jnp.exp2(m - m_new))
        es.append(jnp.where(mask, jnp.exp2(s - m_new), 0.0).astype(BF16))
        new_ms.append(m_new)
    return new_ms, jnp.concatenate(alphas, axis=1), jnp.concatenate(es, axis=1)


def _finish_t(acc, h, qb=QB):
    blk = acc[:, h * qb:(h + 1) * qb]
    return blk[:V_ROWS] / jnp.maximum(blk[V_ROWS:V_ROWS + 1], 1e-30)


def _store_heads_t(o_ref, outs_t):
    for p in range(HEADS // 2):
        pair = jnp.concatenate([outs_t[2 * p], outs_t[2 * p + 1]], axis=0)
        o_ref[0, :, p * 128:(p + 1) * 128] = pair.T


def _chunk_loop(n, scores, consume, init):
    return lax.fori_loop(0, n, lambda c, state: consume(c, scores(c), state), init)


def _n_chunks(q0, qb=QB, ck=CK):
    return (q0 + qb + ck - 1) // ck


def _key_pos(ks, n, qb=QB):
    return ks + lax.broadcasted_iota(jnp.int32, (n, qb), 0)


def _query_pos(q0, n, qb=QB):
    return q0 + lax.broadcasted_iota(jnp.int32, (n, qb), 1)


QB_WIDE = 256


def _mla_kernel(q_ref, k_ref, vt_ref, o_ref):
    qb = QB_WIDE
    q0 = pl.program_id(1) * qb
    tpos = _query_pos(q0, CK, qb)

    n = _n_chunks(q0, qb)

    def scores(c):
        ks = pl.multiple_of(c * CK, CK)
        return tuple(_dot_nt(k_ref[0, h, pl.ds(ks, CK), :], q_ref[0, h]) for h in range(HEADS))

    def consume(c, ss, carry):
        ms, accs = carry
        ks = pl.multiple_of(c * CK, CK)
        mask = _key_pos(ks, CK, qb) <= tpos
        steps = [_softmax_step_t([ms[h]], ss[h], mask) for h in range(HEADS)]
        new_accs = [steps[h][1] * accs[h] + _dot(_with_ones(vt_ref[0, h, :, pl.ds(ks, CK)]), steps[h][2])
                    for h in range(HEADS)]
        return tuple(st[0][0] for st in steps), tuple(new_accs)

    init = (tuple(_m_init(qb) for _ in range(HEADS)), tuple(_acc_init(1, qb) for _ in range(HEADS)))
    _, accs = _chunk_loop(n, scores, consume, init)
    _store_heads_t(o_ref, [_finish_t(accs[h], 0, qb) for h in range(HEADS)])


def _mla(q, k, vt):
    B, H, S, _ = q.shape
    qb = QB_WIDE
    return pl.pallas_call(
        _mla_kernel,
        out_shape=jax.ShapeDtypeStruct((B, S, D_GROUP), F32),
        grid=(B, S // qb),
        in_specs=[pl.BlockSpec((1, H, qb, 128), lambda b, i: (b, 0, i, 0)),
                  pl.BlockSpec((1, H, S, 128), lambda b, i: (b, 0, 0, 0)),
                  pl.BlockSpec((1, H, HEAD_DIM, S), lambda b, i: (b, 0, 0, 0))],
        out_specs=pl.BlockSpec((1, qb, D_GROUP), lambda b, i: (b, i, 0)),
        compiler_params=_params("parallel", "arbitrary"),
        name="mla_attn",
    )(q, k, vt)


SB_TOT_ROWS = 16


SB_CK = 128


def _sb_kernel(q_ref, k_ref, vt_ref, tri_ref, o_ref):
    qb, kc = QB_WIDE, SB_CK
    n_sub = 2
    step = n_sub * kc
    q0 = pl.program_id(1) * qb
    n_steps = (q0 + qb) // step
    tri = tri_ref[...]

    def key_start(r):
        return pl.multiple_of(q0 + qb - (r + 1) * step, step)

    def scores(r):
        ks = key_start(r)
        return tuple(_dot_nt(k_ref[0, h, pl.ds(ks, step), :], q_ref[0, h]) for h in range(HEADS))

    def consume(r, zs, carry, masked):
        runs, accs = carry
        ks = key_start(r)
        if masked:
            mask = _key_pos(ks, step, qb) < _query_pos(q0, step, qb)
        sps, parts = [], []
        for h in range(HEADS):
            sp = jnp.maximum(zs[h], 0.0) + jnp.log2(1.0 + jnp.exp2(-jnp.abs(zs[h])))
            sps.append(sp)
            parts.append(_split_bf16(jnp.where(mask, sp, 0.0) if masked else sp))
        stacked = jnp.concatenate(
            [jnp.concatenate([parts[h][j][b * kc:(b + 1) * kc] for j in range(2)], axis=0)
             for b in range(n_sub) for h in range(HEADS)], axis=1)
        res = _dot(tri, stacked)
        new_runs, new_accs = [], []
        for h in range(HEADS):
            run = runs[h]
            afters = [None] * n_sub
            for b in reversed(range(n_sub)):
                col = (b * HEADS + h) * qb
                afters[b] = res[:kc, col:col + qb] + run
                run = run + res[kc:kc + 1, col:col + qb]
            a = jnp.exp2(zs[h] - sps[h] + jnp.concatenate(afters, axis=0))
            if masked:
                a = jnp.where(mask, a, 0.0)
            new_accs.append(accs[h] + _dot(vt_ref[0, h, :, pl.ds(ks, step)], a.astype(BF16)))
            new_runs.append(run)
        return tuple(new_runs), tuple(new_accs)

    init = (tuple(jnp.zeros((1, qb), F32) for _ in range(HEADS)),
            tuple(jnp.zeros((HEAD_DIM, qb), F32) for _ in range(HEADS)))
    carry = consume(0, scores(0), init, True)
    _, accs = lax.fori_loop(1, n_steps, lambda r, c: consume(r, scores(r), c, False), carry)
    _store_heads_t(o_ref, accs)


def _sb(q, k, vt):
    B, H, S, _ = q.shape
    qb, kc = QB_WIDE, SB_CK
    u = np.triu(np.ones((kc, kc), np.float32), 1)
    u = np.concatenate([u, np.ones((SB_TOT_ROWS, kc), np.float32)], 0)
    tri = jnp.asarray(-np.concatenate([u, u], 1), BF16)
    return pl.pallas_call(
        _sb_kernel,
        out_shape=jax.ShapeDtypeStruct((B, S, D_GROUP), F32),
        grid=(B, S // qb),
        in_specs=[pl.BlockSpec((1, H, qb, 64), lambda b, i: (b, 0, i, 0)),
                  pl.BlockSpec((1, H, S, 64), lambda b, i: (b, 0, 0, 0)),
                  pl.BlockSpec((1, H, HEAD_DIM, S), lambda b, i: (b, 0, 0, 0)),
                  pl.BlockSpec(tri.shape, lambda b, i: (0, 0))],
        out_specs=pl.BlockSpec((1, qb, D_GROUP), lambda b, i: (b, i, 0)),
        compiler_params=_params("parallel", "arbitrary"),
        name="sb_attn",
    )(q, k, vt, tri)


def _ordered_to_f32(c):
    bits = c ^ ((c >> 31) & jnp.int32(0x7FFFFFFF))
    return pltpu.bitcast(bits, F32)


def _dsa_kernel(q_ref, k_ref, vt_ref, qi_ref, ki_ref, misc_ref, o_ref, sc_ref, m_ref, *, seq, n_top):
    q0 = pl.program_id(1) * QB
    ck = DSA_CK
    hk = ck // 2
    nck = _n_chunks(q0, QB, ck)
    w_t = misc_ref[0].T[MISC_IDXW:MISC_IDXW + DSA_IDX_HEADS, :] * (DSA_IDX_HEADS * DSA_IDX_DIM) ** -0.5
    qi_all = qi_ref[0].reshape(DSA_IDX_HEADS * QB, 128)
    q_all = q_ref[0].reshape(HEADS * QB, HEAD_DIM)

    def halves(c):
        ks = c * ck
        return [pl.multiple_of(ks + j * hk, hk) for j in range(2)]

    def score_body(c, _):
        starts = halves(c)
        rs = [_dot_nt(ki_ref[0, pl.ds(ks, hk), :], qi_all) for ks in starts]
        for ks, r in zip(starts, rs):
            sc = jnp.zeros((hk, QB), F32)
            for h in range(DSA_IDX_HEADS):
                sc = sc + w_t[h:h + 1, :] * jnp.maximum(r[:, h * QB:(h + 1) * QB], 0.0)
            sc = jnp.where(sc == 0.0, 0.0, sc)
            sc_ref[pl.ds(ks, hk), :] = jnp.where(_key_pos(ks, hk) <= _query_pos(q0, hk), sc, NEG_INF)
        return 0

    lax.fori_loop(0, nck, score_body, 0)

    n_beyond = (seq - nck * ck).astype(F32)
    n_part = 32

    def count(pred_fn):
        def cb(c, acc):
            ks = pl.multiple_of(c * ck, ck)
            hit = jnp.where(pred_fn(sc_ref[pl.ds(ks, ck), :], ks), 1.0, 0.0)
            return acc + jnp.sum(hit.reshape(ck // n_part, n_part, QB), axis=0)
        acc = lax.fori_loop(0, nck, cb, jnp.zeros((n_part, QB), F32))
        return jnp.sum(acc, axis=0, keepdims=True)

    def count_ge(thr):
        return count(lambda s, ks: s >= thr) + jnp.where(thr <= NEG_INF, n_beyond, 0.0)

    def thr_body(it, carry):
        t_int, cnt_t = carry
        cand = t_int ^ lax.shift_left(jnp.int32(1), 31 - it)
        cnt = count_ge(_ordered_to_f32(cand))
        ok = cnt >= n_top
        return jnp.where(ok, cand, t_int), jnp.where(ok, cnt, cnt_t)

    t_int, cnt_t = lax.fori_loop(0, 32, thr_body, (jnp.full((1, QB), -2 ** 31, jnp.int32),
                                                   jnp.full((1, QB), float(seq), F32)))
    thr = _ordered_to_f32(t_int)
    thr_up = _ordered_to_f32(t_int + 1)
    m_ref[...] = jnp.full(m_ref.shape, seq, jnp.int32)

    @pl.when(jnp.max(cnt_t) > n_top)
    def _():
        need = n_top - count_ge(thr_up)
        tie_beyond = thr <= NEG_INF

        def idx_body(it, m):
            cand = m + lax.shift_left(jnp.int32(1), (seq.bit_length() - 2) - it)
            below = count(lambda s, ks: (s >= thr) & (s < thr_up) & (_key_pos(ks, ck) < cand))
            below = below + jnp.where(tie_beyond, jnp.maximum(cand - nck * ck, 0).astype(F32), 0.0)
            return jnp.where(below < need, cand, m)

        m_idx = lax.fori_loop(0, seq.bit_length() - 1, idx_body, jnp.zeros((1, QB), jnp.int32))
        m_ref[...] = jnp.broadcast_to(m_idx, m_ref.shape)

    m_idx = m_ref[0:1, :]

    tpos = _query_pos(q0, hk)

    def attn_scores(c):
        return tuple(_dot_nt(k_ref[0, pl.ds(ks, hk), :], q_all)
                     for ks in halves(c))

    def attn_consume(c, ss, carry):
        ms, acc = carry
        for ks, s_all in zip(halves(c), ss):
            kp = _key_pos(ks, hk)
            s_idx = sc_ref[pl.ds(ks, hk), :]
            sel = (s_idx >= thr_up) | ((s_idx >= thr) & (kp <= m_idx))
            mask = sel & (kp <= tpos)
            ms, alpha, p = _softmax_step_t(ms, s_all, mask)
            acc = alpha * acc + _dot(_with_ones(vt_ref[0, :, pl.ds(ks, hk)]), p)
        return tuple(ms), acc

    _, acc = _chunk_loop(nck, attn_scores, attn_consume,
                             (tuple(_m_init() for _ in range(HEADS)), _acc_init(HEADS)))
    _store_heads_t(o_ref, [_finish_t(acc, h) for h in range(HEADS)])


def _dsa(q, k, vt, qi, ki, misc):
    B, H, S, _ = q.shape
    n_top = min(DSA_TOPK, S // 4)
    return pl.pallas_call(
        functools.partial(_dsa_kernel, seq=S, n_top=n_top),
        out_shape=jax.ShapeDtypeStruct((B, S, D_GROUP), F32),
        grid=(B, S // QB),
        in_specs=[pl.BlockSpec((1, H, QB, 64), lambda b, i: (b, 0, i, 0)),
                  pl.BlockSpec((1, S, 64), lambda b, i: (b, 0, 0)),
                  pl.BlockSpec((1, HEAD_DIM, S), lambda b, i: (b, 0, 0)),
                  pl.BlockSpec((1, DSA_IDX_HEADS, QB, 128), lambda b, i: (b, 0, i, 0)),
                  pl.BlockSpec((1, S, 128), lambda b, i: (b, 0, 0)),
                  pl.BlockSpec((1, QB, 128), lambda b, i: (b, i, 0))],
        out_specs=pl.BlockSpec((1, QB, D_GROUP), lambda b, i: (b, i, 0)),
        scratch_shapes=[pltpu.VMEM((S, QB), F32), pltpu.VMEM((8, QB), jnp.int32)],
        compiler_params=_params("parallel", "arbitrary"),
        name="dsa_attn",
    )(q, k, vt, qi, ki, misc)


def _gelu_tanh(x):
    return 0.5 * x * (1.0 + jnp.tanh(np.sqrt(2.0 / np.pi) * (x + 0.044715 * (x * x * x))))


def _nsa_cmp_kernel(kg_ref, vg_ref, pos_ref, wk1_ref, wk2_ref, wv1_ref, wv2_ref, kc_ref, vct_ref):
    def compress(g_ref, w1_ref, w2_ref):
        top = _dot((g_ref[0] + pos_ref[0:1, :]).astype(BF16), w1_ref[0])
        bot = _dot((g_ref[0] + pos_ref[1:2, :]).astype(BF16), w1_ref[1])
        pre = top + pltpu.roll(bot, bot.shape[0] - 1, 0)
        return _dot(_gelu_tanh(pre).astype(BF16), w2_ref[...])

    kc_ref[0] = compress(kg_ref, wk1_ref, wk2_ref).astype(BF16)
    vct_ref[0] = compress(vg_ref, wv1_ref, wv2_ref).T[:HEAD_DIM].astype(BF16)


def _nsa_compress(kcmp, vcmp, pos, wk1, wk2, wv1, wv2):
    B, S, _ = kcmp.shape
    ng = S // NSA_CMP_STRIDE
    gw = NSA_CMP_STRIDE * HEAD_DIM
    kg = kcmp.reshape(B, ng, gw)
    vg = vcmp.reshape(B, ng, gw)

    def const(a):
        return pl.BlockSpec(a.shape, lambda b: (0,) * a.ndim)

    wv2p = jnp.concatenate([wv2, jnp.zeros((HEAD_DIM, LANES - HEAD_DIM), wv2.dtype)], axis=1)
    ins = [kg, vg, pos.reshape(2, gw), wk1.reshape(2, gw, HEAD_DIM).astype(BF16), wk2.astype(BF16),
           wv1.reshape(2, gw, HEAD_DIM).astype(BF16), wv2p.astype(BF16)]
    blk = pl.BlockSpec((1, ng, gw), lambda b: (b, 0, 0))
    return pl.pallas_call(
        _nsa_cmp_kernel,
        out_shape=[jax.ShapeDtypeStruct((B, ng, HEAD_DIM), BF16), jax.ShapeDtypeStruct((B, HEAD_DIM, ng), BF16)],
        grid=(B,),
        in_specs=[blk, blk] + [const(a) for a in ins[2:]],
        out_specs=[pl.BlockSpec((1, ng, HEAD_DIM), lambda b: (b, 0, 0)),
                   pl.BlockSpec((1, HEAD_DIM, ng), lambda b: (b, 0, 0))],
        compiler_params=_params("parallel"),
        name="nsa_compress",
    )(*ins)


def _nsa_kernel(q_ref, kc_ref, vct_ref, ks_ref, vst_ref, kw_ref, vwt_ref, misc_ref, ovl_ref, o_ref,
                *, seq):
    i = pl.program_id(1)
    q0 = i * QB
    n_slc = seq // NSA_SEL_LEN
    n_cmp = kc_ref.shape[1]
    q_all = q_ref[0].reshape(HEADS * QB, HEAD_DIM)

    last_tok = lax.broadcasted_iota(jnp.int32, (n_cmp, QB), 0) * NSA_CMP_STRIDE + (NSA_CMP_LEN - 1)
    cmask = last_tok <= _query_pos(q0, n_cmp)
    s_cmp = _dot_nt(kc_ref[0], q_all)
    ws = jnp.maximum(q0 - NSA_WINDOW, 0)
    win_parts = [(pl.multiple_of(ws + off, QB), n) for off, n in NSA_WIN_PARTS]
    s_wins = [_dot_nt(kw_ref[0, pl.ds(start, n), :], q_all) for start, n in win_parts]
    ps = []
    p_sum = jnp.zeros((n_cmp, QB), F32)
    for h in range(HEADS):
        s = jnp.where(cmask, s_cmp[:, h * QB:(h + 1) * QB], NEG_INF)
        e = jnp.where(cmask, jnp.exp2(s - jnp.max(s, axis=0, keepdims=True)), 0.0)
        p = e / jnp.maximum(jnp.sum(e, axis=0, keepdims=True), 1e-30)
        p_sum = p_sum + p
        ps.append(p.astype(BF16))
    o_cmp = _dot(vct_ref[0], jnp.concatenate(ps, axis=1))

    hi, lo = _split_bf16(p_sum)
    imp = _dot(ovl_ref[...], jnp.concatenate([hi, lo], axis=0))
    jb = lax.broadcasted_iota(jnp.int32, (n_slc, 1), 0)
    cur = (q0 + lax.broadcasted_iota(jnp.int32, (1, QB), 1)) >> SEL_SHIFT
    forced = (jb < NSA_N_INIT) | ((jb <= cur) & (jb > cur - NSA_N_LOCAL))
    imp = jnp.where(forced, FORCE_SCORE, jnp.where(jb <= cur, imp, NEG_INF))
    rank = jnp.zeros((n_slc, QB), F32)
    for r in range(n_slc):
        row = imp[r:r + 1, :]
        beats = (row > imp) | ((row == imp) & (r < jb))
        rank = rank + jnp.where(beats, 1.0, 0.0)
    sel_t = jnp.where(rank < min(NSA_N_SEL, n_slc), 1.0, 0.0)
    sel_t = jnp.concatenate([sel_t, jnp.zeros((LANES - n_slc, QB), F32)], axis=0).astype(BF16)

    ms_win, acc_win = [_m_init() for _ in range(HEADS)], _acc_init(HEADS)
    for (start, n), s_win in zip(win_parts, s_wins):
        dist = _query_pos(q0, n) - _key_pos(start, n)
        wmask = (dist >= 0) & (dist < NSA_WINDOW)
        ms_win, alpha, p_win = _softmax_step_t(ms_win, s_win, wmask)
        acc_win = alpha * acc_win + _dot(_with_ones(vwt_ref[0, :, pl.ds(start, n)]), p_win)

    ck = DSA_CK
    hk = ck // 2
    tpos = _query_pos(q0, hk)

    n_slc_chunks = _n_chunks(q0, QB, ck)

    def slc_starts(c):
        return [pl.multiple_of(c * ck + j * hk, hk) for j in range(2)]

    def slc_scores(c):
        out = []
        for ks in slc_starts(c):
            blk = (ks + lax.broadcasted_iota(jnp.int32, (hk, LANES), 0)) >> SEL_SHIFT
            expand = jnp.where(blk == lax.broadcasted_iota(jnp.int32, (hk, LANES), 1), 1.0, 0.0)
            out.append(_dot(expand.astype(BF16), sel_t))
        return tuple(out) + tuple(_dot_nt(ks_ref[0, pl.ds(ks, hk), :], q_all) for ks in slc_starts(c))

    def slc_consume(c, ss, carry):
        ms, acc = carry
        for ks, chosen, s_all in zip(slc_starts(c), ss[:2], ss[2:]):
            mask = (chosen > 0.5) & (_key_pos(ks, hk) <= tpos)
            ms, alpha, p = _softmax_step_t(ms, s_all, mask)
            acc = alpha * acc + _dot(_with_ones(vst_ref[0, :, pl.ds(ks, hk)]), p)
        return tuple(ms), acc

    _, acc_slc = _chunk_loop(n_slc_chunks, slc_scores, slc_consume,
                                 (tuple(_m_init() for _ in range(HEADS)), _acc_init(HEADS)))

    gate_rows = 16
    gate = jax.nn.sigmoid(misc_ref[0].T[MISC_GATE:MISC_GATE + gate_rows, :])
    outs = []
    for h in range(HEADS):
        outs.append(gate[h:h + 1] * o_cmp[:, h * QB:(h + 1) * QB]
                    + gate[HEADS + h:HEADS + h + 1] * _finish_t(acc_slc, h)
                    + gate[2 * HEADS + h:2 * HEADS + h + 1] * _finish_t(acc_win, h))
    _store_heads_t(o_ref, outs)


def _nsa_overlap(seq):
    ng = seq // NSA_CMP_STRIDE
    n_slc = seq // NSA_SEL_LEN
    n = np.arange(ng)
    first, last = n * NSA_CMP_STRIDE, n * NSA_CMP_STRIDE + NSA_CMP_LEN - 1
    start = np.arange(n_slc) * NSA_SEL_LEN
    ovl = ((first[None, :] <= start[:, None] + NSA_SEL_LEN - 1) & (last[None, :] >= start[:, None]))
    ovl = ovl & (last[None, :] < seq)
    ovl = ovl.astype(np.float32)
    return jnp.asarray(np.concatenate([ovl, ovl], 1), BF16)


def _nsa(q, kc, vct, k_slc, vt_slc, k_win, vt_win, misc):
    B, H, S, _ = q.shape
    ng = kc.shape[1]
    ovl = _nsa_overlap(S)
    keys = pl.BlockSpec((1, S, HEAD_DIM), lambda b, i: (b, 0, 0))
    vals_t = pl.BlockSpec((1, HEAD_DIM, S), lambda b, i: (b, 0, 0))
    return pl.pallas_call(
        functools.partial(_nsa_kernel, seq=S),
        out_shape=jax.ShapeDtypeStruct((B, S, D_GROUP), F32),
        grid=(B, S // QB),
        in_specs=[pl.BlockSpec((1, H, QB, 64), lambda b, i: (b, 0, i, 0)),
                  pl.BlockSpec((1, ng, HEAD_DIM), lambda b, i: (b, 0, 0)),
                  pl.BlockSpec((1, HEAD_DIM, ng), lambda b, i: (b, 0, 0)),
                  keys, vals_t, keys, vals_t,
                  pl.BlockSpec((1, QB, 128), lambda b, i: (b, i, 0)),
                  pl.BlockSpec(ovl.shape, lambda b, i: (0, 0))],
        out_specs=pl.BlockSpec((1, QB, D_GROUP), lambda b, i: (b, i, 0)),
        compiler_params=_params("parallel", "arbitrary"),
        name="nsa_attn",
    )(q, kc, vct, k_slc, vt_slc, k_win, vt_win, misc, ovl)


def _outproj_kernel(x_ref, ya_ref, yb_ref, yc_ref, yd_ref, gg_ref, wo_ref, g_ref, b_ref, o_ref):
    acc = None
    for gi, y_ref in enumerate((ya_ref, yb_ref, yc_ref, yd_ref)):
        lo, hi = gi * D_GROUP, (gi + 1) * D_GROUP
        yn = _rms_norm(y_ref[...], gg_ref[:, lo:hi]).astype(BF16)
        part = _dot(yn, wo_ref[lo:hi, :])
        acc = part if acc is None else acc + part
    o_ref[...] = _layer_norm(DN_ALPHA * x_ref[...] + acc, g_ref[...], b_ref[...])


def _outproj(x2, ys, gg, wo, g, b, *, tm):
    n, d = x2.shape
    row = lambda w: pl.BlockSpec((tm, w), lambda i: (i, 0))
    const = lambda a: pl.BlockSpec(a.shape, lambda i: (0, 0))
    return pl.pallas_call(
        _outproj_kernel,
        out_shape=jax.ShapeDtypeStruct((n, d), F32),
        grid=(n // tm,),
        in_specs=[row(d)] + [row(D_GROUP)] * 4 + [const(gg), const(wo), const(g), const(b)],
        out_specs=row(d),
        compiler_params=_params("parallel"),
        name="out_proj",
    )(x2, *ys, gg, wo, g, b)


def _mixer(x, w_in, q_norm, w_uq, kv_norm, w_ukv, cmp_pos, wk1, wk2, wv1, wv2, tabs, place, maps):
    B, S, D = x.shape
    wm = _gather_cols(w_in, maps).astype(BF16)
    wt = jnp.concatenate([wm[:, slice(*_MAIN_OFF['slab_v'])], wm[:, slice(*_MAIN_OFF['sb_v'])]], axis=1).T
    (mq, mk, mvt, nq, dq, sbq, sbk, sbvt, kcmp, kslc, kwin, dk, vcmp, vslct, vwint, dvt,
     qi, ki, misc) = _inproj(x, wm, wt, tabs, q_norm[None, :], kv_norm[None, :],
                             _mla_weights(w_uq, w_ukv), place, tm=min(512, S))
    y_a = _mla(mq, mk, mvt)
    kc, vct = _nsa_compress(kcmp, vcmp, cmp_pos, wk1, wk2, wv1, wv2)
    y_b = _nsa(nq, kc, vct, kslc, vslct, kwin, vwint, misc)
    y_c = _dsa(dq, dk, dvt, qi, ki, misc)
    y_d = _sb(sbq, sbk, sbvt)
    return y_a, y_b, y_c, y_d


def kernel(x, ln1_g, ln1_b, ffn1_w1, ffn1_w3, ffn1_w2, w_in, mla_q_norm, mla_w_uq, mla_kv_norm,
           mla_w_ukv, nsa_cmp_pos, nsa_cmp_wk1, nsa_cmp_wk2, nsa_cmp_wv1, nsa_cmp_wv2, group_norm_g,
           w_out, ln2_g, ln2_b, ffn2_w1, ffn2_w3, ffn2_w2, ln3_g, ln3_b):
    B, S, D = x.shape
    n = B * S
    tabs = _rope_tables(S)
    place = _placement_constants()
    maps = _column_maps()
    tm_ffn = min(512, n)
    x2 = x.reshape(n, D)
    for l in range(DEPTH):
        x2 = _ffn_ln(x2, ffn1_w1[l].astype(BF16), ffn1_w3[l].astype(BF16), ffn1_w2[l].astype(BF16),
                     ln1_g[l][None, :], ln1_b[l][None, :], tm=tm_ffn, tf=256)
        ys = _mixer(x2.reshape(B, S, D), w_in[l], mla_q_norm[l], mla_w_uq[l], mla_kv_norm[l],
                    mla_w_ukv[l], nsa_cmp_pos[l], nsa_cmp_wk1[l], nsa_cmp_wk2[l], nsa_cmp_wv1[l],
                    nsa_cmp_wv2[l], tabs, place, maps)
        x2 = _outproj(x2, [y.reshape(n, D_GROUP) for y in ys], group_norm_g[l][None, :],
                      w_out[l].astype(BF16), ln2_g[l][None, :], ln2_b[l][None, :], tm=min(512, n))
        x2 = _ffn_ln(x2, ffn2_w1[l].astype(BF16), ffn2_w3[l].astype(BF16), ffn2_w2[l].astype(BF16),
                     ln3_g[l][None, :], ln3_b[l][None, :], tm=tm_ffn, tf=256)
    return x2.reshape(B, S, D)
```

```python
import functools

import numpy as np
import jax
import jax.numpy as jnp
from jax import lax
from jax.experimental import pallas as pl
from jax.experimental.pallas import tpu as pltpu

F32 = jnp.float32
BF16 = jnp.bfloat16

D_MODEL = 1024
DEPTH = 2
HEADS = 4
HEAD_DIM = 64
D_GROUP = HEADS * HEAD_DIM
N_GROUPS = 4
D_FF = 2816
ROPE_THETA = 10000.0
MLA_Q_RANK = 256
MLA_KV_RANK = 128
MLA_D_NOPE = 64
MLA_D_ROPE = 32
MLA_D_V = 64
NSA_CMP_LEN = 32
NSA_CMP_STRIDE = 16
NSA_SEL_LEN = 64
NSA_N_SEL = 8
NSA_N_INIT = 1
NSA_N_LOCAL = 2
NSA_WINDOW = 512
SEL_SHIFT = NSA_SEL_LEN.bit_length() - 1
DSA_TOPK = 256
DSA_IDX_HEADS = 8
DSA_IDX_DIM = 32
DN_ALPHA = (2.0 * DEPTH) ** 0.25
LN_EPS = 1e-5
RMS_EPS = 1e-6
NEG_INF = -1e30
FORCE_SCORE = 1e4
LOG2_E = 1.4426950408889634

LANES = 128
QB = 128
CK = 256
DSA_CK = 512
NSA_WIN_PARTS = ((0, NSA_WINDOW + QB),)
VMEM_LIMIT = 48 * 1024 * 1024

MISC_KROPE = 0
MISC_IDXK = 32
MISC_IDXW = 64
MISC_GATE = 72

NT_DIMS = (((1,), (1,)), ((), ()))


def _dot(a, b):
    return jnp.dot(a, b, preferred_element_type=F32)


def _dot_nt(a, b):
    return lax.dot_general(a, b, NT_DIMS, preferred_element_type=F32)


def _split_bf16(x):
    hi = x.astype(BF16)
    lo = (x - hi.astype(F32)).astype(BF16)
    return hi, lo


def _layer_norm(y, g, b):
    mu = jnp.mean(y, -1, keepdims=True)
    d = y - mu
    var = jnp.mean(d * d, -1, keepdims=True)
    return d * lax.rsqrt(var + LN_EPS) * g + b


def _rms_norm(y, g):
    return y * lax.rsqrt(jnp.mean(y * y, -1, keepdims=True) + RMS_EPS) * g


def _params(*sem):
    return pltpu.CompilerParams(dimension_semantics=sem, vmem_limit_bytes=VMEM_LIMIT)


def _ffn_kernel(x_ref, w1_ref, w3_ref, w2_ref, g_ref, b_ref, o_ref, a_ref, *, tf):
    xb = x_ref[...].astype(BF16)
    n_slabs = w1_ref.shape[1] // tf

    def up(j):
        return _dot(xb, w1_ref[:, j * tf:(j + 1) * tf]), _dot(xb, w3_ref[:, j * tf:(j + 1) * tf])

    hu = up(0)
    for j in range(n_slabs):
        nxt = up(j + 1) if j + 1 < n_slabs else None
        h, u = hu
        a_ref[:, j * tf:(j + 1) * tf] = (h * jax.nn.sigmoid(h) * u).astype(BF16)
        hu = nxt
    y = DN_ALPHA * x_ref[...] + 0.5 * _dot(a_ref[...], w2_ref[...])
    o_ref[...] = _layer_norm(y, g_ref[...], b_ref[...])


def _ffn_ln(x2, w1, w3, w2, g, b, *, tm, tf):
    n, d = x2.shape
    dff = w1.shape[1]
    const = lambda a: pl.BlockSpec(a.shape, lambda i: (0, 0))
    return pl.pallas_call(
        functools.partial(_ffn_kernel, tf=tf),
        out_shape=jax.ShapeDtypeStruct((n, d), F32),
        grid=(n // tm,),
        in_specs=[pl.BlockSpec((tm, d), lambda i: (i, 0)), const(w1), const(w3), const(w2), const(g), const(b)],
        out_specs=pl.BlockSpec((tm, d), lambda i: (i, 0)),
        scratch_shapes=[pltpu.VMEM((tm, dff), BF16)],
        compiler_params=_params("parallel"),
        name="ffn_ln",
    )(x2, w1, w3, w2, g, b)


_MAIN_ORDER = (
    ('mla_cq', 256), ('nsa_q', 256), ('dsa_q', 256), ('idx_q', 256),
    ('sb_q', 256), ('sb_k', 256), ('sb_v', 256),
    ('slab_k', 256),
    ('slab_v', 256),
    ('mla_ckv', 128),
    ('misc', 128),
)
_MAIN_OFF = {}
_o = 0
for _n, _w in _MAIN_ORDER:
    _MAIN_OFF[_n] = (_o, _o + _w)
    _o += _w
N_MAIN = _o

_IN_SPLITS = (
    ('mla_cq', MLA_Q_RANK), ('mla_ckv', MLA_KV_RANK), ('mla_krope', MLA_D_ROPE),
    ('nsa_q', D_GROUP), ('nsa_k_cmp', HEAD_DIM), ('nsa_v_cmp', HEAD_DIM),
    ('nsa_k_slc', HEAD_DIM), ('nsa_v_slc', HEAD_DIM), ('nsa_k_win', HEAD_DIM),
    ('nsa_v_win', HEAD_DIM), ('nsa_gate', 3 * HEADS),
    ('dsa_q', D_GROUP), ('dsa_k', HEAD_DIM), ('dsa_v', HEAD_DIM),
    ('idx_q', DSA_IDX_HEADS * DSA_IDX_DIM), ('idx_k', DSA_IDX_DIM), ('idx_w', DSA_IDX_HEADS),
    ('sb_q', D_GROUP), ('sb_k', D_GROUP), ('sb_v', D_GROUP),
)
_SRC = {}
_o = 0
for _n, _w in _IN_SPLITS:
    _SRC[_n] = np.arange(_o, _o + _w)
    _o += _w
D_IN = _o


def _swap_halves(width, dim):
    idx = np.arange(width)
    return (idx // dim) * dim + (idx % dim + dim // 2) % dim


def _column_maps():
    pieces = {
        'mla_cq': [_SRC['mla_cq']], 'nsa_q': [_SRC['nsa_q']], 'dsa_q': [_SRC['dsa_q']],
        'idx_q': [_SRC['idx_q']], 'sb_q': [_SRC['sb_q']], 'sb_k': [_SRC['sb_k']], 'sb_v': [_SRC['sb_v']],
        'slab_k': [_SRC['nsa_k_cmp'], _SRC['nsa_k_slc'], _SRC['nsa_k_win'], _SRC['dsa_k']],
        'slab_v': [_SRC['nsa_v_cmp'], _SRC['nsa_v_slc'], _SRC['nsa_v_win'], _SRC['dsa_v']],
        'mla_ckv': [_SRC['mla_ckv']],
        'misc': [_SRC['mla_krope'], _SRC['idx_k'], _SRC['idx_w'], _SRC['nsa_gate'],
                 -np.ones(LANES - MISC_GATE - 3 * HEADS, np.int64)],
    }
    return np.concatenate([np.concatenate(pieces[n]) for n, _ in _MAIN_ORDER])


def _gather_cols(w, cols):
    out = jnp.take(w, np.maximum(cols, 0), axis=1)
    return jnp.where(jnp.asarray(cols >= 0)[None, :], out, 0.0)


def _rope_tables(seq):
    def base(dim):
        inv = ROPE_THETA ** (-jnp.arange(0, dim, 2, dtype=F32) / dim)
        ang = jnp.arange(seq, dtype=F32)[:, None] * inv[None, :]
        c, s = jnp.cos(ang), jnp.sin(ang)
        return jnp.concatenate([c, c], -1), jnp.concatenate([-s, s], -1)

    c64, s64 = base(HEAD_DIM)
    c32, s32 = base(DSA_IDX_DIM)
    t64 = (jnp.tile(c64, (1, 4)), jnp.tile(s64, (1, 4)))
    t32 = (jnp.tile(c32, (1, 8)), jnp.tile(s32, (1, 8)))
    ones = jnp.ones((seq, LANES - 64), F32)
    td = (jnp.concatenate([c32, c32, ones], -1), jnp.concatenate([s32, s32, 0.0 * ones], -1))
    scale = (MLA_D_NOPE + MLA_D_ROPE) ** -0.5 * LOG2_E
    cq = jnp.concatenate([jnp.ones((seq, 64), F32), c32, jnp.zeros((seq, 32), F32)], -1) * scale
    sq = jnp.concatenate([jnp.zeros((seq, 64), F32), s32, jnp.zeros((seq, 32), F32)], -1) * scale
    tq = (jnp.tile(cq, (1, 4)), jnp.tile(sq, (1, 4)))
    return t64, t32, td, tq


def _placement_constants():
    p_hi = np.zeros((256, 1024), np.float32)
    p_lo = np.zeros((256, 1024), np.float32)
    for h in range(DSA_IDX_HEADS):
        for d in range(DSA_IDX_DIM):
            p_hi[h * 32 + d, h * 128 + d] = 1
            p_hi[h * 32 + d, h * 128 + 64 + d] = 1
            p_lo[h * 32 + d, h * 128 + 32 + d] = 1
    pk_hi = np.zeros((128, 128), np.float32)
    pk_lo = np.zeros((128, 128), np.float32)
    for d in range(DSA_IDX_DIM):
        pk_hi[MISC_IDXK + d, d] = 1
        pk_hi[MISC_IDXK + d, 32 + d] = 1
        pk_lo[MISC_IDXK + d, 64 + d] = 1
    pk_pe = np.zeros((128, 512), np.float32)
    for h in range(HEADS):
        for d in range(MLA_D_ROPE):
            pk_pe[MISC_KROPE + d, h * 128 + 64 + d] = 1
    return tuple(jnp.asarray(a, BF16) for a in (p_hi, p_lo, pk_hi, pk_lo, pk_pe))


def _mla_weights(w_uq, w_ukv):
    dq = MLA_D_NOPE + MLA_D_ROPE
    cols_q = -np.ones(512, np.int64)
    cols_qr = -np.ones(512, np.int64)
    cols_k = -np.ones(512, np.int64)
    cols_v = np.zeros(256, np.int64)
    for h in range(HEADS):
        cols_q[h * 128:h * 128 + dq] = h * dq + np.arange(dq)
        pe = h * dq + MLA_D_NOPE + np.arange(MLA_D_ROPE)
        cols_qr[h * 128 + 64:h * 128 + 96] = pe[_swap_halves(32, 32)]
        cols_k[h * 128:h * 128 + 64] = h * 128 + np.arange(64)
        cols_v[h * 64:(h + 1) * 64] = h * 128 + 64 + np.arange(64)
    return (_gather_cols(w_uq, cols_q).astype(BF16), _gather_cols(w_uq, cols_qr).astype(BF16),
            _gather_cols(w_ukv, cols_k).astype(BF16), jnp.take(w_ukv, cols_v, axis=1).T.astype(BF16))


def _rotate_half(z, dim):
    n = z.shape[1]
    half = dim // 2
    lane = lax.broadcasted_iota(jnp.int32, z.shape, 1)
    in_first_half = (lane & (dim - 1)) < half
    return jnp.where(in_first_half, pltpu.roll(z, n - half, 1), pltpu.roll(z, half, 1))


def _inproj_kernel(x_ref, wm_ref, wt_ref, c64_ref, s64_ref, c32_ref, s32_ref, cd_ref, sd_ref,
                   cq_ref, sq_ref, qg_ref, kvg_ref, wuq_ref, wuqr_ref, wuk_ref, wuvt_ref,
                   phi_ref, plo_ref, pkhi_ref, pklo_ref, pkpe_ref,
                   mq_ref, mk_ref, mvt_ref, nq_ref, dq_ref, sbq_ref, sbk_ref, sbvt_ref,
                   kcmp_ref, kslc_ref, kwin_ref, dk_ref, vcmp_ref, vslct_ref, vwint_ref, dvt_ref,
                   qi_ref, ki_ref, misc_ref):
    xb = x_ref[0].astype(BF16)

    def main(name):
        a, b = _MAIN_OFF[name]
        return _dot(xb, wm_ref[:, a:b])

    def roped(name, c_ref, s_ref, dim):
        z = main(name)
        return z * c_ref[...] + _rotate_half(z, dim) * s_ref[...]

    def store_heads(ref, val, width):
        for h in range(HEADS):
            ref[0, h] = val[:, h * width:(h + 1) * width].astype(ref.dtype)

    cqn = _rms_norm(main('mla_cq'), qg_ref[...]).astype(BF16)
    q = _dot(cqn, wuq_ref[...]) * cq_ref[...] + _dot(cqn, wuqr_ref[...]) * sq_ref[...]
    store_heads(mq_ref, q, 128)
    ckv_misc = _dot(xb, wm_ref[:, _MAIN_OFF['mla_ckv'][0]:_MAIN_OFF['misc'][1]])
    z_misc = ckv_misc[:, LANES:]
    misc = z_misc * cd_ref[...] + _rotate_half(z_misc, DSA_IDX_DIM) * sd_ref[...]
    misc_ref[0] = misc
    m_hi, m_lo = _split_bf16(misc)
    ckvn = _rms_norm(ckv_misc[:, :LANES], kvg_ref[...]).astype(BF16)
    k = _dot(ckvn, wuk_ref[...]) + _dot(m_hi, pkpe_ref[...])
    store_heads(mk_ref, k, 128)
    mvt = _dot_nt(wuvt_ref[...], ckvn)
    for h in range(HEADS):
        mvt_ref[0, h] = mvt[h * 64:(h + 1) * 64, :].astype(BF16)
    qscale = HEAD_DIM ** -0.5 * LOG2_E
    store_heads(nq_ref, roped('nsa_q', c64_ref, s64_ref, HEAD_DIM) * qscale, 64)
    store_heads(dq_ref, roped('dsa_q', c64_ref, s64_ref, HEAD_DIM) * qscale, 64)
    store_heads(sbq_ref, main('sb_q') * qscale, 64)
    store_heads(sbk_ref, main('sb_k'), 64)
    sk = roped('slab_k', c64_ref, s64_ref, HEAD_DIM)
    for j, ref in enumerate((kcmp_ref, kslc_ref, kwin_ref, dk_ref)):
        ref[0] = sk[:, j * 64:(j + 1) * 64].astype(ref.dtype)
    vt = _dot_nt(wt_ref[...], xb)
    vcmp_ref[0] = vt[0:LANES, :].T[:, 0:HEAD_DIM]
    for j, ref in ((1, vslct_ref), (2, vwint_ref), (3, dvt_ref)):
        ref[0] = vt[j * 64:(j + 1) * 64, :].astype(BF16)
    for h in range(HEADS):
        sbvt_ref[0, h] = vt[256 + h * 64:256 + (h + 1) * 64, :].astype(BF16)
    qi_hi, qi_lo = _split_bf16(roped('idx_q', c32_ref, s32_ref, DSA_IDX_DIM))
    qi = _dot(qi_hi, phi_ref[...]) + _dot(qi_lo, plo_ref[...])
    for h in range(DSA_IDX_HEADS):
        qi_ref[0, h] = qi[:, h * 128:(h + 1) * 128].astype(BF16)
    ki_ref[0] = (_dot(m_hi, pkhi_ref[...]) + _dot(m_lo, pklo_ref[...])).astype(BF16)


def _inproj(x, wm, wt, tabs, qg, kvg, mla_w, place, *, tm):
    B, S, D = x.shape
    (c64, s64), (c32, s32), (cd, sd), (cq, sq) = tabs
    wuq, wuqr, wuk, wuvt = mla_w

    def const(a):
        return pl.BlockSpec(a.shape, lambda b, s: (0,) * a.ndim)

    def tab(a):
        return pl.BlockSpec((tm, a.shape[1]), lambda b, s: (s, 0))

    def heads_out(w, dt=BF16):
        return (jax.ShapeDtypeStruct((B, HEADS, S, w), dt),
                pl.BlockSpec((1, HEADS, tm, w), lambda b, s: (b, 0, s, 0)))

    def flat_out(w, dt):
        return (jax.ShapeDtypeStruct((B, S, w), dt), pl.BlockSpec((1, tm, w), lambda b, s: (b, s, 0)))

    heads_t = (jax.ShapeDtypeStruct((B, HEADS, HEAD_DIM, S), BF16),
               pl.BlockSpec((1, HEADS, HEAD_DIM, tm), lambda b, s: (b, 0, 0, s)))
    flat_t = (jax.ShapeDtypeStruct((B, HEAD_DIM, S), BF16),
              pl.BlockSpec((1, HEAD_DIM, tm), lambda b, s: (b, 0, s)))
    qi_out = (jax.ShapeDtypeStruct((B, DSA_IDX_HEADS, S, 128), BF16),
              pl.BlockSpec((1, DSA_IDX_HEADS, tm, 128), lambda b, s: (b, 0, s, 0)))
    outs = [heads_out(128), heads_out(128), heads_t,
            heads_out(64), heads_out(64),
            heads_out(64), heads_out(64), heads_t,
            flat_out(64, F32), flat_out(64, BF16), flat_out(64, BF16), flat_out(64, BF16),
            flat_out(64, F32), flat_t, flat_t, flat_t,
            qi_out, flat_out(128, BF16), flat_out(128, F32)]
    ins = [x, wm, wt, c64, s64, c32, s32, cd, sd, cq, sq, qg, kvg, wuq, wuqr, wuk, wuvt, *place]
    in_specs = [pl.BlockSpec((1, tm, D), lambda b, s: (b, s, 0)), const(wm), const(wt),
                tab(c64), tab(s64), tab(c32), tab(s32), tab(cd), tab(sd), tab(cq), tab(sq),
                const(qg), const(kvg), const(wuq), const(wuqr), const(wuk), const(wuvt),
                *[const(p) for p in place]]
    return pl.pallas_call(
        _inproj_kernel,
        out_shape=[o[0] for o in outs],
        grid=(B, S // tm),
        in_specs=in_specs,
        out_specs=[o[1] for o in outs],
        compiler_params=_params("parallel", "arbitrary"),
        name="in_proj",
    )(*ins)


V_ROWS = HEAD_DIM


def _acc_init(n_heads, qb=QB):
    return jnp.zeros((2 * V_ROWS, n_heads * qb), F32)


def _m_init(qb=QB):
    return jnp.full((1, qb), NEG_INF, F32)


def _with_ones(vt):
    return jnp.concatenate([vt, jnp.ones((V_ROWS, vt.shape[1]), vt.dtype)], axis=0)


def _softmax_step_t(ms, s_all, mask):
    new_ms, alphas, es = [], [], []
    qb = mask.shape[1]
    for h, m in enumerate(ms):
        s = jnp.where(mask, s_all[:, h * qb:(h + 1) * qb], NEG_INF)
        m_new = jnp.maximum(m, jnp.max(s, axis=0, keepdims=True))
        alphas.append(jnp.exp2(m - m_new))
        es.append(jnp.where(mask, jnp.exp2(s - m_new), 0.0).astype(BF16))
        new_ms.append(m_new)
    return new_ms, jnp.concatenate(alphas, axis=1), jnp.concatenate(es, axis=1)


def _finish_t(acc, h, qb=QB):
    blk = acc[:, h * qb:(h + 1) * qb]
    return blk[:V_ROWS] / jnp.maximum(blk[V_ROWS:V_ROWS + 1], 1e-30)


def _store_heads_t(o_ref, outs_t):
    for p in range(HEADS // 2):
        pair = jnp.concatenate([outs_t[2 * p], outs_t[2 * p + 1]], axis=0)
        o_ref[0, :, p * 128:(p + 1) * 128] = pair.T


def _chunk_loop(n, scores, consume, init):
    return lax.fori_loop(0, n, lambda c, state: consume(c, scores(c), state), init)


def _n_chunks(q0, qb=QB, ck=CK):
    return (q0 + qb + ck - 1) // ck


def _key_pos(ks, n, qb=QB):
    return ks + lax.broadcasted_iota(jnp.int32, (n, qb), 0)


def _query_pos(q0, n, qb=QB):
    return q0 + lax.broadcasted_iota(jnp.int32, (n, qb), 1)


QB_WIDE = 256


def _mla_kernel(q_ref, k_ref, vt_ref, o_ref):
    qb = QB_WIDE
    q0 = pl.program_id(1) * qb
    tpos = _query_pos(q0, CK, qb)

    n = _n_chunks(q0, qb)

    def scores(c):
        ks = pl.multiple_of(c * CK, CK)
        return tuple(_dot_nt(k_ref[0, h, pl.ds(ks, CK), :], q_ref[0, h]) for h in range(HEADS))

    def consume(c, ss, carry):
        ms, accs = carry
        ks = pl.multiple_of(c * CK, CK)
        mask = _key_pos(ks, CK, qb) <= tpos
        steps = [_softmax_step_t([ms[h]], ss[h], mask) for h in range(HEADS)]
        new_accs = [steps[h][1] * accs[h] + _dot(_with_ones(vt_ref[0, h, :, pl.ds(ks, CK)]), steps[h][2])
                    for h in range(HEADS)]
        return tuple(st[0][0] for st in steps), tuple(new_accs)

    init = (tuple(_m_init(qb) for _ in range(HEADS)), tuple(_acc_init(1, qb) for _ in range(HEADS)))
    _, accs = _chunk_loop(n, scores, consume, init)
    _store_heads_t(o_ref, [_finish_t(accs[h], 0, qb) for h in range(HEADS)])


def _mla(q, k, vt):
    B, H, S, _ = q.shape
    qb = QB_WIDE
    return pl.pallas_call(
        _mla_kernel,
        out_shape=jax.ShapeDtypeStruct((B, S, D_GROUP), F32),
        grid=(B, S // qb),
        in_specs=[pl.BlockSpec((1, H, qb, 128), lambda b, i: (b, 0, i, 0)),
                  pl.BlockSpec((1, H, S, 128), lambda b, i: (b, 0, 0, 0)),
                  pl.BlockSpec((1, H, HEAD_DIM, S), lambda b, i: (b, 0, 0, 0))],
        out_specs=pl.BlockSpec((1, qb, D_GROUP), lambda b, i: (b, i, 0)),
        compiler_params=_params("parallel", "arbitrary"),
        name="mla_attn",
    )(q, k, vt)


SB_TOT_ROWS = 16


SB_CK = 128


def _sb_kernel(q_ref, k_ref, vt_ref, tri_ref, o_ref):
    qb, kc = QB_WIDE, SB_CK
    n_sub = 2
    step = n_sub * kc
    q0 = pl.program_id(1) * qb
    n_steps = (q0 + qb) // step
    tri = tri_ref[...]

    def key_start(r):
        return pl.multiple_of(q0 + qb - (r + 1) * step, step)

    def scores(r):
        ks = key_start(r)
        return tuple(_dot_nt(k_ref[0, h, pl.ds(ks, step), :], q_ref[0, h]) for h in range(HEADS))

    def consume(r, zs, carry, masked):
        runs, accs = carry
        ks = key_start(r)
        if masked:
            mask = _key_pos(ks, step, qb) < _query_pos(q0, step, qb)
        sps, parts = [], []
        for h in range(HEADS):
            sp = jnp.maximum(zs[h], 0.0) + jnp.log2(1.0 + jnp.exp2(-jnp.abs(zs[h])))
            sps.append(sp)
            parts.append(_split_bf16(jnp.where(mask, sp, 0.0) if masked else sp))
        stacked = jnp.concatenate(
            [jnp.concatenate([parts[h][j][b * kc:(b + 1) * kc] for j in range(2)], axis=0)
             for b in range(n_sub) for h in range(HEADS)], axis=1)
        res = _dot(tri, stacked)
        new_runs, new_accs = [], []
        for h in range(HEADS):
            run = runs[h]
            afters = [None] * n_sub
            for b in reversed(range(n_sub)):
                col = (b * HEADS + h) * qb
                afters[b] = res[:kc, col:col + qb] + run
                run = run + res[kc:kc + 1, col:col + qb]
            a = jnp.exp2(zs[h] - sps[h] + jnp.concatenate(afters, axis=0))
            if masked:
                a = jnp.where(mask, a, 0.0)
            new_accs.append(accs[h] + _dot(vt_ref[0, h, :, pl.ds(ks, step)], a.astype(BF16)))
            new_runs.append(run)
        return tuple(new_runs), tuple(new_accs)

    init = (tuple(jnp.zeros((1, qb), F32) for _ in range(HEADS)),
            tuple(jnp.zeros((HEAD_DIM, qb), F32) for _ in range(HEADS)))
    carry = consume(0, scores(0), init, True)
    _, accs = lax.fori_loop(1, n_steps, lambda r, c: consume(r, scores(r), c, False), carry)
    _store_heads_t(o_ref, accs)


def _sb(q, k, vt):
    B, H, S, _ = q.shape
    qb, kc = QB_WIDE, SB_CK
    u = np.triu(np.ones((kc, kc), np.float32), 1)
    u = np.concatenate([u, np.ones((SB_TOT_ROWS, kc), np.float32)], 0)
    tri = jnp.asarray(-np.concatenate([u, u], 1), BF16)
    return pl.pallas_call(
        _sb_kernel,
        out_shape=jax.ShapeDtypeStruct((B, S, D_GROUP), F32),
        grid=(B, S // qb),
        in_specs=[pl.BlockSpec((1, H, qb, 64), lambda b, i: (b, 0, i, 0)),
                  pl.BlockSpec((1, H, S, 64), lambda b, i: (b, 0, 0, 0)),
                  pl.BlockSpec((1, H, HEAD_DIM, S), lambda b, i: (b, 0, 0, 0)),
                  pl.BlockSpec(tri.shape, lambda b, i: (0, 0))],
        out_specs=pl.BlockSpec((1, qb, D_GROUP), lambda b, i: (b, i, 0)),
        compiler_params=_params("parallel", "arbitrary"),
        name="sb_attn",
    )(q, k, vt, tri)


def _ordered_to_f32(c):
    bits = c ^ ((c >> 31) & jnp.int32(0x7FFFFFFF))
    return pltpu.bitcast(bits, F32)


def _dsa_kernel(q_ref, k_ref, vt_ref, qi_ref, ki_ref, misc_ref, o_ref, sc_ref, m_ref, t_ref, cnt_ref,
                *, seq, n_top):
    q0 = pl.program_id(1) * QB
    ck = DSA_CK
    hk = ck // 2
    nck = _n_chunks(q0, QB, ck)
    w_t = misc_ref[0].T[MISC_IDXW:MISC_IDXW + DSA_IDX_HEADS, :] * (DSA_IDX_HEADS * DSA_IDX_DIM) ** -0.5
    qi_all = qi_ref[0].reshape(DSA_IDX_HEADS * QB, 128)
    q_all = q_ref[0].reshape(HEADS * QB, HEAD_DIM)

    def halves(c):
        ks = c * ck
        return [pl.multiple_of(ks + j * hk, hk) for j in range(2)]

    def score_body(c, _):
        starts = halves(c)
        rs = [_dot_nt(ki_ref[0, pl.ds(ks, hk), :], qi_all) for ks in starts]
        for ks, r in zip(starts, rs):
            sc = jnp.zeros((hk, QB), F32)
            for h in range(DSA_IDX_HEADS):
                sc = sc + w_t[h:h + 1, :] * jnp.maximum(r[:, h * QB:(h + 1) * QB], 0.0)
            sc = jnp.where(sc == 0.0, 0.0, sc)
            sc_ref[pl.ds(ks, hk), :] = jnp.where(_key_pos(ks, hk) <= _query_pos(q0, hk), sc, NEG_INF)
        return 0

    lax.fori_loop(0, nck, score_body, 0)

    n_beyond = (seq - nck * ck).astype(F32)
    n_part = 32

    def count(pred_fn):
        def cb(c, acc):
            ks = pl.multiple_of(c * ck, ck)
            hit = jnp.where(pred_fn(sc_ref[pl.ds(ks, ck), :], ks), 1.0, 0.0)
            return acc + jnp.sum(hit.reshape(ck // n_part, n_part, QB), axis=0)
        acc = lax.fori_loop(0, nck, cb, jnp.zeros((n_part, QB), F32))
        return jnp.sum(acc, axis=0, keepdims=True)

    def count_ge(thr):
        return count(lambda s, ks: s >= thr) + jnp.where(thr <= NEG_INF, n_beyond, 0.0)

    def search(n_static):
        def count_ge_static(thr):
            acc = jnp.zeros((n_part, QB), F32)
            for c in range(n_static):
                hit = jnp.where(sc_ref[c * ck:(c + 1) * ck, :] >= thr, 1.0, 0.0)
                acc = acc + jnp.sum(hit.reshape(ck // n_part, n_part, QB), axis=0)
            cnt = jnp.sum(acc, axis=0, keepdims=True)
            return cnt + jnp.where(thr <= NEG_INF, float(seq - n_static * ck), 0.0)

        def thr_body(it, carry):
            t_int, cnt_t = carry
            cand = t_int ^ lax.shift_left(jnp.int32(1), 31 - it)
            cnt = count_ge_static(_ordered_to_f32(cand))
            ok = cnt >= n_top
            return jnp.where(ok, cand, t_int), jnp.where(ok, cnt, cnt_t)

        t_int, cnt_t = lax.fori_loop(0, 32, thr_body, (jnp.full((1, QB), -2 ** 31, jnp.int32),
                                                       jnp.full((1, QB), float(seq), F32)))
        t_ref[...] = jnp.broadcast_to(t_int, t_ref.shape)
        cnt_ref[...] = jnp.broadcast_to(cnt_t, cnt_ref.shape)

    for n_static in range(1, seq // ck + 1):
        pl.when(nck == n_static)(functools.partial(search, n_static))
    t_int = t_ref[0:1, :]
    cnt_t = cnt_ref[0:1, :]
    thr = _ordered_to_f32(t_int)
    thr_up = _ordered_to_f32(t_int + 1)
    m_ref[...] = jnp.full(m_ref.shape, seq, jnp.int32)

    @pl.when(jnp.max(cnt_t) > n_top)
    def _():
        need = n_top - count_ge(thr_up)
        tie_beyond = thr <= NEG_INF

        def idx_body(it, m):
            cand = m + lax.shift_left(jnp.int32(1), (seq.bit_length() - 2) - it)
            below = count(lambda s, ks: (s >= thr) & (s < thr_up) & (_key_pos(ks, ck) < cand))
            below = below + jnp.where(tie_beyond, jnp.maximum(cand - nck * ck, 0).astype(F32), 0.0)
            return jnp.where(below < need, cand, m)

        m_idx = lax.fori_loop(0, seq.bit_length() - 1, idx_body, jnp.zeros((1, QB), jnp.int32))
        m_ref[...] = jnp.broadcast_to(m_idx, m_ref.shape)

    m_idx = m_ref[0:1, :]

    tpos = _query_pos(q0, hk)

    def attn_scores(c):
        return tuple(_dot_nt(k_ref[0, pl.ds(ks, hk), :], q_all)
                     for ks in halves(c))

    def attn_consume(c, ss, carry):
        ms, acc = carry
        for ks, s_all in zip(halves(c), ss):
            kp = _key_pos(ks, hk)
            s_idx = sc_ref[pl.ds(ks, hk), :]
            sel = (s_idx >= thr_up) | ((s_idx >= thr) & (kp <= m_idx))
            mask = sel & (kp <= tpos)
            ms, alpha, p = _softmax_step_t(ms, s_all, mask)
            acc = alpha * acc + _dot(_with_ones(vt_ref[0, :, pl.ds(ks, hk)]), p)
        return tuple(ms), acc

    _, acc = _chunk_loop(nck, attn_scores, attn_consume,
                             (tuple(_m_init() for _ in range(HEADS)), _acc_init(HEADS)))
    _store_heads_t(o_ref, [_finish_t(acc, h) for h in range(HEADS)])


def _dsa(q, k, vt, qi, ki, misc):
    B, H, S, _ = q.shape
    n_top = min(DSA_TOPK, S // 4)
    return pl.pallas_call(
        functools.partial(_dsa_kernel, seq=S, n_top=n_top),
        out_shape=jax.ShapeDtypeStruct((B, S, D_GROUP), F32),
        grid=(B, S // QB),
        in_specs=[pl.BlockSpec((1, H, QB, 64), lambda b, i: (b, 0, i, 0)),
                  pl.BlockSpec((1, S, 64), lambda b, i: (b, 0, 0)),
                  pl.BlockSpec((1, HEAD_DIM, S), lambda b, i: (b, 0, 0)),
                  pl.BlockSpec((1, DSA_IDX_HEADS, QB, 128), lambda b, i: (b, 0, i, 0)),
                  pl.BlockSpec((1, S, 128), lambda b, i: (b, 0, 0)),
                  pl.BlockSpec((1, QB, 128), lambda b, i: (b, i, 0))],
        out_specs=pl.BlockSpec((1, QB, D_GROUP), lambda b, i: (b, i, 0)),
        scratch_shapes=[pltpu.VMEM((S, QB), F32), pltpu.VMEM((8, QB), jnp.int32),
                        pltpu.VMEM((8, QB), jnp.int32), pltpu.VMEM((8, QB), F32)],
        compiler_params=_params("parallel", "arbitrary"),
        name="dsa_attn",
    )(q, k, vt, qi, ki, misc)


def _gelu_tanh(x):
    return 0.5 * x * (1.0 + jnp.tanh(np.sqrt(2.0 / np.pi) * (x + 0.044715 * (x * x * x))))


def _nsa_cmp_kernel(kg_ref, vg_ref, pos_ref, wk1_ref, wk2_ref, wv1_ref, wv2_ref, kc_ref, vct_ref):
    def compress(g_ref, w1_ref, w2_ref):
        top = _dot((g_ref[0] + pos_ref[0:1, :]).astype(BF16), w1_ref[0])
        bot = _dot((g_ref[0] + pos_ref[1:2, :]).astype(BF16), w1_ref[1])
        pre = top + pltpu.roll(bot, bot.shape[0] - 1, 0)
        return _dot(_gelu_tanh(pre).astype(BF16), w2_ref[...])

    kc_ref[0] = compress(kg_ref, wk1_ref, wk2_ref).astype(BF16)
    vct_ref[0] = compress(vg_ref, wv1_ref, wv2_ref).T[:HEAD_DIM].astype(BF16)


def _nsa_compress(kcmp, vcmp, pos, wk1, wk2, wv1, wv2):
    B, S, _ = kcmp.shape
    ng = S // NSA_CMP_STRIDE
    gw = NSA_CMP_STRIDE * HEAD_DIM
    kg = kcmp.reshape(B, ng, gw)
    vg = vcmp.reshape(B, ng, gw)

    def const(a):
        return pl.BlockSpec(a.shape, lambda b: (0,) * a.ndim)

    wv2p = jnp.concatenate([wv2, jnp.zeros((HEAD_DIM, LANES - HEAD_DIM), wv2.dtype)], axis=1)
    ins = [kg, vg, pos.reshape(2, gw), wk1.reshape(2, gw, HEAD_DIM).astype(BF16), wk2.astype(BF16),
           wv1.reshape(2, gw, HEAD_DIM).astype(BF16), wv2p.astype(BF16)]
    blk = pl.BlockSpec((1, ng, gw), lambda b: (b, 0, 0))
    return pl.pallas_call(
        _nsa_cmp_kernel,
        out_shape=[jax.ShapeDtypeStruct((B, ng, HEAD_DIM), BF16), jax.ShapeDtypeStruct((B, HEAD_DIM, ng), BF16)],
        grid=(B,),
        in_specs=[blk, blk] + [const(a) for a in ins[2:]],
        out_specs=[pl.BlockSpec((1, ng, HEAD_DIM), lambda b: (b, 0, 0)),
                   pl.BlockSpec((1, HEAD_DIM, ng), lambda b: (b, 0, 0))],
        compiler_params=_params("parallel"),
        name="nsa_compress",
    )(*ins)


def _nsa_kernel(q_ref, kc_ref, vct_ref, ks_ref, vst_ref, kw_ref, vwt_ref, misc_ref, ovl_ref, o_ref,
                *, seq):
    i = pl.program_id(1)
    q0 = i * QB
    n_slc = seq // NSA_SEL_LEN
    n_cmp = kc_ref.shape[1]
    q_all = q_ref[0].reshape(HEADS * QB, HEAD_DIM)

    last_tok = lax.broadcasted_iota(jnp.int32, (n_cmp, QB), 0) * NSA_CMP_STRIDE + (NSA_CMP_LEN - 1)
    cmask = last_tok <= _query_pos(q0, n_cmp)
    s_cmp = _dot_nt(kc_ref[0], q_all)
    ws = jnp.maximum(q0 - NSA_WINDOW, 0)
    win_parts = [(pl.multiple_of(ws + off, QB), n) for off, n in NSA_WIN_PARTS]
    s_wins = [_dot_nt(kw_ref[0, pl.ds(start, n), :], q_all) for start, n in win_parts]
    ps = []
    p_sum = jnp.zeros((n_cmp, QB), F32)
    for h in range(HEADS):
        s = jnp.where(cmask, s_cmp[:, h * QB:(h + 1) * QB], NEG_INF)
        e = jnp.where(cmask, jnp.exp2(s - jnp.max(s, axis=0, keepdims=True)), 0.0)
        p = e / jnp.maximum(jnp.sum(e, axis=0, keepdims=True), 1e-30)
        p_sum = p_sum + p
        ps.append(p.astype(BF16))
    o_cmp = _dot(vct_ref[0], jnp.concatenate(ps, axis=1))

    hi, lo = _split_bf16(p_sum)
    imp = _dot(ovl_ref[...], jnp.concatenate([hi, lo], axis=0))
    jb = lax.broadcasted_iota(jnp.int32, (n_slc, 1), 0)
    cur = (q0 + lax.broadcasted_iota(jnp.int32, (1, QB), 1)) >> SEL_SHIFT
    forced = (jb < NSA_N_INIT) | ((jb <= cur) & (jb > cur - NSA_N_LOCAL))
    imp = jnp.where(forced, FORCE_SCORE, jnp.where(jb <= cur, imp, NEG_INF))
    rank = jnp.zeros((n_slc, QB), F32)
    for r in range(n_slc):
        row = imp[r:r + 1, :]
        beats = (row > imp) | ((row == imp) & (r < jb))
        rank = rank + jnp.where(beats, 1.0, 0.0)
    sel_t = jnp.where(rank < min(NSA_N_SEL, n_slc), 1.0, 0.0)
    sel_t = jnp.concatenate([sel_t, jnp.zeros((LANES - n_slc, QB), F32)], axis=0).astype(BF16)

    ms_win, acc_win = [_m_init() for _ in range(HEADS)], _acc_init(HEADS)
    for (start, n), s_win in zip(win_parts, s_wins):
        dist = _query_pos(q0, n) - _key_pos(start, n)
        wmask = (dist >= 0) & (dist < NSA_WINDOW)
        ms_win, alpha, p_win = _softmax_step_t(ms_win, s_win, wmask)
        acc_win = alpha * acc_win + _dot(_with_ones(vwt_ref[0, :, pl.ds(start, n)]), p_win)

    ck = DSA_CK
    hk = ck // 2
    tpos = _query_pos(q0, hk)

    n_slc_chunks = _n_chunks(q0, QB, ck)

    def slc_starts(c):
        return [pl.multiple_of(c * ck + j * hk, hk) for j in range(2)]

    def slc_scores(c):
        out = []
        for ks in slc_starts(c):
            blk = (ks + lax.broadcasted_iota(jnp.int32, (hk, LANES), 0)) >> SEL_SHIFT
            expand = jnp.where(blk == lax.broadcasted_iota(jnp.int32, (hk, LANES), 1), 1.0, 0.0)
            out.append(_dot(expand.astype(BF16), sel_t))
        return tuple(out) + tuple(_dot_nt(ks_ref[0, pl.ds(ks, hk), :], q_all) for ks in slc_starts(c))

    def slc_consume(c, ss, carry):
        ms, acc = carry
        for ks, chosen, s_all in zip(slc_starts(c), ss[:2], ss[2:]):
            mask = (chosen > 0.5) & (_key_pos(ks, hk) <= tpos)
            ms, alpha, p = _softmax_step_t(ms, s_all, mask)
            acc = alpha * acc + _dot(_with_ones(vst_ref[0, :, pl.ds(ks, hk)]), p)
        return tuple(ms), acc

    _, acc_slc = _chunk_loop(n_slc_chunks, slc_scores, slc_consume,
                                 (tuple(_m_init() for _ in range(HEADS)), _acc_init(HEADS)))

    gate_rows = 16
    gate = jax.nn.sigmoid(misc_ref[0].T[MISC_GATE:MISC_GATE + gate_rows, :])
    outs = []
    for h in range(HEADS):
        outs.append(gate[h:h + 1] * o_cmp[:, h * QB:(h + 1) * QB]
                    + gate[HEADS + h:HEADS + h + 1] * _finish_t(acc_slc, h)
                    + gate[2 * HEADS + h:2 * HEADS + h + 1] * _finish_t(acc_win, h))
    _store_heads_t(o_ref, outs)


def _nsa_overlap(seq):
    ng = seq // NSA_CMP_STRIDE
    n_slc = seq // NSA_SEL_LEN
    n = np.arange(ng)
    first, last = n * NSA_CMP_STRIDE, n * NSA_CMP_STRIDE + NSA_CMP_LEN - 1
    start = np.arange(n_slc) * NSA_SEL_LEN
    ovl = ((first[None, :] <= start[:, None] + NSA_SEL_LEN - 1) & (last[None, :] >= start[:, None]))
    ovl = ovl & (last[None, :] < seq)
    ovl = ovl.astype(np.float32)
    return jnp.asarray(np.concatenate([ovl, ovl], 1), BF16)


def _nsa(q, kc, vct, k_slc, vt_slc, k_win, vt_win, misc):
    B, H, S, _ = q.shape
    ng = kc.shape[1]
    ovl = _nsa_overlap(S)
    keys = pl.BlockSpec((1, S, HEAD_DIM), lambda b, i: (b, 0, 0))
    vals_t = pl.BlockSpec((1, HEAD_DIM, S), lambda b, i: (b, 0, 0))
    return pl.pallas_call(
        functools.partial(_nsa_kernel, seq=S),
        out_shape=jax.ShapeDtypeStruct((B, S, D_GROUP), F32),
        grid=(B, S // QB),
        in_specs=[pl.BlockSpec((1, H, QB, 64), lambda b, i: (b, 0, i, 0)),
                  pl.BlockSpec((1, ng, HEAD_DIM), lambda b, i: (b, 0, 0)),
                  pl.BlockSpec((1, HEAD_DIM, ng), lambda b, i: (b, 0, 0)),
                  keys, vals_t, keys, vals_t,
                  pl.BlockSpec((1, QB, 128), lambda b, i: (b, i, 0)),
                  pl.BlockSpec(ovl.shape, lambda b, i: (0, 0))],
        out_specs=pl.BlockSpec((1, QB, D_GROUP), lambda b, i: (b, i, 0)),
        compiler_params=_params("parallel", "arbitrary"),
        name="nsa_attn",
    )(q, kc, vct, k_slc, vt_slc, k_win, vt_win, misc, ovl)


def _outproj_kernel(x_ref, ya_ref, yb_ref, yc_ref, yd_ref, gg_ref, wo_ref, g_ref, b_ref, o_ref):
    acc = None
    for gi, y_ref in enumerate((ya_ref, yb_ref, yc_ref, yd_ref)):
        lo, hi = gi * D_GROUP, (gi + 1) * D_GROUP
        yn = _rms_norm(y_ref[...], gg_ref[:, lo:hi]).astype(BF16)
        part = _dot(yn, wo_ref[lo:hi, :])
        acc = part if acc is None else acc + part
    o_ref[...] = _layer_norm(DN_ALPHA * x_ref[...] + acc, g_ref[...], b_ref[...])


def _outproj(x2, ys, gg, wo, g, b, *, tm):
    n, d = x2.shape
    row = lambda w: pl.BlockSpec((tm, w), lambda i: (i, 0))
    const = lambda a: pl.BlockSpec(a.shape, lambda i: (0, 0))
    return pl.pallas_call(
        _outproj_kernel,
        out_shape=jax.ShapeDtypeStruct((n, d), F32),
        grid=(n // tm,),
        in_specs=[row(d)] + [row(D_GROUP)] * 4 + [const(gg), const(wo), const(g), const(b)],
        out_specs=row(d),
        compiler_params=_params("parallel"),
        name="out_proj",
    )(x2, *ys, gg, wo, g, b)


def _mixer(x, w_in, q_norm, w_uq, kv_norm, w_ukv, cmp_pos, wk1, wk2, wv1, wv2, tabs, place, maps):
    B, S, D = x.shape
    wm = _gather_cols(w_in, maps).astype(BF16)
    wt = jnp.concatenate([wm[:, slice(*_MAIN_OFF['slab_v'])], wm[:, slice(*_MAIN_OFF['sb_v'])]], axis=1).T
    (mq, mk, mvt, nq, dq, sbq, sbk, sbvt, kcmp, kslc, kwin, dk, vcmp, vslct, vwint, dvt,
     qi, ki, misc) = _inproj(x, wm, wt, tabs, q_norm[None, :], kv_norm[None, :],
                             _mla_weights(w_uq, w_ukv), place, tm=min(512, S))
    y_a = _mla(mq, mk, mvt)
    kc, vct = _nsa_compress(kcmp, vcmp, cmp_pos, wk1, wk2, wv1, wv2)
    y_b = _nsa(nq, kc, vct, kslc, vslct, kwin, vwint, misc)
    y_c = _dsa(dq, dk, dvt, qi, ki, misc)
    y_d = _sb(sbq, sbk, sbvt)
    return y_a, y_b, y_c, y_d


def kernel(x, ln1_g, ln1_b, ffn1_w1, ffn1_w3, ffn1_w2, w_in, mla_q_norm, mla_w_uq, mla_kv_norm,
           mla_w_ukv, nsa_cmp_pos, nsa_cmp_wk1, nsa_cmp_wk2, nsa_cmp_wv1, nsa_cmp_wv2, group_norm_g,
           w_out, ln2_g, ln2_b, ffn2_w1, ffn2_w3, ffn2_w2, ln3_g, ln3_b):
    B, S, D = x.shape
    n = B * S
    tabs = _rope_tables(S)
    place = _placement_constants()
    maps = _column_maps()
    tm_ffn = min(512, n)
    x2 = x.reshape(n, D)
    for l in range(DEPTH):
        x2 = _ffn_ln(x2, ffn1_w1[l].astype(BF16), ffn1_w3[l].astype(BF16), ffn1_w2[l].astype(BF16),
                     ln1_g[l][None, :], ln1_b[l][None, :], tm=tm_ffn, tf=256)
        ys = _mixer(x2.reshape(B, S, D), w_in[l], mla_q_norm[l], mla_w_uq[l], mla_kv_norm[l],
                    mla_w_ukv[l], nsa_cmp_pos[l], nsa_cmp_wk1[l], nsa_cmp_wk2[l], nsa_cmp_wv1[l],
                    nsa_cmp_wv2[l], tabs, place, maps)
        x2 = _outproj(x2, [y.reshape(n, D_GROUP) for y in ys], group_norm_g[l][None, :],
                      w_out[l].astype(BF16), ln2_g[l][None, :], ln2_b[l][None, :], tm=min(1024, n))
        x2 = _ffn_ln(x2, ffn2_w1[l].astype(BF16), ffn2_w3[l].astype(BF16), ffn2_w2[l].astype(BF16),
                     ln3_g[l][None, :], ln3_b[l][None, :], tm=tm_ffn, tf=256)
    return x2.reshape(B, S, D)
```

```python
import functools

import numpy as np
import jax
import jax.numpy as jnp
from jax import lax
from jax.experimental import pallas as pl
from jax.experimental.pallas import tpu as pltpu

F32 = jnp.float32
BF16 = jnp.bfloat16

D_MODEL = 1024
DEPTH = 2
HEADS = 4
HEAD_DIM = 64
D_GROUP = HEADS * HEAD_DIM
N_GROUPS = 4
D_FF = 2816
ROPE_THETA = 10000.0
MLA_Q_RANK = 256
MLA_KV_RANK = 128
MLA_D_NOPE = 64
MLA_D_ROPE = 32
MLA_D_V = 64
NSA_CMP_LEN = 32
NSA_CMP_STRIDE = 16
NSA_SEL_LEN = 64
NSA_N_SEL = 8
NSA_N_INIT = 1
NSA_N_LOCAL = 2
NSA_WINDOW = 512
SEL_SHIFT = NSA_SEL_LEN.bit_length() - 1
DSA_TOPK = 256
DSA_IDX_HEADS = 8
DSA_IDX_DIM = 32
DN_ALPHA = (2.0 * DEPTH) ** 0.25
LN_EPS = 1e-5
RMS_EPS = 1e-6
NEG_INF = -1e30
FORCE_SCORE = 1e4
LOG2_E = 1.4426950408889634

LANES = 128
QB = 128
CK = 256
DSA_CK = 512
NSA_WIN_PARTS = ((0, NSA_WINDOW + QB),)
VMEM_LIMIT = 48 * 1024 * 1024

MISC_KROPE = 0
MISC_IDXK = 32
MISC_IDXW = 64
MISC_GATE = 72

NT_DIMS = (((1,), (1,)), ((), ()))


def _dot(a, b):
    return jnp.dot(a, b, preferred_element_type=F32)


def _dot_nt(a, b):
    return lax.dot_general(a, b, NT_DIMS, preferred_element_type=F32)


def _split_bf16(x):
    hi = x.astype(BF16)
    lo = (x - hi.astype(F32)).astype(BF16)
    return hi, lo


def _layer_norm(y, g, b):
    mu = jnp.mean(y, -1, keepdims=True)
    d = y - mu
    var = jnp.mean(d * d, -1, keepdims=True)
    return d * lax.rsqrt(var + LN_EPS) * g + b


def _rms_norm(y, g):
    return y * lax.rsqrt(jnp.mean(y * y, -1, keepdims=True) + RMS_EPS) * g


def _params(*sem):
    return pltpu.CompilerParams(dimension_semantics=sem, vmem_limit_bytes=VMEM_LIMIT)


def _ffn_kernel(x_ref, w1_ref, w3_ref, w2_ref, g_ref, b_ref, o_ref, a_ref, *, tf):
    xb = x_ref[...].astype(BF16)
    n_slabs = w1_ref.shape[1] // tf

    def up(j):
        return _dot(xb, w1_ref[:, j * tf:(j + 1) * tf]), _dot(xb, w3_ref[:, j * tf:(j + 1) * tf])

    hu = up(0)
    for j in range(n_slabs):
        nxt = up(j + 1) if j + 1 < n_slabs else None
        h, u = hu
        a_ref[:, j * tf:(j + 1) * tf] = (h * jax.nn.sigmoid(h) * u).astype(BF16)
        hu = nxt
    y = DN_ALPHA * x_ref[...] + 0.5 * _dot(a_ref[...], w2_ref[...])
    o_ref[...] = _layer_norm(y, g_ref[...], b_ref[...])


def _ffn_ln(x2, w1, w3, w2, g, b, *, tm, tf):
    n, d = x2.shape
    dff = w1.shape[1]
    const = lambda a: pl.BlockSpec(a.shape, lambda i: (0, 0))
    return pl.pallas_call(
        functools.partial(_ffn_kernel, tf=tf),
        out_shape=jax.ShapeDtypeStruct((n, d), F32),
        grid=(n // tm,),
        in_specs=[pl.BlockSpec((tm, d), lambda i: (i, 0)), const(w1), const(w3), const(w2), const(g), const(b)],
        out_specs=pl.BlockSpec((tm, d), lambda i: (i, 0)),
        scratch_shapes=[pltpu.VMEM((tm, dff), BF16)],
        compiler_params=_params("parallel"),
        name="ffn_ln",
    )(x2, w1, w3, w2, g, b)


_MAIN_ORDER = (
    ('mla_cq', 256), ('nsa_q', 256), ('dsa_q', 256), ('idx_q', 256),
    ('sb_q', 256), ('sb_k', 256), ('sb_v', 256),
    ('slab_k', 256),
    ('slab_v', 256),
    ('mla_ckv', 128),
    ('misc', 128),
)
_MAIN_OFF = {}
_o = 0
for _n, _w in _MAIN_ORDER:
    _MAIN_OFF[_n] = (_o, _o + _w)
    _o += _w
N_MAIN = _o

_IN_SPLITS = (
    ('mla_cq', MLA_Q_RANK), ('mla_ckv', MLA_KV_RANK), ('mla_krope', MLA_D_ROPE),
    ('nsa_q', D_GROUP), ('nsa_k_cmp', HEAD_DIM), ('nsa_v_cmp', HEAD_DIM),
    ('nsa_k_slc', HEAD_DIM), ('nsa_v_slc', HEAD_DIM), ('nsa_k_win', HEAD_DIM),
    ('nsa_v_win', HEAD_DIM), ('nsa_gate', 3 * HEADS),
    ('dsa_q', D_GROUP), ('dsa_k', HEAD_DIM), ('dsa_v', HEAD_DIM),
    ('idx_q', DSA_IDX_HEADS * DSA_IDX_DIM), ('idx_k', DSA_IDX_DIM), ('idx_w', DSA_IDX_HEADS),
    ('sb_q', D_GROUP), ('sb_k', D_GROUP), ('sb_v', D_GROUP),
)
_SRC = {}
_o = 0
for _n, _w in _IN_SPLITS:
    _SRC[_n] = np.arange(_o, _o + _w)
    _o += _w
D_IN = _o


def _swap_halves(width, dim):
    idx = np.arange(width)
    return (idx // dim) * dim + (idx % dim + dim // 2) % dim


def _column_maps():
    pieces = {
        'mla_cq': [_SRC['mla_cq']], 'nsa_q': [_SRC['nsa_q']], 'dsa_q': [_SRC['dsa_q']],
        'idx_q': [_SRC['idx_q']], 'sb_q': [_SRC['sb_q']], 'sb_k': [_SRC['sb_k']], 'sb_v': [_SRC['sb_v']],
        'slab_k': [_SRC['nsa_k_cmp'], _SRC['nsa_k_slc'], _SRC['nsa_k_win'], _SRC['dsa_k']],
        'slab_v': [_SRC['nsa_v_cmp'], _SRC['nsa_v_slc'], _SRC['nsa_v_win'], _SRC['dsa_v']],
        'mla_ckv': [_SRC['mla_ckv']],
        'misc': [_SRC['mla_krope'], _SRC['idx_k'], _SRC['idx_w'], _SRC['nsa_gate'],
                 -np.ones(LANES - MISC_GATE - 3 * HEADS, np.int64)],
    }
    return np.concatenate([np.concatenate(pieces[n]) for n, _ in _MAIN_ORDER])


def _gather_cols(w, cols):
    out = jnp.take(w, np.maximum(cols, 0), axis=1)
    return jnp.where(jnp.asarray(cols >= 0)[None, :], out, 0.0)


def _rope_tables(seq):
    def base(dim):
        inv = ROPE_THETA ** (-jnp.arange(0, dim, 2, dtype=F32) / dim)
        ang = jnp.arange(seq, dtype=F32)[:, None] * inv[None, :]
        c, s = jnp.cos(ang), jnp.sin(ang)
        return jnp.concatenate([c, c], -1), jnp.concatenate([-s, s], -1)

    c64, s64 = base(HEAD_DIM)
    c32, s32 = base(DSA_IDX_DIM)
    t64 = (jnp.tile(c64, (1, 4)), jnp.tile(s64, (1, 4)))
    t32 = (jnp.tile(c32, (1, 8)), jnp.tile(s32, (1, 8)))
    ones = jnp.ones((seq, LANES - 64), F32)
    td = (jnp.concatenate([c32, c32, ones], -1), jnp.concatenate([s32, s32, 0.0 * ones], -1))
    scale = (MLA_D_NOPE + MLA_D_ROPE) ** -0.5 * LOG2_E
    cq = jnp.concatenate([jnp.ones((seq, 64), F32), c32, jnp.zeros((seq, 32), F32)], -1) * scale
    sq = jnp.concatenate([jnp.zeros((seq, 64), F32), s32, jnp.zeros((seq, 32), F32)], -1) * scale
    tq = (jnp.tile(cq, (1, 4)), jnp.tile(sq, (1, 4)))
    return t64, t32, td, tq


def _placement_constants():
    p_hi = np.zeros((256, 1024), np.float32)
    p_lo = np.zeros((256, 1024), np.float32)
    for h in range(DSA_IDX_HEADS):
        for d in range(DSA_IDX_DIM):
            p_hi[h * 32 + d, h * 128 + d] = 1
            p_hi[h * 32 + d, h * 128 + 64 + d] = 1
            p_lo[h * 32 + d, h * 128 + 32 + d] = 1
    pk_hi = np.zeros((128, 128), np.float32)
    pk_lo = np.zeros((128, 128), np.float32)
    for d in range(DSA_IDX_DIM):
        pk_hi[MISC_IDXK + d, d] = 1
        pk_hi[MISC_IDXK + d, 32 + d] = 1
        pk_lo[MISC_IDXK + d, 64 + d] = 1
    pk_pe = np.zeros((128, 512), np.float32)
    for h in range(HEADS):
        for d in range(MLA_D_ROPE):
            pk_pe[MISC_KROPE + d, h * 128 + 64 + d] = 1
    return tuple(jnp.asarray(a, BF16) for a in (p_hi, p_lo, pk_hi, pk_lo, pk_pe))


def _mla_weights(w_uq, w_ukv):
    dq = MLA_D_NOPE + MLA_D_ROPE
    cols_q = -np.ones(512, np.int64)
    cols_qr = -np.ones(512, np.int64)
    cols_k = -np.ones(512, np.int64)
    cols_v = np.zeros(256, np.int64)
    for h in range(HEADS):
        cols_q[h * 128:h * 128 + dq] = h * dq + np.arange(dq)
        pe = h * dq + MLA_D_NOPE + np.arange(MLA_D_ROPE)
        cols_qr[h * 128 + 64:h * 128 + 96] = pe[_swap_halves(32, 32)]
        cols_k[h * 128:h * 128 + 64] = h * 128 + np.arange(64)
        cols_v[h * 64:(h + 1) * 64] = h * 128 + 64 + np.arange(64)
    return (_gather_cols(w_uq, cols_q).astype(BF16), _gather_cols(w_uq, cols_qr).astype(BF16),
            _gather_cols(w_ukv, cols_k).astype(BF16), jnp.take(w_ukv, cols_v, axis=1).T.astype(BF16))


def _rotate_half(z, dim):
    n = z.shape[1]
    half = dim // 2
    lane = lax.broadcasted_iota(jnp.int32, z.shape, 1)
    in_first_half = (lane & (dim - 1)) < half
    return jnp.where(in_first_half, pltpu.roll(z, n - half, 1), pltpu.roll(z, half, 1))


def _inproj_kernel(x_ref, wm_ref, wt_ref, c64_ref, s64_ref, c32_ref, s32_ref, cd_ref, sd_ref,
                   cq_ref, sq_ref, qg_ref, kvg_ref, wuq_ref, wuqr_ref, wuk_ref, wuvt_ref,
                   phi_ref, plo_ref, pkhi_ref, pklo_ref, pkpe_ref,
                   mq_ref, mk_ref, mvt_ref, nq_ref, dq_ref, sbq_ref, sbk_ref, sbvt_ref,
                   kcmp_ref, kslc_ref, kwin_ref, dk_ref, vcmp_ref, vslct_ref, vwint_ref, dvt_ref,
                   qi_ref, ki_ref, misc_ref):
    xb = x_ref[0].astype(BF16)

    def main(name):
        a, b = _MAIN_OFF[name]
        return _dot(xb, wm_ref[:, a:b])

    def roped(name, c_ref, s_ref, dim):
        z = main(name)
        return z * c_ref[...] + _rotate_half(z, dim) * s_ref[...]

    def store_heads(ref, val, width):
        for h in range(HEADS):
            ref[0, h] = val[:, h * width:(h + 1) * width].astype(ref.dtype)

    cqn = _rms_norm(main('mla_cq'), qg_ref[...]).astype(BF16)
    q = _dot(cqn, wuq_ref[...]) * cq_ref[...] + _dot(cqn, wuqr_ref[...]) * sq_ref[...]
    store_heads(mq_ref, q, 128)
    ckv_misc = _dot(xb, wm_ref[:, _MAIN_OFF['mla_ckv'][0]:_MAIN_OFF['misc'][1]])
    z_misc = ckv_misc[:, LANES:]
    misc = z_misc * cd_ref[...] + _rotate_half(z_misc, DSA_IDX_DIM) * sd_ref[...]
    misc_ref[0] = misc
    m_hi, m_lo = _split_bf16(misc)
    ckvn = _rms_norm(ckv_misc[:, :LANES], kvg_ref[...]).astype(BF16)
    k = _dot(ckvn, wuk_ref[...]) + _dot(m_hi, pkpe_ref[...])
    store_heads(mk_ref, k, 128)
    mvt = _dot_nt(wuvt_ref[...], ckvn)
    for h in range(HEADS):
        mvt_ref[0, h] = mvt[h * 64:(h + 1) * 64, :].astype(BF16)
    qscale = HEAD_DIM ** -0.5 * LOG2_E
    store_heads(nq_ref, roped('nsa_q', c64_ref, s64_ref, HEAD_DIM) * qscale, 64)
    store_heads(dq_ref, roped('dsa_q', c64_ref, s64_ref, HEAD_DIM) * qscale, 64)
    store_heads(sbq_ref, main('sb_q') * qscale, 64)
    store_heads(sbk_ref, main('sb_k'), 64)
    sk = roped('slab_k', c64_ref, s64_ref, HEAD_DIM)
    for j, ref in enumerate((kcmp_ref, kslc_ref, kwin_ref, dk_ref)):
        ref[0] = sk[:, j * 64:(j + 1) * 64].astype(ref.dtype)
    vt = _dot_nt(wt_ref[...], xb)
    vcmp_ref[0] = vt[0:LANES, :].T[:, 0:HEAD_DIM]
    for j, ref in ((1, vslct_ref), (2, vwint_ref), (3, dvt_ref)):
        ref[0] = vt[j * 64:(j + 1) * 64, :].astype(BF16)
    for h in range(HEADS):
        sbvt_ref[0, h] = vt[256 + h * 64:256 + (h + 1) * 64, :].astype(BF16)
    qi_hi, qi_lo = _split_bf16(roped('idx_q', c32_ref, s32_ref, DSA_IDX_DIM))
    qi = _dot(qi_hi, phi_ref[...]) + _dot(qi_lo, plo_ref[...])
    for h in range(DSA_IDX_HEADS):
        qi_ref[0, h] = qi[:, h * 128:(h + 1) * 128].astype(BF16)
    ki_ref[0] = (_dot(m_hi, pkhi_ref[...]) + _dot(m_lo, pklo_ref[...])).astype(BF16)


def _inproj(x, wm, wt, tabs, qg, kvg, mla_w, place, *, tm):
    B, S, D = x.shape
    (c64, s64), (c32, s32), (cd, sd), (cq, sq) = tabs
    wuq, wuqr, wuk, wuvt = mla_w

    def const(a):
        return pl.BlockSpec(a.shape, lambda b, s: (0,) * a.ndim)

    def tab(a):
        return pl.BlockSpec((tm, a.shape[1]), lambda b, s: (s, 0))

    def heads_out(w, dt=BF16):
        return (jax.ShapeDtypeStruct((B, HEADS, S, w), dt),
                pl.BlockSpec((1, HEADS, tm, w), lambda b, s: (b, 0, s, 0)))

    def flat_out(w, dt):
        return (jax.ShapeDtypeStruct((B, S, w), dt), pl.BlockSpec((1, tm, w), lambda b, s: (b, s, 0)))

    heads_t = (jax.ShapeDtypeStruct((B, HEADS, HEAD_DIM, S), BF16),
               pl.BlockSpec((1, HEADS, HEAD_DIM, tm), lambda b, s: (b, 0, 0, s)))
    flat_t = (jax.ShapeDtypeStruct((B, HEAD_DIM, S), BF16),
              pl.BlockSpec((1, HEAD_DIM, tm), lambda b, s: (b, 0, s)))
    qi_out = (jax.ShapeDtypeStruct((B, DSA_IDX_HEADS, S, 128), BF16),
              pl.BlockSpec((1, DSA_IDX_HEADS, tm, 128), lambda b, s: (b, 0, s, 0)))
    outs = [heads_out(128), heads_out(128), heads_t,
            heads_out(64), heads_out(64),
            heads_out(64), heads_out(64), heads_t,
            flat_out(64, F32), flat_out(64, BF16), flat_out(64, BF16), flat_out(64, BF16),
            flat_out(64, F32), flat_t, flat_t, flat_t,
            qi_out, flat_out(128, BF16), flat_out(128, F32)]
    ins = [x, wm, wt, c64, s64, c32, s32, cd, sd, cq, sq, qg, kvg, wuq, wuqr, wuk, wuvt, *place]
    in_specs = [pl.BlockSpec((1, tm, D), lambda b, s: (b, s, 0)), const(wm), const(wt),
                tab(c64), tab(s64), tab(c32), tab(s32), tab(cd), tab(sd), tab(cq), tab(sq),
                const(qg), const(kvg), const(wuq), const(wuqr), const(wuk), const(wuvt),
                *[const(p) for p in place]]
    return pl.pallas_call(
        _inproj_kernel,
        out_shape=[o[0] for o in outs],
        grid=(B, S // tm),
        in_specs=in_specs,
        out_specs=[o[1] for o in outs],
        compiler_params=_params("parallel", "arbitrary"),
        name="in_proj",
    )(*ins)


V_ROWS = HEAD_DIM


def _acc_init(n_heads, qb=QB):
    return jnp.zeros((2 * V_ROWS, n_heads * qb), F32)


def _m_init(qb=QB):
    return jnp.full((1, qb), NEG_INF, F32)


def _with_ones(vt):
    return jnp.concatenate([vt, jnp.ones((V_ROWS, vt.shape[1]), vt.dtype)], axis=0)


def _softmax_step_t(ms, s_all, mask):
    new_ms, alphas, es = [], [], []
    qb = mask.shape[1]
    for h, m in enumerate(ms):
        s = jnp.where(mask, s_all[:, h * qb:(h + 1) * qb], NEG_INF)
        m_new = jnp.maximum(m, jnp.max(s, axis=0, keepdims=True))
        alphas.append(jnp.exp2(m - m_new))
        es.append(jnp.where(mask, jnp.exp2(s - m_new), 0.0).astype(BF16))
        new_ms.append(m_new)
    return new_ms, jnp.concatenate(alphas, axis=1), jnp.concatenate(es, axis=1)


def _finish_t(acc, h, qb=QB):
    blk = acc[:, h * qb:(h + 1) * qb]
    return blk[:V_ROWS] / jnp.maximum(blk[V_ROWS:V_ROWS + 1], 1e-30)


def _store_heads_t(o_ref, outs_t):
    for p in range(HEADS // 2):
        pair = jnp.concatenate([outs_t[2 * p], outs_t[2 * p + 1]], axis=0)
        o_ref[0, :, p * 128:(p + 1) * 128] = pair.T


def _chunk_loop(n, scores, consume, init):
    return lax.fori_loop(0, n, lambda c, state: consume(c, scores(c), state), init)


def _n_chunks(q0, qb=QB, ck=CK):
    return (q0 + qb + ck - 1) // ck


def _key_pos(ks, n, qb=QB):
    return ks + lax.broadcasted_iota(jnp.int32, (n, qb), 0)


def _query_pos(q0, n, qb=QB):
    return q0 + lax.broadcasted_iota(jnp.int32, (n, qb), 1)


QB_WIDE = 256


def _mla_kernel(q_ref, k_ref, vt_ref, o_ref):
    qb = QB_WIDE
    q0 = pl.program_id(1) * qb
    tpos = _query_pos(q0, CK, qb)

    n = _n_chunks(q0, qb)

    def scores(c):
        ks = pl.multiple_of(c * CK, CK)
        return tuple(_dot_nt(k_ref[0, h, pl.ds(ks, CK), :], q_ref[0, h]) for h in range(HEADS))

    def consume(c, ss, carry):
        ms, accs = carry
        ks = pl.multiple_of(c * CK, CK)
        mask = _key_pos(ks, CK, qb) <= tpos
        steps = [_softmax_step_t([ms[h]], ss[h], mask) for h in range(HEADS)]
        new_accs = [steps[h][1] * accs[h] + _dot(_with_ones(vt_ref[0, h, :, pl.ds(ks, CK)]), steps[h][2])
                    for h in range(HEADS)]
        return tuple(st[0][0] for st in steps), tuple(new_accs)

    init = (tuple(_m_init(qb) for _ in range(HEADS)), tuple(_acc_init(1, qb) for _ in range(HEADS)))
    _, accs = _chunk_loop(n, scores, consume, init)
    _store_heads_t(o_ref, [_finish_t(accs[h], 0, qb) for h in range(HEADS)])


def _mla(q, k, vt):
    B, H, S, _ = q.shape
    qb = QB_WIDE
    return pl.pallas_call(
        _mla_kernel,
        out_shape=jax.ShapeDtypeStruct((B, S, D_GROUP), F32),
        grid=(B, S // qb),
        in_specs=[pl.BlockSpec((1, H, qb, 128), lambda b, i: (b, 0, i, 0)),
                  pl.BlockSpec((1, H, S, 128), lambda b, i: (b, 0, 0, 0)),
                  pl.BlockSpec((1, H, HEAD_DIM, S), lambda b, i: (b, 0, 0, 0))],
        out_specs=pl.BlockSpec((1, qb, D_GROUP), lambda b, i: (b, i, 0)),
        compiler_params=_params("parallel", "arbitrary"),
        name="mla_attn",
    )(q, k, vt)


SB_TOT_ROWS = 16


SB_CK = 128


def _sb_kernel(q_ref, k_ref, vt_ref, tri_ref, o_ref):
    qb, kc = QB_WIDE, SB_CK
    n_sub = 2
    step = n_sub * kc
    q0 = pl.program_id(1) * qb
    n_steps = (q0 + qb) // step
    tri = tri_ref[...]

    def key_start(r):
        return pl.multiple_of(q0 + qb - (r + 1) * step, step)

    def scores(r):
        ks = key_start(r)
        return tuple(_dot_nt(k_ref[0, h, pl.ds(ks, step), :], q_ref[0, h]) for h in range(HEADS))

    def consume(r, zs, carry, masked):
        runs, accs = carry
        ks = key_start(r)
        if masked:
            mask = _key_pos(ks, step, qb) < _query_pos(q0, step, qb)
        sps, parts = [], []
        for h in range(HEADS):
            sp = jnp.maximum(zs[h], 0.0) + jnp.log2(1.0 + jnp.exp2(-jnp.abs(zs[h])))
            sps.append(sp)
            parts.append(_split_bf16(jnp.where(mask, sp, 0.0) if masked else sp))
        stacked = jnp.concatenate(
            [jnp.concatenate([parts[h][j][b * kc:(b + 1) * kc] for j in range(2)], axis=0)
             for b in range(n_sub) for h in range(HEADS)], axis=1)
        res = _dot(tri, stacked)
        new_runs, new_accs = [], []
        for h in range(HEADS):
            run = runs[h]
            afters = [None] * n_sub
            for b in reversed(range(n_sub)):
                col = (b * HEADS + h) * qb
                afters[b] = res[:kc, col:col + qb] + run
                run = run + res[kc:kc + 1, col:col + qb]
            a = jnp.exp2(zs[h] - sps[h] + jnp.concatenate(afters, axis=0))
            if masked:
                a = jnp.where(mask, a, 0.0)
            new_accs.append(accs[h] + _dot(vt_ref[0, h, :, pl.ds(ks, step)], a.astype(BF16)))
            new_runs.append(run)
        return tuple(new_runs), tuple(new_accs)

    init = (tuple(jnp.zeros((1, qb), F32) for _ in range(HEADS)),
            tuple(jnp.zeros((HEAD_DIM, qb), F32) for _ in range(HEADS)))
    carry = consume(0, scores(0), init, True)
    _, accs = lax.fori_loop(1, n_steps, lambda r, c: consume(r, scores(r), c, False), carry)
    _store_heads_t(o_ref, accs)


def _sb(q, k, vt):
    B, H, S, _ = q.shape
    qb, kc = QB_WIDE, SB_CK
    u = np.triu(np.ones((kc, kc), np.float32), 1)
    u = np.concatenate([u, np.ones((SB_TOT_ROWS, kc), np.float32)], 0)
    tri = jnp.asarray(-np.concatenate([u, u], 1), BF16)
    return pl.pallas_call(
        _sb_kernel,
        out_shape=jax.ShapeDtypeStruct((B, S, D_GROUP), F32),
        grid=(B, S // qb),
        in_specs=[pl.BlockSpec((1, H, qb, 64), lambda b, i: (b, 0, i, 0)),
                  pl.BlockSpec((1, H, S, 64), lambda b, i: (b, 0, 0, 0)),
                  pl.BlockSpec((1, H, HEAD_DIM, S), lambda b, i: (b, 0, 0, 0)),
                  pl.BlockSpec(tri.shape, lambda b, i: (0, 0))],
        out_specs=pl.BlockSpec((1, qb, D_GROUP), lambda b, i: (b, i, 0)),
        compiler_params=_params("parallel", "arbitrary"),
        name="sb_attn",
    )(q, k, vt, tri)


def _ordered_to_f32(c):
    bits = c ^ ((c >> 31) & jnp.int32(0x7FFFFFFF))
    return pltpu.bitcast(bits, F32)


def _dsa_kernel(q_ref, k_ref, vt_ref, qi_ref, ki_ref, misc_ref, o_ref, sc_ref, m_ref, t_ref, cnt_ref,
                lo_ref, *, seq, n_top):
    q0 = pl.program_id(1) * QB
    ck = DSA_CK
    hk = ck // 2
    nck = _n_chunks(q0, QB, ck)
    w_t = misc_ref[0].T[MISC_IDXW:MISC_IDXW + DSA_IDX_HEADS, :] * (DSA_IDX_HEADS * DSA_IDX_DIM) ** -0.5
    qi_all = qi_ref[0].reshape(DSA_IDX_HEADS * QB, 128)
    q_all = q_ref[0].reshape(HEADS * QB, HEAD_DIM)

    def halves(c):
        ks = c * ck
        return [pl.multiple_of(ks + j * hk, hk) for j in range(2)]

    def score_body(c, _):
        starts = halves(c)
        rs = [_dot_nt(ki_ref[0, pl.ds(ks, hk), :], qi_all) for ks in starts]
        for ks, r in zip(starts, rs):
            sc = jnp.zeros((hk, QB), F32)
            for h in range(DSA_IDX_HEADS):
                sc = sc + w_t[h:h + 1, :] * jnp.maximum(r[:, h * QB:(h + 1) * QB], 0.0)
            sc = jnp.where(sc == 0.0, 0.0, sc)
            sc_ref[pl.ds(ks, hk), :] = jnp.where(_key_pos(ks, hk) <= _query_pos(q0, hk), sc, NEG_INF)
        return 0

    lax.fori_loop(0, nck, score_body, 0)

    n_beyond = (seq - nck * ck).astype(F32)
    n_part = 32

    def count(pred_fn):
        def cb(c, acc):
            ks = pl.multiple_of(c * ck, ck)
            hit = jnp.where(pred_fn(sc_ref[pl.ds(ks, ck), :], ks), 1.0, 0.0)
            return acc + jnp.sum(hit.reshape(ck // n_part, n_part, QB), axis=0)
        acc = lax.fori_loop(0, nck, cb, jnp.zeros((n_part, QB), F32))
        return jnp.sum(acc, axis=0, keepdims=True)

    def count_ge(thr):
        return count(lambda s, ks: s >= thr) + jnp.where(thr <= NEG_INF, n_beyond, 0.0)

    def search(n_static):
        def count_ge_static(thr):
            acc = jnp.zeros((n_part, QB), F32)
            for c in range(n_static):
                hit = jnp.where(sc_ref[c * ck:(c + 1) * ck, :] >= thr, 1.0, 0.0)
                acc = acc + jnp.sum(hit.reshape(ck // n_part, n_part, QB), axis=0)
            cnt = jnp.sum(acc, axis=0, keepdims=True)
            return cnt + jnp.where(thr <= NEG_INF, float(seq - n_static * ck), 0.0)

        def thr_body(it, carry):
            t_int, cnt_t = carry
            cand = t_int ^ lax.shift_left(jnp.int32(1), 31 - it)
            cnt = count_ge_static(_ordered_to_f32(cand))
            ok = cnt >= n_top
            return jnp.where(ok, cand, t_int), jnp.where(ok, cnt, cnt_t)

        t_int, cnt_t = lax.fori_loop(0, 32, thr_body, (jnp.full((1, QB), -2 ** 31, jnp.int32),
                                                       jnp.full((1, QB), float(seq), F32)))
        t_ref[...] = jnp.broadcast_to(t_int, t_ref.shape)
        cnt_ref[...] = jnp.broadcast_to(cnt_t, cnt_ref.shape)

    for n_static in range(1, seq // ck + 1):
        pl.when(nck == n_static)(functools.partial(search, n_static))
    t_int = t_ref[0:1, :]
    cnt_t = cnt_ref[0:1, :]
    thr = _ordered_to_f32(t_int)
    thr_up = _ordered_to_f32(t_int + 1)
    m_ref[...] = jnp.full(m_ref.shape, seq, jnp.int32)
    lo_ref[...] = jnp.broadcast_to(thr, lo_ref.shape)

    @pl.when(jnp.max(cnt_t) > n_top)
    def _():
        need = n_top - count_ge(thr_up)
        tie_beyond = thr <= NEG_INF

        def in_band(s):
            return (s >= thr) & (s < thr_up)

        def idx_body(it, m):
            cand = m + lax.shift_left(jnp.int32(1), (seq.bit_length() - 2) - it)
            below = count(lambda s, ks: in_band(s) & (_key_pos(ks, ck) < cand))
            below = below + jnp.where(tie_beyond, jnp.maximum(cand - nck * ck, 0).astype(F32), 0.0)
            return jnp.where(below < need, cand, m)

        m_idx = lax.fori_loop(0, seq.bit_length() - 1, idx_body, jnp.zeros((1, QB), jnp.int32))

        def fold(pick, reduce, init):
            def cb(c, acc):
                ks = pl.multiple_of(c * ck, ck)
                vals = pick(sc_ref[pl.ds(ks, ck), :], _key_pos(ks, ck).astype(F32))
                return reduce(acc, vals.reshape(ck // 8, 8, QB))
            return lax.fori_loop(0, nck, cb, jnp.full((8, QB), init, F32))

        best = jnp.max(fold(lambda s, kp: jnp.where(in_band(s), s, -jnp.inf),
                            lambda a, v: jnp.maximum(a, jnp.max(v, axis=0)), -jnp.inf), axis=0, keepdims=True)
        first = jnp.min(fold(lambda s, kp: jnp.where(in_band(s) & (s >= best), kp, float(seq)),
                             lambda a, v: jnp.minimum(a, jnp.min(v, axis=0)), float(seq)), axis=0, keepdims=True)
        single = need == 1.0
        m_ref[...] = jnp.broadcast_to(jnp.where(single, first.astype(jnp.int32), m_idx), m_ref.shape)
        lo_ref[...] = jnp.broadcast_to(jnp.where(single, best, thr), lo_ref.shape)

    m_idx = m_ref[0:1, :]
    tie_lo = lo_ref[0:1, :]

    tpos = _query_pos(q0, hk)

    def attn_scores(c):
        return tuple(_dot_nt(k_ref[0, pl.ds(ks, hk), :], q_all)
                     for ks in halves(c))

    def attn_consume(c, ss, carry):
        ms, acc = carry
        for ks, s_all in zip(halves(c), ss):
            kp = _key_pos(ks, hk)
            s_idx = sc_ref[pl.ds(ks, hk), :]
            bound = jnp.where(kp <= m_idx, tie_lo, thr_up)
            mask = s_idx >= jnp.where(kp <= tpos, bound, jnp.inf)
            ms, alpha, p = _softmax_step_t(ms, s_all, mask)
            acc = alpha * acc + _dot(_with_ones(vt_ref[0, :, pl.ds(ks, hk)]), p)
        return tuple(ms), acc

    _, acc = _chunk_loop(nck, attn_scores, attn_consume,
                             (tuple(_m_init() for _ in range(HEADS)), _acc_init(HEADS)))
    _store_heads_t(o_ref, [_finish_t(acc, h) for h in range(HEADS)])


def _dsa(q, k, vt, qi, ki, misc):
    B, H, S, _ = q.shape
    n_top = min(DSA_TOPK, S // 4)
    return pl.pallas_call(
        functools.partial(_dsa_kernel, seq=S, n_top=n_top),
        out_shape=jax.ShapeDtypeStruct((B, S, D_GROUP), F32),
        grid=(B, S // QB),
        in_specs=[pl.BlockSpec((1, H, QB, 64), lambda b, i: (b, 0, i, 0)),
                  pl.BlockSpec((1, S, 64), lambda b, i: (b, 0, 0)),
                  pl.BlockSpec((1, HEAD_DIM, S), lambda b, i: (b, 0, 0)),
                  pl.BlockSpec((1, DSA_IDX_HEADS, QB, 128), lambda b, i: (b, 0, i, 0)),
                  pl.BlockSpec((1, S, 128), lambda b, i: (b, 0, 0)),
                  pl.BlockSpec((1, QB, 128), lambda b, i: (b, i, 0))],
        out_specs=pl.BlockSpec((1, QB, D_GROUP), lambda b, i: (b, i, 0)),
        scratch_shapes=[pltpu.VMEM((S, QB), F32), pltpu.VMEM((8, QB), jnp.int32),
                        pltpu.VMEM((8, QB), jnp.int32), pltpu.VMEM((8, QB), F32), pltpu.VMEM((8, QB), F32)],
        compiler_params=_params("parallel", "arbitrary"),
        name="dsa_attn",
    )(q, k, vt, qi, ki, misc)


def _gelu_tanh(x):
    return 0.5 * x * (1.0 + jnp.tanh(np.sqrt(2.0 / np.pi) * (x + 0.044715 * (x * x * x))))


def _nsa_cmp_kernel(kg_ref, vg_ref, pos_ref, wk1_ref, wk2_ref, wv1_ref, wv2_ref, kc_ref, vct_ref):
    def compress(g_ref, w1_ref, w2_ref):
        top = _dot((g_ref[0] + pos_ref[0:1, :]).astype(BF16), w1_ref[0])
        bot = _dot((g_ref[0] + pos_ref[1:2, :]).astype(BF16), w1_ref[1])
        pre = top + pltpu.roll(bot, bot.shape[0] - 1, 0)
        return _dot(_gelu_tanh(pre).astype(BF16), w2_ref[...])

    kc_ref[0] = compress(kg_ref, wk1_ref, wk2_ref).astype(BF16)
    vct_ref[0] = compress(vg_ref, wv1_ref, wv2_ref).T[:HEAD_DIM].astype(BF16)


def _nsa_compress(kcmp, vcmp, pos, wk1, wk2, wv1, wv2):
    B, S, _ = kcmp.shape
    ng = S // NSA_CMP_STRIDE
    gw = NSA_CMP_STRIDE * HEAD_DIM
    kg = kcmp.reshape(B, ng, gw)
    vg = vcmp.reshape(B, ng, gw)

    def const(a):
        return pl.BlockSpec(a.shape, lambda b: (0,) * a.ndim)

    wv2p = jnp.concatenate([wv2, jnp.zeros((HEAD_DIM, LANES - HEAD_DIM), wv2.dtype)], axis=1)
    ins = [kg, vg, pos.reshape(2, gw), wk1.reshape(2, gw, HEAD_DIM).astype(BF16), wk2.astype(BF16),
           wv1.reshape(2, gw, HEAD_DIM).astype(BF16), wv2p.astype(BF16)]
    blk = pl.BlockSpec((1, ng, gw), lambda b: (b, 0, 0))
    return pl.pallas_call(
        _nsa_cmp_kernel,
        out_shape=[jax.ShapeDtypeStruct((B, ng, HEAD_DIM), BF16), jax.ShapeDtypeStruct((B, HEAD_DIM, ng), BF16)],
        grid=(B,),
        in_specs=[blk, blk] + [const(a) for a in ins[2:]],
        out_specs=[pl.BlockSpec((1, ng, HEAD_DIM), lambda b: (b, 0, 0)),
                   pl.BlockSpec((1, HEAD_DIM, ng), lambda b: (b, 0, 0))],
        compiler_params=_params("parallel"),
        name="nsa_compress",
    )(*ins)


def _nsa_kernel(q_ref, kc_ref, vct_ref, ks_ref, vst_ref, kw_ref, vwt_ref, misc_ref, ovl_ref, o_ref,
                *, seq):
    i = pl.program_id(1)
    q0 = i * QB
    n_slc = seq // NSA_SEL_LEN
    n_cmp = kc_ref.shape[1]
    q_all = q_ref[0].reshape(HEADS * QB, HEAD_DIM)

    last_tok = lax.broadcasted_iota(jnp.int32, (n_cmp, QB), 0) * NSA_CMP_STRIDE + (NSA_CMP_LEN - 1)
    cmask = last_tok <= _query_pos(q0, n_cmp)
    s_cmp = _dot_nt(kc_ref[0], q_all)
    ws = jnp.maximum(q0 - NSA_WINDOW, 0)
    win_parts = [(pl.multiple_of(ws + off, QB), n) for off, n in NSA_WIN_PARTS]
    s_wins = [_dot_nt(kw_ref[0, pl.ds(start, n), :], q_all) for start, n in win_parts]
    ps = []
    p_sum = jnp.zeros((n_cmp, QB), F32)
    for h in range(HEADS):
        s = jnp.where(cmask, s_cmp[:, h * QB:(h + 1) * QB], NEG_INF)
        e = jnp.where(cmask, jnp.exp2(s - jnp.max(s, axis=0, keepdims=True)), 0.0)
        p = e / jnp.maximum(jnp.sum(e, axis=0, keepdims=True), 1e-30)
        p_sum = p_sum + p
        ps.append(p.astype(BF16))
    o_cmp = _dot(vct_ref[0], jnp.concatenate(ps, axis=1))

    hi, lo = _split_bf16(p_sum)
    imp = _dot(ovl_ref[...], jnp.concatenate([hi, lo], axis=0))
    jb = lax.broadcasted_iota(jnp.int32, (n_slc, 1), 0)
    cur = (q0 + lax.broadcasted_iota(jnp.int32, (1, QB), 1)) >> SEL_SHIFT
    forced = (jb < NSA_N_INIT) | ((jb <= cur) & (jb > cur - NSA_N_LOCAL))
    imp = jnp.where(forced, FORCE_SCORE, jnp.where(jb <= cur, imp, NEG_INF))
    sub = 8
    ranks = []
    for g in range(n_slc // sub):
        blk = imp[g * sub:(g + 1) * sub]
        jg = jb[g * sub:(g + 1) * sub]
        rank_g = jnp.zeros((sub, QB), F32)
        for r in range(n_slc):
            row = imp[r:r + 1, :]
            ge = jnp.where(row >= blk, 1.0, 0.0)
            gt = jnp.where(row > blk, 1.0, 0.0)
            if g * sub > r:
                rank_g = rank_g + ge
            elif (g + 1) * sub - 1 <= r:
                rank_g = rank_g + gt
            else:
                rank_g = rank_g + jnp.where(jg > r, ge, gt)
        ranks.append(rank_g)
    rank = jnp.concatenate(ranks, axis=0)
    sel_t = jnp.where(rank < min(NSA_N_SEL, n_slc), 1.0, 0.0)
    sel_t = jnp.concatenate([sel_t, jnp.zeros((LANES - n_slc, QB), F32)], axis=0).astype(BF16)

    ms_win, acc_win = [_m_init() for _ in range(HEADS)], _acc_init(HEADS)
    for (start, n), s_win in zip(win_parts, s_wins):
        dist = _query_pos(q0, n) - _key_pos(start, n)
        wmask = (dist >= 0) & (dist < NSA_WINDOW)
        ms_win, alpha, p_win = _softmax_step_t(ms_win, s_win, wmask)
        acc_win = alpha * acc_win + _dot(_with_ones(vwt_ref[0, :, pl.ds(start, n)]), p_win)

    ck = DSA_CK
    hk = ck // 2
    tpos = _query_pos(q0, hk)

    n_slc_chunks = _n_chunks(q0, QB, ck)

    def slc_starts(c):
        return [pl.multiple_of(c * ck + j * hk, hk) for j in range(2)]

    def slc_scores(c):
        out = []
        for ks in slc_starts(c):
            blk = (ks + lax.broadcasted_iota(jnp.int32, (hk, LANES), 0)) >> SEL_SHIFT
            expand = jnp.where(blk == lax.broadcasted_iota(jnp.int32, (hk, LANES), 1), 1.0, 0.0)
            out.append(_dot(expand.astype(BF16), sel_t))
        return tuple(out) + tuple(_dot_nt(ks_ref[0, pl.ds(ks, hk), :], q_all) for ks in slc_starts(c))

    def slc_consume(c, ss, carry):
        ms, acc = carry
        for ks, chosen, s_all in zip(slc_starts(c), ss[:2], ss[2:]):
            mask = chosen > jnp.where(_key_pos(ks, hk) <= tpos, 0.5, 2.0)
            ms, alpha, p = _softmax_step_t(ms, s_all, mask)
            acc = alpha * acc + _dot(_with_ones(vst_ref[0, :, pl.ds(ks, hk)]), p)
        return tuple(ms), acc

    _, acc_slc = _chunk_loop(n_slc_chunks, slc_scores, slc_consume,
                                 (tuple(_m_init() for _ in range(HEADS)), _acc_init(HEADS)))

    gate_rows = 16
    gate = jax.nn.sigmoid(misc_ref[0].T[MISC_GATE:MISC_GATE + gate_rows, :])
    outs = []
    for h in range(HEADS):
        outs.append(gate[h:h + 1] * o_cmp[:, h * QB:(h + 1) * QB]
                    + gate[HEADS + h:HEADS + h + 1] * _finish_t(acc_slc, h)
                    + gate[2 * HEADS + h:2 * HEADS + h + 1] * _finish_t(acc_win, h))
    _store_heads_t(o_ref, outs)


def _nsa_overlap(seq):
    ng = seq // NSA_CMP_STRIDE
    n_slc = seq // NSA_SEL_LEN
    n = np.arange(ng)
    first, last = n * NSA_CMP_STRIDE, n * NSA_CMP_STRIDE + NSA_CMP_LEN - 1
    start = np.arange(n_slc) * NSA_SEL_LEN
    ovl = ((first[None, :] <= start[:, None] + NSA_SEL_LEN - 1) & (last[None, :] >= start[:, None]))
    ovl = ovl & (last[None, :] < seq)
    ovl = ovl.astype(np.float32)
    return jnp.asarray(np.concatenate([ovl, ovl], 1), BF16)


def _nsa(q, kc, vct, k_slc, vt_slc, k_win, vt_win, misc):
    B, H, S, _ = q.shape
    ng = kc.shape[1]
    ovl = _nsa_overlap(S)
    keys = pl.BlockSpec((1, S, HEAD_DIM), lambda b, i: (b, 0, 0))
    vals_t = pl.BlockSpec((1, HEAD_DIM, S), lambda b, i: (b, 0, 0))
    return pl.pallas_call(
        functools.partial(_nsa_kernel, seq=S),
        out_shape=jax.ShapeDtypeStruct((B, S, D_GROUP), F32),
        grid=(B, S // QB),
        in_specs=[pl.BlockSpec((1, H, QB, 64), lambda b, i: (b, 0, i, 0)),
                  pl.BlockSpec((1, ng, HEAD_DIM), lambda b, i: (b, 0, 0)),
                  pl.BlockSpec((1, HEAD_DIM, ng), lambda b, i: (b, 0, 0)),
                  keys, vals_t, keys, vals_t,
                  pl.BlockSpec((1, QB, 128), lambda b, i: (b, i, 0)),
                  pl.BlockSpec(ovl.shape, lambda b, i: (0, 0))],
        out_specs=pl.BlockSpec((1, QB, D_GROUP), lambda b, i: (b, i, 0)),
        compiler_params=_params("parallel", "arbitrary"),
        name="nsa_attn",
    )(q, kc, vct, k_slc, vt_slc, k_win, vt_win, misc, ovl)


def _outproj_kernel(x_ref, ya_ref, yb_ref, yc_ref, yd_ref, gg_ref, wo_ref, g_ref, b_ref, o_ref):
    acc = None
    for gi, y_ref in enumerate((ya_ref, yb_ref, yc_ref, yd_ref)):
        lo, hi = gi * D_GROUP, (gi + 1) * D_GROUP
        yn = _rms_norm(y_ref[...], gg_ref[:, lo:hi]).astype(BF16)
        part = _dot(yn, wo_ref[lo:hi, :])
        acc = part if acc is None else acc + part
    o_ref[...] = _layer_norm(DN_ALPHA * x_ref[...] + acc, g_ref[...], b_ref[...])


def _outproj(x2, ys, gg, wo, g, b, *, tm):
    n, d = x2.shape
    row = lambda w: pl.BlockSpec((tm, w), lambda i: (i, 0))
    const = lambda a: pl.BlockSpec(a.shape, lambda i: (0, 0))
    return pl.pallas_call(
        _outproj_kernel,
        out_shape=jax.ShapeDtypeStruct((n, d), F32),
        grid=(n // tm,),
        in_specs=[row(d)] + [row(D_GROUP)] * 4 + [const(gg), const(wo), const(g), const(b)],
        out_specs=row(d),
        compiler_params=_params("parallel"),
        name="out_proj",
    )(x2, *ys, gg, wo, g, b)


def _mixer(x, w_in, q_norm, w_uq, kv_norm, w_ukv, cmp_pos, wk1, wk2, wv1, wv2, tabs, place, maps):
    B, S, D = x.shape
    wm = _gather_cols(w_in, maps).astype(BF16)
    wt = jnp.concatenate([wm[:, slice(*_MAIN_OFF['slab_v'])], wm[:, slice(*_MAIN_OFF['sb_v'])]], axis=1).T
    (mq, mk, mvt, nq, dq, sbq, sbk, sbvt, kcmp, kslc, kwin, dk, vcmp, vslct, vwint, dvt,
     qi, ki, misc) = _inproj(x, wm, wt, tabs, q_norm[None, :], kv_norm[None, :],
                             _mla_weights(w_uq, w_ukv), place, tm=min(512, S))
    y_a = _mla(mq, mk, mvt)
    kc, vct = _nsa_compress(kcmp, vcmp, cmp_pos, wk1, wk2, wv1, wv2)
    y_b = _nsa(nq, kc, vct, kslc, vslct, kwin, vwint, misc)
    y_c = _dsa(dq, dk, dvt, qi, ki, misc)
    y_d = _sb(sbq, sbk, sbvt)
    return y_a, y_b, y_c, y_d


def kernel(x, ln1_g, ln1_b, ffn1_w1, ffn1_w3, ffn1_w2, w_in, mla_q_norm, mla_w_uq, mla_kv_norm,
           mla_w_ukv, nsa_cmp_pos, nsa_cmp_wk1, nsa_cmp_wk2, nsa_cmp_wv1, nsa_cmp_wv2, group_norm_g,
           w_out, ln2_g, ln2_b, ffn2_w1, ffn2_w3, ffn2_w2, ln3_g, ln3_b):
    B, S, D = x.shape
    n = B * S
    tabs = _rope_tables(S)
    place = _placement_constants()
    maps = _column_maps()
    tm_ffn = min(512, n)
    x2 = x.reshape(n, D)
    for l in range(DEPTH):
        x2 = _ffn_ln(x2, ffn1_w1[l].astype(BF16), ffn1_w3[l].astype(BF16), ffn1_w2[l].astype(BF16),
                     ln1_g[l][None, :], ln1_b[l][None, :], tm=tm_ffn, tf=256)
        ys = _mixer(x2.reshape(B, S, D), w_in[l], mla_q_norm[l], mla_w_uq[l], mla_kv_norm[l],
                    mla_w_ukv[l], nsa_cmp_pos[l], nsa_cmp_wk1[l], nsa_cmp_wk2[l], nsa_cmp_wv1[l],
                    nsa_cmp_wv2[l], tabs, place, maps)
        x2 = _outproj(x2, [y.reshape(n, D_GROUP) for y in ys], group_norm_g[l][None, :],
                      w_out[l].astype(BF16), ln2_g[l][None, :], ln2_b[l][None, :], tm=min(1024, n))
        x2 = _ffn_ln(x2, ffn2_w1[l].astype(BF16), ffn2_w3[l].astype(BF16), ffn2_w2[l].astype(BF16),
                     ln3_g[l][None, :], ln3_b[l][None, :], tm=tm_ffn, tf=256)
    return x2.reshape(B, S, D)
```

```python
import functools

import numpy as np
import jax
import jax.numpy as jnp
from jax import lax
from jax.experimental import pallas as pl
from jax.experimental.pallas import tpu as pltpu

F32 = jnp.float32
BF16 = jnp.bfloat16

D_MODEL = 1024
DEPTH = 2
HEADS = 4
HEAD_DIM = 64
D_GROUP = HEADS * HEAD_DIM
N_GROUPS = 4
D_FF = 2816
ROPE_THETA = 10000.0
MLA_Q_RANK = 256
MLA_KV_RANK = 128
MLA_D_NOPE = 64
MLA_D_ROPE = 32
MLA_D_V = 64
NSA_CMP_LEN = 32
NSA_CMP_STRIDE = 16
NSA_SEL_LEN = 64
NSA_N_SEL = 8
NSA_N_INIT = 1
NSA_N_LOCAL = 2
NSA_WINDOW = 512
SEL_SHIFT = NSA_SEL_LEN.bit_length() - 1
DSA_TOPK = 256
DSA_IDX_HEADS = 8
DSA_IDX_DIM = 32
DN_ALPHA = (2.0 * DEPTH) ** 0.25
LN_EPS = 1e-5
RMS_EPS = 1e-6
NEG_INF = -1e30
FORCE_SCORE = 1e4
LOG2_E = 1.4426950408889634

LANES = 128
QB = 128
CK = 256
DSA_CK = 512
NSA_WIN_PARTS = ((0, NSA_WINDOW + QB),)
VMEM_LIMIT = 48 * 1024 * 1024

MISC_KROPE = 0
MISC_IDXK = 32
MISC_IDXW = 64
MISC_GATE = 72

NT_DIMS = (((1,), (1,)), ((), ()))


def _dot(a, b):
    return jnp.dot(a, b, preferred_element_type=F32)


def _dot_nt(a, b):
    return lax.dot_general(a, b, NT_DIMS, preferred_element_type=F32)


def _split_bf16(x):
    hi = x.astype(BF16)
    lo = (x - hi.astype(F32)).astype(BF16)
    return hi, lo


def _layer_norm(y, g, b):
    mu = jnp.mean(y, -1, keepdims=True)
    d = y - mu
    var = jnp.mean(d * d, -1, keepdims=True)
    return d * lax.rsqrt(var + LN_EPS) * g + b


def _rms_norm(y, g):
    return y * lax.rsqrt(jnp.mean(y * y, -1, keepdims=True) + RMS_EPS) * g


def _params(*sem):
    return pltpu.CompilerParams(dimension_semantics=sem, vmem_limit_bytes=VMEM_LIMIT)


def _ffn_kernel(x_ref, w1_ref, w3_ref, w2_ref, g_ref, b_ref, o_ref, a_ref, *, tf):
    xb = x_ref[...].astype(BF16)
    n_slabs = w1_ref.shape[1] // tf

    def up(j):
        return _dot(xb, w1_ref[:, j * tf:(j + 1) * tf]), _dot(xb, w3_ref[:, j * tf:(j + 1) * tf])

    hu = up(0)
    for j in range(n_slabs):
        nxt = up(j + 1) if j + 1 < n_slabs else None
        h, u = hu
        a_ref[:, j * tf:(j + 1) * tf] = (h * jax.nn.sigmoid(h) * u).astype(BF16)
        hu = nxt
    y = DN_ALPHA * x_ref[...] + 0.5 * _dot(a_ref[...], w2_ref[...])
    o_ref[...] = _layer_norm(y, g_ref[...], b_ref[...])


def _ffn_ln(x2, w1, w3, w2, g, b, *, tm, tf):
    n, d = x2.shape
    dff = w1.shape[1]
    const = lambda a: pl.BlockSpec(a.shape, lambda i: (0, 0))
    return pl.pallas_call(
        functools.partial(_ffn_kernel, tf=tf),
        out_shape=jax.ShapeDtypeStruct((n, d), F32),
        grid=(n // tm,),
        in_specs=[pl.BlockSpec((tm, d), lambda i: (i, 0)), const(w1), const(w3), const(w2), const(g), const(b)],
        out_specs=pl.BlockSpec((tm, d), lambda i: (i, 0)),
        scratch_shapes=[pltpu.VMEM((tm, dff), BF16)],
        compiler_params=_params("parallel"),
        name="ffn_ln",
    )(x2, w1, w3, w2, g, b)


_MAIN_ORDER = (
    ('mla_cq', 256), ('nsa_q', 256), ('dsa_q', 256), ('idx_q', 256),
    ('sb_q', 256), ('sb_k', 256), ('sb_v', 256),
    ('slab_k', 256),
    ('slab_v', 256),
    ('mla_ckv', 128),
    ('misc', 128),
)
_MAIN_OFF = {}
_o = 0
for _n, _w in _MAIN_ORDER:
    _MAIN_OFF[_n] = (_o, _o + _w)
    _o += _w
N_MAIN = _o

_IN_SPLITS = (
    ('mla_cq', MLA_Q_RANK), ('mla_ckv', MLA_KV_RANK), ('mla_krope', MLA_D_ROPE),
    ('nsa_q', D_GROUP), ('nsa_k_cmp', HEAD_DIM), ('nsa_v_cmp', HEAD_DIM),
    ('nsa_k_slc', HEAD_DIM), ('nsa_v_slc', HEAD_DIM), ('nsa_k_win', HEAD_DIM),
    ('nsa_v_win', HEAD_DIM), ('nsa_gate', 3 * HEADS),
    ('dsa_q', D_GROUP), ('dsa_k', HEAD_DIM), ('dsa_v', HEAD_DIM),
    ('idx_q', DSA_IDX_HEADS * DSA_IDX_DIM), ('idx_k', DSA_IDX_DIM), ('idx_w', DSA_IDX_HEADS),
    ('sb_q', D_GROUP), ('sb_k', D_GROUP), ('sb_v', D_GROUP),
)
_SRC = {}
_o = 0
for _n, _w in _IN_SPLITS:
    _SRC[_n] = np.arange(_o, _o + _w)
    _o += _w
D_IN = _o


def _swap_halves(width, dim):
    idx = np.arange(width)
    return (idx // dim) * dim + (idx % dim + dim // 2) % dim


def _column_maps():
    pieces = {
        'mla_cq': [_SRC['mla_cq']], 'nsa_q': [_SRC['nsa_q']], 'dsa_q': [_SRC['dsa_q']],
        'idx_q': [_SRC['idx_q']], 'sb_q': [_SRC['sb_q']], 'sb_k': [_SRC['sb_k']], 'sb_v': [_SRC['sb_v']],
        'slab_k': [_SRC['nsa_k_cmp'], _SRC['nsa_k_slc'], _SRC['nsa_k_win'], _SRC['dsa_k']],
        'slab_v': [_SRC['nsa_v_cmp'], _SRC['nsa_v_slc'], _SRC['nsa_v_win'], _SRC['dsa_v']],
        'mla_ckv': [_SRC['mla_ckv']],
        'misc': [_SRC['mla_krope'], _SRC['idx_k'], _SRC['idx_w'], _SRC['nsa_gate'],
                 -np.ones(LANES - MISC_GATE - 3 * HEADS, np.int64)],
    }
    return np.concatenate([np.concatenate(pieces[n]) for n, _ in _MAIN_ORDER])


def _gather_cols(w, cols):
    cols = [int(c) for c in cols]
    parts, i = [], 0
    while i < len(cols):
        j = i + 1
        if cols[i] < 0:
            while j < len(cols) and cols[j] < 0:
                j += 1
            parts.append(jnp.zeros((w.shape[0], j - i), w.dtype))
        else:
            while j < len(cols) and cols[j] == cols[j - 1] + 1:
                j += 1
            parts.append(w[:, cols[i]:cols[i] + (j - i)])
        i = j
    return jnp.concatenate(parts, axis=1)


def _rope_tables(seq):
    def base(dim):
        inv = ROPE_THETA ** (-jnp.arange(0, dim, 2, dtype=F32) / dim)
        ang = jnp.arange(seq, dtype=F32)[:, None] * inv[None, :]
        c, s = jnp.cos(ang), jnp.sin(ang)
        return jnp.concatenate([c, c], -1), jnp.concatenate([-s, s], -1)

    c64, s64 = base(HEAD_DIM)
    c32, s32 = base(DSA_IDX_DIM)
    t64 = (jnp.tile(c64, (1, 4)), jnp.tile(s64, (1, 4)))
    t32 = (jnp.tile(c32, (1, 8)), jnp.tile(s32, (1, 8)))
    ones = jnp.ones((seq, LANES - 64), F32)
    td = (jnp.concatenate([c32, c32, ones], -1), jnp.concatenate([s32, s32, 0.0 * ones], -1))
    scale = (MLA_D_NOPE + MLA_D_ROPE) ** -0.5 * LOG2_E
    cq = jnp.concatenate([jnp.ones((seq, 64), F32), c32, jnp.zeros((seq, 32), F32)], -1) * scale
    sq = jnp.concatenate([jnp.zeros((seq, 64), F32), s32, jnp.zeros((seq, 32), F32)], -1) * scale
    tq = (jnp.tile(cq, (1, 4)), jnp.tile(sq, (1, 4)))
    return t64, t32, td, tq


def _placement_constants():
    p_hi = np.zeros((256, 1024), np.float32)
    p_lo = np.zeros((256, 1024), np.float32)
    for h in range(DSA_IDX_HEADS):
        for d in range(DSA_IDX_DIM):
            p_hi[h * 32 + d, h * 128 + d] = 1
            p_hi[h * 32 + d, h * 128 + 64 + d] = 1
            p_lo[h * 32 + d, h * 128 + 32 + d] = 1
    pk_hi = np.zeros((128, 128), np.float32)
    pk_lo = np.zeros((128, 128), np.float32)
    for d in range(DSA_IDX_DIM):
        pk_hi[MISC_IDXK + d, d] = 1
        pk_hi[MISC_IDXK + d, 32 + d] = 1
        pk_lo[MISC_IDXK + d, 64 + d] = 1
    pk_pe = np.zeros((128, 512), np.float32)
    for h in range(HEADS):
        for d in range(MLA_D_ROPE):
            pk_pe[MISC_KROPE + d, h * 128 + 64 + d] = 1
    return tuple(jnp.asarray(a, BF16) for a in (p_hi, p_lo, pk_hi, pk_lo, pk_pe))


def _mla_weights(w_uq, w_ukv):
    dq = MLA_D_NOPE + MLA_D_ROPE
    cols_q = -np.ones(512, np.int64)
    cols_qr = -np.ones(512, np.int64)
    cols_k = -np.ones(512, np.int64)
    cols_v = np.zeros(256, np.int64)
    for h in range(HEADS):
        cols_q[h * 128:h * 128 + dq] = h * dq + np.arange(dq)
        pe = h * dq + MLA_D_NOPE + np.arange(MLA_D_ROPE)
        cols_qr[h * 128 + 64:h * 128 + 96] = pe[_swap_halves(32, 32)]
        cols_k[h * 128:h * 128 + 64] = h * 128 + np.arange(64)
        cols_v[h * 64:(h + 1) * 64] = h * 128 + 64 + np.arange(64)
    return (_gather_cols(w_uq, cols_q).astype(BF16), _gather_cols(w_uq, cols_qr).astype(BF16),
            _gather_cols(w_ukv, cols_k).astype(BF16), _gather_cols(w_ukv, cols_v).T.astype(BF16))


def _rotate_half(z, dim):
    n = z.shape[1]
    half = dim // 2
    lane = lax.broadcasted_iota(jnp.int32, z.shape, 1)
    in_first_half = (lane & (dim - 1)) < half
    return jnp.where(in_first_half, pltpu.roll(z, n - half, 1), pltpu.roll(z, half, 1))


def _inproj_kernel(x_ref, wm_ref, wt_ref, c64_ref, s64_ref, c32_ref, s32_ref, cd_ref, sd_ref,
                   cq_ref, sq_ref, qg_ref, kvg_ref, wuq_ref, wuqr_ref, wuk_ref, wuvt_ref,
                   phi_ref, plo_ref, pkhi_ref, pklo_ref, pkpe_ref,
                   mq_ref, mk_ref, mvt_ref, nq_ref, dq_ref, sbq_ref, sbk_ref, sbvt_ref,
                   kcmp_ref, kslc_ref, kwin_ref, dk_ref, vcmp_ref, vslct_ref, vwint_ref, dvt_ref,
                   qi_ref, ki_ref, misc_ref):
    xb = x_ref[0].astype(BF16)

    def main(name):
        a, b = _MAIN_OFF[name]
        return _dot(xb, wm_ref[:, a:b])

    def roped(name, c_ref, s_ref, dim):
        z = main(name)
        return z * c_ref[...] + _rotate_half(z, dim) * s_ref[...]

    def store_heads(ref, val, width):
        for h in range(HEADS):
            ref[0, h] = val[:, h * width:(h + 1) * width].astype(ref.dtype)

    cqn = _rms_norm(main('mla_cq'), qg_ref[...]).astype(BF16)
    q = _dot(cqn, wuq_ref[...]) * cq_ref[...] + _dot(cqn, wuqr_ref[...]) * sq_ref[...]
    store_heads(mq_ref, q, 128)
    ckv_misc = _dot(xb, wm_ref[:, _MAIN_OFF['mla_ckv'][0]:_MAIN_OFF['misc'][1]])
    z_misc = ckv_misc[:, LANES:]
    misc = z_misc * cd_ref[...] + _rotate_half(z_misc, DSA_IDX_DIM) * sd_ref[...]
    misc_ref[0] = misc
    m_hi, m_lo = _split_bf16(misc)
    ckvn = _rms_norm(ckv_misc[:, :LANES], kvg_ref[...]).astype(BF16)
    k = _dot(ckvn, wuk_ref[...]) + _dot(m_hi, pkpe_ref[...])
    store_heads(mk_ref, k, 128)
    mvt = _dot_nt(wuvt_ref[...], ckvn)
    for h in range(HEADS):
        mvt_ref[0, h] = mvt[h * 64:(h + 1) * 64, :].astype(BF16)
    qscale = HEAD_DIM ** -0.5 * LOG2_E
    store_heads(nq_ref, roped('nsa_q', c64_ref, s64_ref, HEAD_DIM) * qscale, 64)
    store_heads(dq_ref, roped('dsa_q', c64_ref, s64_ref, HEAD_DIM) * qscale, 64)
    store_heads(sbq_ref, main('sb_q') * qscale, 64)
    store_heads(sbk_ref, main('sb_k'), 64)
    sk = roped('slab_k', c64_ref, s64_ref, HEAD_DIM)
    for j, ref in enumerate((kcmp_ref, kslc_ref, kwin_ref, dk_ref)):
        ref[0] = sk[:, j * 64:(j + 1) * 64].astype(ref.dtype)
    vt = _dot_nt(wt_ref[...], xb)
    vcmp_ref[0] = vt[0:LANES, :].T[:, 0:HEAD_DIM]
    for j, ref in ((1, vslct_ref), (2, vwint_ref), (3, dvt_ref)):
        ref[0] = vt[j * 64:(j + 1) * 64, :].astype(BF16)
    for h in range(HEADS):
        sbvt_ref[0, h] = vt[256 + h * 64:256 + (h + 1) * 64, :].astype(BF16)
    qi_hi, qi_lo = _split_bf16(roped('idx_q', c32_ref, s32_ref, DSA_IDX_DIM))
    qi = _dot(qi_hi, phi_ref[...]) + _dot(qi_lo, plo_ref[...])
    for h in range(DSA_IDX_HEADS):
        qi_ref[0, h] = qi[:, h * 128:(h + 1) * 128].astype(BF16)
    ki_ref[0] = (_dot(m_hi, pkhi_ref[...]) + _dot(m_lo, pklo_ref[...])).astype(BF16)


def _inproj(x, wm, wt, tabs, qg, kvg, mla_w, place, *, tm):
    B, S, D = x.shape
    (c64, s64), (c32, s32), (cd, sd), (cq, sq) = tabs
    wuq, wuqr, wuk, wuvt = mla_w

    def const(a):
        return pl.BlockSpec(a.shape, lambda b, s: (0,) * a.ndim)

    def tab(a):
        return pl.BlockSpec((tm, a.shape[1]), lambda b, s: (s, 0))

    def heads_out(w, dt=BF16):
        return (jax.ShapeDtypeStruct((B, HEADS, S, w), dt),
                pl.BlockSpec((1, HEADS, tm, w), lambda b, s: (b, 0, s, 0)))

    def flat_out(w, dt):
        return (jax.ShapeDtypeStruct((B, S, w), dt), pl.BlockSpec((1, tm, w), lambda b, s: (b, s, 0)))

    heads_t = (jax.ShapeDtypeStruct((B, HEADS, HEAD_DIM, S), BF16),
               pl.BlockSpec((1, HEADS, HEAD_DIM, tm), lambda b, s: (b, 0, 0, s)))
    flat_t = (jax.ShapeDtypeStruct((B, HEAD_DIM, S), BF16),
              pl.BlockSpec((1, HEAD_DIM, tm), lambda b, s: (b, 0, s)))
    qi_out = (jax.ShapeDtypeStruct((B, DSA_IDX_HEADS, S, 128), BF16),
              pl.BlockSpec((1, DSA_IDX_HEADS, tm, 128), lambda b, s: (b, 0, s, 0)))
    outs = [heads_out(128), heads_out(128), heads_t,
            heads_out(64), heads_out(64),
            heads_out(64), heads_out(64), heads_t,
            flat_out(64, F32), flat_out(64, BF16), flat_out(64, BF16), flat_out(64, BF16),
            flat_out(64, F32), flat_t, flat_t, flat_t,
            qi_out, flat_out(128, BF16), flat_out(128, F32)]
    ins = [x, wm, wt, c64, s64, c32, s32, cd, sd, cq, sq, qg, kvg, wuq, wuqr, wuk, wuvt, *place]
    in_specs = [pl.BlockSpec((1, tm, D), lambda b, s: (b, s, 0)), const(wm), const(wt),
                tab(c64), tab(s64), tab(c32), tab(s32), tab(cd), tab(sd), tab(cq), tab(sq),
                const(qg), const(kvg), const(wuq), const(wuqr), const(wuk), const(wuvt),
                *[const(p) for p in place]]
    return pl.pallas_call(
        _inproj_kernel,
        out_shape=[o[0] for o in outs],
        grid=(B, S // tm),
        in_specs=in_specs,
        out_specs=[o[1] for o in outs],
        compiler_params=_params("parallel", "arbitrary"),
        name="in_proj",
    )(*ins)


V_ROWS = HEAD_DIM


def _acc_init(n_heads, qb=QB):
    return jnp.zeros((2 * V_ROWS, n_heads * qb), F32)


def _m_init(qb=QB):
    return jnp.full((1, qb), NEG_INF, F32)


def _with_ones(vt):
    return jnp.concatenate([vt, jnp.ones((V_ROWS, vt.shape[1]), vt.dtype)], axis=0)


def _softmax_step_t(ms, s_all, mask):
    new_ms, alphas, es = [], [], []
    qb = mask.shape[1]
    for h, m in enumerate(ms):
        s = jnp.where(mask, s_all[:, h * qb:(h + 1) * qb], NEG_INF)
        m_new = jnp.maximum(m, jnp.max(s, axis=0, keepdims=True))
        alphas.append(jnp.exp2(m - m_new))
        es.append(jnp.where(mask, jnp.exp2(s - m_new), 0.0).astype(BF16))
        new_ms.append(m_new)
    return new_ms, jnp.concatenate(alphas, axis=1), jnp.concatenate(es, axis=1)


def _finish_t(acc, h, qb=QB):
    blk = acc[:, h * qb:(h + 1) * qb]
    return blk[:V_ROWS] / jnp.maximum(blk[V_ROWS:V_ROWS + 1], 1e-30)


def _store_heads_t(o_ref, outs_t):
    for p in range(HEADS // 2):
        pair = jnp.concatenate([outs_t[2 * p], outs_t[2 * p + 1]], axis=0)
        o_ref[0, :, p * 128:(p + 1) * 128] = pair.T


def _chunk_loop(n, scores, consume, init):
    return lax.fori_loop(0, n, lambda c, state: consume(c, scores(c), state), init)


def _n_chunks(q0, qb=QB, ck=CK):
    return (q0 + qb + ck - 1) // ck


def _key_pos(ks, n, qb=QB):
    return ks + lax.broadcasted_iota(jnp.int32, (n, qb), 0)


def _query_pos(q0, n, qb=QB):
    return q0 + lax.broadcasted_iota(jnp.int32, (n, qb), 1)


QB_WIDE = 256


def _mla_kernel(q_ref, k_ref, vt_ref, o_ref):
    qb = QB_WIDE
    q0 = pl.program_id(1) * qb
    tpos = _query_pos(q0, CK, qb)

    n = _n_chunks(q0, qb)

    def scores(c):
        ks = pl.multiple_of(c * CK, CK)
        return tuple(_dot_nt(k_ref[0, h, pl.ds(ks, CK), :], q_ref[0, h]) for h in range(HEADS))

    def consume(c, ss, carry):
        ms, accs = carry
        ks = pl.multiple_of(c * CK, CK)
        mask = _key_pos(ks, CK, qb) <= tpos
        steps = [_softmax_step_t([ms[h]], ss[h], mask) for h in range(HEADS)]
        new_accs = [steps[h][1] * accs[h] + _dot(_with_ones(vt_ref[0, h, :, pl.ds(ks, CK)]), steps[h][2])
                    for h in range(HEADS)]
        return tuple(st[0][0] for st in steps), tuple(new_accs)

    init = (tuple(_m_init(qb) for _ in range(HEADS)), tuple(_acc_init(1, qb) for _ in range(HEADS)))
    _, accs = _chunk_loop(n, scores, consume, init)
    _store_heads_t(o_ref, [_finish_t(accs[h], 0, qb) for h in range(HEADS)])


def _mla(q, k, vt):
    B, H, S, _ = q.shape
    qb = QB_WIDE
    return pl.pallas_call(
        _mla_kernel,
        out_shape=jax.ShapeDtypeStruct((B, S, D_GROUP), F32),
        grid=(B, S // qb),
        in_specs=[pl.BlockSpec((1, H, qb, 128), lambda b, i: (b, 0, i, 0)),
                  pl.BlockSpec((1, H, S, 128), lambda b, i: (b, 0, 0, 0)),
                  pl.BlockSpec((1, H, HEAD_DIM, S), lambda b, i: (b, 0, 0, 0))],
        out_specs=pl.BlockSpec((1, qb, D_GROUP), lambda b, i: (b, i, 0)),
        compiler_params=_params("parallel", "arbitrary"),
        name="mla_attn",
    )(q, k, vt)


SB_TOT_ROWS = 16


SB_CK = 128


def _sb_kernel(q_ref, k_ref, vt_ref, tri_ref, o_ref):
    qb, kc = QB_WIDE, SB_CK
    n_sub = 2
    step = n_sub * kc
    q0 = pl.program_id(1) * qb
    n_steps = (q0 + qb) // step
    tri = tri_ref[...]

    def key_start(r):
        return pl.multiple_of(q0 + qb - (r + 1) * step, step)

    def scores(r):
        ks = key_start(r)
        return tuple(_dot_nt(k_ref[0, h, pl.ds(ks, step), :], q_ref[0, h]) for h in range(HEADS))

    def consume(r, zs, carry, masked):
        runs, accs = carry
        ks = key_start(r)
        if masked:
            mask = _key_pos(ks, step, qb) < _query_pos(q0, step, qb)
        sps, parts = [], []
        for h in range(HEADS):
            sp = jnp.maximum(zs[h], 0.0) + jnp.log2(1.0 + jnp.exp2(-jnp.abs(zs[h])))
            sps.append(sp)
            parts.append(_split_bf16(jnp.where(mask, sp, 0.0) if masked else sp))
        stacked = jnp.concatenate(
            [jnp.concatenate([parts[h][j][b * kc:(b + 1) * kc] for j in range(2)], axis=0)
             for b in range(n_sub) for h in range(HEADS)], axis=1)
        res = _dot(tri, stacked)
        new_runs, new_accs = [], []
        for h in range(HEADS):
            run = runs[h]
            afters = [None] * n_sub
            for b in reversed(range(n_sub)):
                col = (b * HEADS + h) * qb
                afters[b] = res[:kc, col:col + qb] + run
                run = run + res[kc:kc + 1, col:col + qb]
            a = jnp.exp2(zs[h] - sps[h] + jnp.concatenate(afters, axis=0))
            if masked:
                a = jnp.where(mask, a, 0.0)
            new_accs.append(accs[h] + _dot(vt_ref[0, h, :, pl.ds(ks, step)], a.astype(BF16)))
            new_runs.append(run)
        return tuple(new_runs), tuple(new_accs)

    init = (tuple(jnp.zeros((1, qb), F32) for _ in range(HEADS)),
            tuple(jnp.zeros((HEAD_DIM, qb), F32) for _ in range(HEADS)))
    carry = consume(0, scores(0), init, True)
    _, accs = lax.fori_loop(1, n_steps, lambda r, c: consume(r, scores(r), c, False), carry)
    _store_heads_t(o_ref, accs)


def _sb(q, k, vt):
    B, H, S, _ = q.shape
    qb, kc = QB_WIDE, SB_CK
    u = np.triu(np.ones((kc, kc), np.float32), 1)
    u = np.concatenate([u, np.ones((SB_TOT_ROWS, kc), np.float32)], 0)
    tri = jnp.asarray(-np.concatenate([u, u], 1), BF16)
    return pl.pallas_call(
        _sb_kernel,
        out_shape=jax.ShapeDtypeStruct((B, S, D_GROUP), F32),
        grid=(B, S // qb),
        in_specs=[pl.BlockSpec((1, H, qb, 64), lambda b, i: (b, 0, i, 0)),
                  pl.BlockSpec((1, H, S, 64), lambda b, i: (b, 0, 0, 0)),
                  pl.BlockSpec((1, H, HEAD_DIM, S), lambda b, i: (b, 0, 0, 0)),
                  pl.BlockSpec(tri.shape, lambda b, i: (0, 0))],
        out_specs=pl.BlockSpec((1, qb, D_GROUP), lambda b, i: (b, i, 0)),
        compiler_params=_params("parallel", "arbitrary"),
        name="sb_attn",
    )(q, k, vt, tri)


def _ordered_to_f32(c):
    bits = c ^ ((c >> 31) & jnp.int32(0x7FFFFFFF))
    return pltpu.bitcast(bits, F32)


def _dsa_kernel(q_ref, k_ref, vt_ref, qi_ref, ki_ref, misc_ref, o_ref, sc_ref, m_ref, t_ref, cnt_ref,
                lo_ref, tie_ref, *, seq, n_top):
    q0 = pl.program_id(1) * QB
    ck = DSA_CK
    hk = ck // 2
    nck = _n_chunks(q0, QB, ck)
    w_t = misc_ref[0].T[MISC_IDXW:MISC_IDXW + DSA_IDX_HEADS, :] * (DSA_IDX_HEADS * DSA_IDX_DIM) ** -0.5
    qi_all = qi_ref[0].reshape(DSA_IDX_HEADS * QB, 128)
    q_all = q_ref[0].reshape(HEADS * QB, HEAD_DIM)

    def halves(c):
        ks = c * ck
        return [pl.multiple_of(ks + j * hk, hk) for j in range(2)]

    def score_body(c, _):
        starts = halves(c)
        rs = [_dot_nt(ki_ref[0, pl.ds(ks, hk), :], qi_all) for ks in starts]
        for ks, r in zip(starts, rs):
            sc = jnp.zeros((hk, QB), F32)
            for h in range(DSA_IDX_HEADS):
                sc = sc + w_t[h:h + 1, :] * jnp.maximum(r[:, h * QB:(h + 1) * QB], 0.0)
            sc = jnp.where(sc == 0.0, 0.0, sc)
            sc_ref[pl.ds(ks, hk), :] = jnp.where(_key_pos(ks, hk) <= _query_pos(q0, hk), sc, NEG_INF)
        return 0

    lax.fori_loop(0, nck, score_body, 0)

    n_beyond = (seq - nck * ck).astype(F32)
    n_part = 32

    def count(pred_fn):
        def cb(c, acc):
            ks = pl.multiple_of(c * ck, ck)
            hit = jnp.where(pred_fn(sc_ref[pl.ds(ks, ck), :], ks), 1.0, 0.0)
            return acc + jnp.sum(hit.reshape(ck // n_part, n_part, QB), axis=0)
        acc = lax.fori_loop(0, nck, cb, jnp.zeros((n_part, QB), F32))
        return jnp.sum(acc, axis=0, keepdims=True)

    def count_ge(thr):
        return count(lambda s, ks: s >= thr) + jnp.where(thr <= NEG_INF, n_beyond, 0.0)

    def search(n_static):
        def count_ge_static(thr):
            acc = jnp.zeros((n_part, QB), F32)
            for c in range(n_static):
                hit = jnp.where(sc_ref[c * ck:(c + 1) * ck, :] >= thr, 1.0, 0.0)
                acc = acc + jnp.sum(hit.reshape(ck // n_part, n_part, QB), axis=0)
            cnt = jnp.sum(acc, axis=0, keepdims=True)
            return cnt + jnp.where(thr <= NEG_INF, float(seq - n_static * ck), 0.0)

        def thr_body(it, carry):
            t_int, cnt_t = carry
            cand = t_int ^ lax.shift_left(jnp.int32(1), 31 - it)
            cnt = count_ge_static(_ordered_to_f32(cand))
            ok = cnt >= n_top
            return jnp.where(ok, cand, t_int), jnp.where(ok, cnt, cnt_t)

        t_int, cnt_t = lax.fori_loop(0, 32, thr_body, (jnp.full((1, QB), -2 ** 31, jnp.int32),
                                                       jnp.full((1, QB), float(seq), F32)))
        t_ref[...] = jnp.broadcast_to(t_int, t_ref.shape)
        cnt_ref[...] = jnp.broadcast_to(cnt_t, cnt_ref.shape)

    for n_static in range(1, seq // ck + 1):
        pl.when(nck == n_static)(functools.partial(search, n_static))
    t_int = t_ref[0:1, :]
    cnt_t = cnt_ref[0:1, :]
    thr = _ordered_to_f32(t_int)
    thr_up = _ordered_to_f32(t_int + 1)
    m_ref[...] = jnp.full(m_ref.shape, seq, jnp.int32)
    lo_ref[...] = jnp.broadcast_to(thr, lo_ref.shape)

    @pl.when(jnp.max(cnt_t) > n_top)
    def _():
        need = n_top - count_ge(thr_up)
        tie_beyond = thr <= NEG_INF

        far = float(2 * seq)

        def mark(c, _):
            ks = pl.multiple_of(c * ck, ck)
            s = sc_ref[pl.ds(ks, ck), :]
            kp = _key_pos(ks, ck).astype(F32)
            tie_ref[pl.ds(ks, ck), :] = jnp.where(s >= thr, jnp.where(s < thr_up, kp, far), far)
            return 0

        lax.fori_loop(0, nck, mark, 0)

        def idx_body(it, m):
            cand = m + lax.shift_left(jnp.int32(1), (seq.bit_length() - 2) - it)
            cand_f = cand.astype(F32)
            below = count(lambda s, ks: tie_ref[pl.ds(ks, ck), :] < cand_f)
            below = below + jnp.where(tie_beyond, jnp.maximum(cand - nck * ck, 0).astype(F32), 0.0)
            return jnp.where(below < need, cand, m)

        m_idx = lax.fori_loop(0, seq.bit_length() - 1, idx_body, jnp.zeros((1, QB), jnp.int32))

        def fold(pick, reduce, init):
            def cb(c, acc):
                ks = pl.multiple_of(c * ck, ck)
                vals = pick(sc_ref[pl.ds(ks, ck), :], tie_ref[pl.ds(ks, ck), :])
                return reduce(acc, vals.reshape(ck // 8, 8, QB))
            return lax.fori_loop(0, nck, cb, jnp.full((8, QB), init, F32))

        best = jnp.max(fold(lambda s, tie: jnp.where(tie < far, s, -jnp.inf),
                            lambda a, v: jnp.maximum(a, jnp.max(v, axis=0)), -jnp.inf), axis=0, keepdims=True)
        first = jnp.min(fold(lambda s, tie: jnp.where(s >= best, tie, far),
                             lambda a, v: jnp.minimum(a, jnp.min(v, axis=0)), far), axis=0, keepdims=True)
        first = jnp.minimum(first, float(seq))
        single = need == 1.0
        m_ref[...] = jnp.broadcast_to(jnp.where(single, first.astype(jnp.int32), m_idx), m_ref.shape)
        lo_ref[...] = jnp.broadcast_to(jnp.where(single, best, thr), lo_ref.shape)

    m_idx = m_ref[0:1, :]
    tie_lo = lo_ref[0:1, :]

    tpos = _query_pos(q0, hk)

    def attn_scores(c):
        return tuple(_dot_nt(k_ref[0, pl.ds(ks, hk), :], q_all)
                     for ks in halves(c))

    def attn_consume(c, ss, carry):
        ms, acc = carry
        for ks, s_all in zip(halves(c), ss):
            kp = _key_pos(ks, hk)
            s_idx = sc_ref[pl.ds(ks, hk), :]
            bound = jnp.where(kp <= m_idx, tie_lo, thr_up)
            mask = s_idx >= jnp.where(kp <= tpos, bound, jnp.inf)
            ms, alpha, p = _softmax_step_t(ms, s_all, mask)
            acc = alpha * acc + _dot(_with_ones(vt_ref[0, :, pl.ds(ks, hk)]), p)
        return tuple(ms), acc

    _, acc = _chunk_loop(nck, attn_scores, attn_consume,
                             (tuple(_m_init() for _ in range(HEADS)), _acc_init(HEADS)))
    _store_heads_t(o_ref, [_finish_t(acc, h) for h in range(HEADS)])


def _dsa(q, k, vt, qi, ki, misc):
    B, H, S, _ = q.shape
    n_top = min(DSA_TOPK, S // 4)
    return pl.pallas_call(
        functools.partial(_dsa_kernel, seq=S, n_top=n_top),
        out_shape=jax.ShapeDtypeStruct((B, S, D_GROUP), F32),
        grid=(B, S // QB),
        in_specs=[pl.BlockSpec((1, H, QB, 64), lambda b, i: (b, 0, i, 0)),
                  pl.BlockSpec((1, S, 64), lambda b, i: (b, 0, 0)),
                  pl.BlockSpec((1, HEAD_DIM, S), lambda b, i: (b, 0, 0)),
                  pl.BlockSpec((1, DSA_IDX_HEADS, QB, 128), lambda b, i: (b, 0, i, 0)),
                  pl.BlockSpec((1, S, 128), lambda b, i: (b, 0, 0)),
                  pl.BlockSpec((1, QB, 128), lambda b, i: (b, i, 0))],
        out_specs=pl.BlockSpec((1, QB, D_GROUP), lambda b, i: (b, i, 0)),
        scratch_shapes=[pltpu.VMEM((S, QB), F32), pltpu.VMEM((8, QB), jnp.int32),
                        pltpu.VMEM((8, QB), jnp.int32), pltpu.VMEM((8, QB), F32), pltpu.VMEM((8, QB), F32),
                        pltpu.VMEM((S, QB), F32)],
        compiler_params=_params("parallel", "arbitrary"),
        name="dsa_attn",
    )(q, k, vt, qi, ki, misc)


def _gelu_tanh(x):
    return 0.5 * x * (1.0 + jnp.tanh(np.sqrt(2.0 / np.pi) * (x + 0.044715 * (x * x * x))))


def _nsa_cmp_kernel(kg_ref, vg_ref, pos_ref, wk1_ref, wk2_ref, wv1_ref, wv2_ref, kc_ref, vct_ref):
    def compress(g_ref, w1_ref, w2_ref):
        top = _dot((g_ref[0] + pos_ref[0:1, :]).astype(BF16), w1_ref[0])
        bot = _dot((g_ref[0] + pos_ref[1:2, :]).astype(BF16), w1_ref[1])
        pre = top + pltpu.roll(bot, bot.shape[0] - 1, 0)
        return _dot(_gelu_tanh(pre).astype(BF16), w2_ref[...])

    kc_ref[0] = compress(kg_ref, wk1_ref, wk2_ref).astype(BF16)
    vct_ref[0] = compress(vg_ref, wv1_ref, wv2_ref).T[:HEAD_DIM].astype(BF16)


def _nsa_compress(kcmp, vcmp, pos, wk1, wk2, wv1, wv2):
    B, S, _ = kcmp.shape
    ng = S // NSA_CMP_STRIDE
    gw = NSA_CMP_STRIDE * HEAD_DIM
    kg = kcmp.reshape(B, ng, gw)
    vg = vcmp.reshape(B, ng, gw)

    def const(a):
        return pl.BlockSpec(a.shape, lambda b: (0,) * a.ndim)

    wv2p = jnp.concatenate([wv2, jnp.zeros((HEAD_DIM, LANES - HEAD_DIM), wv2.dtype)], axis=1)
    ins = [kg, vg, pos.reshape(2, gw), wk1.reshape(2, gw, HEAD_DIM).astype(BF16), wk2.astype(BF16),
           wv1.reshape(2, gw, HEAD_DIM).astype(BF16), wv2p.astype(BF16)]
    blk = pl.BlockSpec((1, ng, gw), lambda b: (b, 0, 0))
    return pl.pallas_call(
        _nsa_cmp_kernel,
        out_shape=[jax.ShapeDtypeStruct((B, ng, HEAD_DIM), BF16), jax.ShapeDtypeStruct((B, HEAD_DIM, ng), BF16)],
        grid=(B,),
        in_specs=[blk, blk] + [const(a) for a in ins[2:]],
        out_specs=[pl.BlockSpec((1, ng, HEAD_DIM), lambda b: (b, 0, 0)),
                   pl.BlockSpec((1, HEAD_DIM, ng), lambda b: (b, 0, 0))],
        compiler_params=_params("parallel"),
        name="nsa_compress",
    )(*ins)


def _nsa_kernel(q_ref, kc_ref, vct_ref, ks_ref, vst_ref, kw_ref, vwt_ref, misc_ref, ovl_ref, o_ref,
                *, seq):
    i = pl.program_id(1)
    q0 = i * QB
    n_slc = seq // NSA_SEL_LEN
    n_cmp = kc_ref.shape[1]
    q_all = q_ref[0].reshape(HEADS * QB, HEAD_DIM)

    last_tok = lax.broadcasted_iota(jnp.int32, (n_cmp, QB), 0) * NSA_CMP_STRIDE + (NSA_CMP_LEN - 1)
    cmask = last_tok <= _query_pos(q0, n_cmp)
    s_cmp = _dot_nt(kc_ref[0], q_all)
    ws = jnp.maximum(q0 - NSA_WINDOW, 0)
    win_parts = [(pl.multiple_of(ws + off, QB), n) for off, n in NSA_WIN_PARTS]
    s_wins = [_dot_nt(kw_ref[0, pl.ds(start, n), :], q_all) for start, n in win_parts]
    ps = []
    p_sum = jnp.zeros((n_cmp, QB), F32)
    for h in range(HEADS):
        s = jnp.where(cmask, s_cmp[:, h * QB:(h + 1) * QB], NEG_INF)
        e = jnp.where(cmask, jnp.exp2(s - jnp.max(s, axis=0, keepdims=True)), 0.0)
        p = e / jnp.maximum(jnp.sum(e, axis=0, keepdims=True), 1e-30)
        p_sum = p_sum + p
        ps.append(p.astype(BF16))
    o_cmp = _dot(vct_ref[0], jnp.concatenate(ps, axis=1))

    hi, lo = _split_bf16(p_sum)
    imp = _dot(ovl_ref[...], jnp.concatenate([hi, lo], axis=0))
    jb = lax.broadcasted_iota(jnp.int32, (n_slc, 1), 0)
    cur = (q0 + lax.broadcasted_iota(jnp.int32, (1, QB), 1)) >> SEL_SHIFT
    forced = (jb < NSA_N_INIT) | ((jb <= cur) & (jb > cur - NSA_N_LOCAL))
    imp = jnp.where(forced, FORCE_SCORE, jnp.where(jb <= cur, imp, NEG_INF))
    sub = 8
    ranks = []
    for g in range(n_slc // sub):
        blk = imp[g * sub:(g + 1) * sub]
        jg = jb[g * sub:(g + 1) * sub]
        rank_g = jnp.zeros((sub, QB), F32)
        for r in range(n_slc):
            row = imp[r:r + 1, :]
            ge = jnp.where(row >= blk, 1.0, 0.0)
            gt = jnp.where(row > blk, 1.0, 0.0)
            if g * sub > r:
                rank_g = rank_g + ge
            elif (g + 1) * sub - 1 <= r:
                rank_g = rank_g + gt
            else:
                rank_g = rank_g + jnp.where(jg > r, ge, gt)
        ranks.append(rank_g)
    rank = jnp.concatenate(ranks, axis=0)
    sel_t = jnp.where(rank < min(NSA_N_SEL, n_slc), 1.0, 0.0)
    sel_t = jnp.concatenate([sel_t, jnp.zeros((LANES - n_slc, QB), F32)], axis=0).astype(BF16)

    ms_win, acc_win = [_m_init() for _ in range(HEADS)], _acc_init(HEADS)
    for (start, n), s_win in zip(win_parts, s_wins):
        dist = _query_pos(q0, n) - _key_pos(start, n)
        wmask = (dist >= 0) & (dist < NSA_WINDOW)
        ms_win, alpha, p_win = _softmax_step_t(ms_win, s_win, wmask)
        acc_win = alpha * acc_win + _dot(_with_ones(vwt_ref[0, :, pl.ds(start, n)]), p_win)

    ck = DSA_CK
    hk = ck // 2
    tpos = _query_pos(q0, hk)

    n_slc_chunks = _n_chunks(q0, QB, ck)

    def slc_starts(c):
        return [pl.multiple_of(c * ck + j * hk, hk) for j in range(2)]

    def slc_scores(c):
        out = []
        for ks in slc_starts(c):
            blk = (ks + lax.broadcasted_iota(jnp.int32, (hk, LANES), 0)) >> SEL_SHIFT
            expand = jnp.where(blk == lax.broadcasted_iota(jnp.int32, (hk, LANES), 1), 1.0, 0.0)
            out.append(_dot(expand.astype(BF16), sel_t))
        return tuple(out) + tuple(_dot_nt(ks_ref[0, pl.ds(ks, hk), :], q_all) for ks in slc_starts(c))

    def slc_consume(c, ss, carry):
        ms, acc = carry
        for ks, chosen, s_all in zip(slc_starts(c), ss[:2], ss[2:]):
            mask = chosen > jnp.where(_key_pos(ks, hk) <= tpos, 0.5, 2.0)
            ms, alpha, p = _softmax_step_t(ms, s_all, mask)
            acc = alpha * acc + _dot(_with_ones(vst_ref[0, :, pl.ds(ks, hk)]), p)
        return tuple(ms), acc

    _, acc_slc = _chunk_loop(n_slc_chunks, slc_scores, slc_consume,
                                 (tuple(_m_init() for _ in range(HEADS)), _acc_init(HEADS)))

    gate_rows = 16
    gate = jax.nn.sigmoid(misc_ref[0].T[MISC_GATE:MISC_GATE + gate_rows, :])
    outs = []
    for h in range(HEADS):
        outs.append(gate[h:h + 1] * o_cmp[:, h * QB:(h + 1) * QB]
                    + gate[HEADS + h:HEADS + h + 1] * _finish_t(acc_slc, h)
                    + gate[2 * HEADS + h:2 * HEADS + h + 1] * _finish_t(acc_win, h))
    _store_heads_t(o_ref, outs)


def _nsa_overlap(seq):
    ng = seq // NSA_CMP_STRIDE
    n_slc = seq // NSA_SEL_LEN
    n = np.arange(ng)
    first, last = n * NSA_CMP_STRIDE, n * NSA_CMP_STRIDE + NSA_CMP_LEN - 1
    start = np.arange(n_slc) * NSA_SEL_LEN
    ovl = ((first[None, :] <= start[:, None] + NSA_SEL_LEN - 1) & (last[None, :] >= start[:, None]))
    ovl = ovl & (last[None, :] < seq)
    ovl = ovl.astype(np.float32)
    return jnp.asarray(np.concatenate([ovl, ovl], 1), BF16)


def _nsa(q, kc, vct, k_slc, vt_slc, k_win, vt_win, misc):
    B, H, S, _ = q.shape
    ng = kc.shape[1]
    ovl = _nsa_overlap(S)
    keys = pl.BlockSpec((1, S, HEAD_DIM), lambda b, i: (b, 0, 0))
    vals_t = pl.BlockSpec((1, HEAD_DIM, S), lambda b, i: (b, 0, 0))
    return pl.pallas_call(
        functools.partial(_nsa_kernel, seq=S),
        out_shape=jax.ShapeDtypeStruct((B, S, D_GROUP), F32),
        grid=(B, S // QB),
        in_specs=[pl.BlockSpec((1, H, QB, 64), lambda b, i: (b, 0, i, 0)),
                  pl.BlockSpec((1, ng, HEAD_DIM), lambda b, i: (b, 0, 0)),
                  pl.BlockSpec((1, HEAD_DIM, ng), lambda b, i: (b, 0, 0)),
                  keys, vals_t, keys, vals_t,
                  pl.BlockSpec((1, QB, 128), lambda b, i: (b, i, 0)),
                  pl.BlockSpec(ovl.shape, lambda b, i: (0, 0))],
        out_specs=pl.BlockSpec((1, QB, D_GROUP), lambda b, i: (b, i, 0)),
        compiler_params=_params("parallel", "arbitrary"),
        name="nsa_attn",
    )(q, kc, vct, k_slc, vt_slc, k_win, vt_win, misc, ovl)


def _outproj_kernel(x_ref, ya_ref, yb_ref, yc_ref, yd_ref, gg_ref, wo_ref, g_ref, b_ref, o_ref):
    acc = None
    for gi, y_ref in enumerate((ya_ref, yb_ref, yc_ref, yd_ref)):
        lo, hi = gi * D_GROUP, (gi + 1) * D_GROUP
        yn = _rms_norm(y_ref[...], gg_ref[:, lo:hi]).astype(BF16)
        part = _dot(yn, wo_ref[lo:hi, :])
        acc = part if acc is None else acc + part
    o_ref[...] = _layer_norm(DN_ALPHA * x_ref[...] + acc, g_ref[...], b_ref[...])


def _outproj(x2, ys, gg, wo, g, b, *, tm):
    n, d = x2.shape
    row = lambda w: pl.BlockSpec((tm, w), lambda i: (i, 0))
    const = lambda a: pl.BlockSpec(a.shape, lambda i: (0, 0))
    return pl.pallas_call(
        _outproj_kernel,
        out_shape=jax.ShapeDtypeStruct((n, d), F32),
        grid=(n // tm,),
        in_specs=[row(d)] + [row(D_GROUP)] * 4 + [const(gg), const(wo), const(g), const(b)],
        out_specs=row(d),
        compiler_params=_params("parallel"),
        name="out_proj",
    )(x2, *ys, gg, wo, g, b)


def _mixer(x, w_in, q_norm, w_uq, kv_norm, w_ukv, cmp_pos, wk1, wk2, wv1, wv2, tabs, place, maps):
    B, S, D = x.shape
    wm = _gather_cols(w_in, maps).astype(BF16)
    wt = jnp.concatenate([wm[:, slice(*_MAIN_OFF['slab_v'])], wm[:, slice(*_MAIN_OFF['sb_v'])]], axis=1).T
    (mq, mk, mvt, nq, dq, sbq, sbk, sbvt, kcmp, kslc, kwin, dk, vcmp, vslct, vwint, dvt,
     qi, ki, misc) = _inproj(x, wm, wt, tabs, q_norm[None, :], kv_norm[None, :],
                             _mla_weights(w_uq, w_ukv), place, tm=min(512, S))
    y_a = _mla(mq, mk, mvt)
    kc, vct = _nsa_compress(kcmp, vcmp, cmp_pos, wk1, wk2, wv1, wv2)
    y_b = _nsa(nq, kc, vct, kslc, vslct, kwin, vwint, misc)
    y_c = _dsa(dq, dk, dvt, qi, ki, misc)
    y_d = _sb(sbq, sbk, sbvt)
    return y_a, y_b, y_c, y_d


def kernel(x, ln1_g, ln1_b, ffn1_w1, ffn1_w3, ffn1_w2, w_in, mla_q_norm, mla_w_uq, mla_kv_norm,
           mla_w_ukv, nsa_cmp_pos, nsa_cmp_wk1, nsa_cmp_wk2, nsa_cmp_wv1, nsa_cmp_wv2, group_norm_g,
           w_out, ln2_g, ln2_b, ffn2_w1, ffn2_w3, ffn2_w2, ln3_g, ln3_b):
    B, S, D = x.shape
    n = B * S
    tabs = _rope_tables(S)
    place = _placement_constants()
    maps = _column_maps()
    tm_ffn = min(512, n)
    x2 = x.reshape(n, D)
    for l in range(DEPTH):
        x2 = _ffn_ln(x2, ffn1_w1[l].astype(BF16), ffn1_w3[l].astype(BF16), ffn1_w2[l].astype(BF16),
                     ln1_g[l][None, :], ln1_b[l][None, :], tm=tm_ffn, tf=256)
        ys = _mixer(x2.reshape(B, S, D), w_in[l], mla_q_norm[l], mla_w_uq[l], mla_kv_norm[l],
                    mla_w_ukv[l], nsa_cmp_pos[l], nsa_cmp_wk1[l], nsa_cmp_wk2[l], nsa_cmp_wv1[l],
                    nsa_cmp_wv2[l], tabs, place, maps)
        x2 = _outproj(x2, [y.reshape(n, D_GROUP) for y in ys], group_norm_g[l][None, :],
                      w_out[l].astype(BF16), ln2_g[l][None, :], ln2_b[l][None, :], tm=min(1024, n))
        x2 = _ffn_ln(x2, ffn2_w1[l].astype(BF16), ffn2_w3[l].astype(BF16), ffn2_w2[l].astype(BF16),
                     ln3_g[l][None, :], ln3_b[l][None, :], tm=tm_ffn, tf=256)
    return x2.reshape(B, S, D)
```

```python
import functools

import numpy as np
import jax
import jax.numpy as jnp
from jax import lax
from jax.experimental import pallas as pl
from jax.experimental.pallas import tpu as pltpu

F32 = jnp.float32
BF16 = jnp.bfloat16

D_MODEL = 1024
DEPTH = 2
HEADS = 4
HEAD_DIM = 64
D_GROUP = HEADS * HEAD_DIM
N_GROUPS = 4
D_FF = 2816
ROPE_THETA = 10000.0
MLA_Q_RANK = 256
MLA_KV_RANK = 128
MLA_D_NOPE = 64
MLA_D_ROPE = 32
MLA_D_V = 64
NSA_CMP_LEN = 32
NSA_CMP_STRIDE = 16
NSA_SEL_LEN = 64
NSA_N_SEL = 8
NSA_N_INIT = 1
NSA_N_LOCAL = 2
NSA_WINDOW = 512
SEL_SHIFT = NSA_SEL_LEN.bit_length() - 1
DSA_TOPK = 256
DSA_IDX_HEADS = 8
DSA_IDX_DIM = 32
DN_ALPHA = (2.0 * DEPTH) ** 0.25
LN_EPS = 1e-5
RMS_EPS = 1e-6
NEG_INF = -1e30
FORCE_SCORE = 1e4
LOG2_E = 1.4426950408889634

LANES = 128
QB = 128
CK = 256
DSA_CK = 512
NSA_WIN_PARTS = ((0, NSA_WINDOW + QB),)
VMEM_LIMIT = 48 * 1024 * 1024

MISC_KROPE = 0
MISC_IDXK = 32
MISC_IDXW = 64
MISC_GATE = 72

NT_DIMS = (((1,), (1,)), ((), ()))


def _dot(a, b):
    return jnp.dot(a, b, preferred_element_type=F32)


def _dot_nt(a, b):
    return lax.dot_general(a, b, NT_DIMS, preferred_element_type=F32)


def _split_bf16(x):
    hi = x.astype(BF16)
    lo = (x - hi.astype(F32)).astype(BF16)
    return hi, lo


def _layer_norm(y, g, b):
    mu = jnp.mean(y, -1, keepdims=True)
    d = y - mu
    var = jnp.mean(d * d, -1, keepdims=True)
    return d * lax.rsqrt(var + LN_EPS) * g + b


def _rms_norm(y, g):
    return y * lax.rsqrt(jnp.mean(y * y, -1, keepdims=True) + RMS_EPS) * g


def _params(*sem):
    return pltpu.CompilerParams(dimension_semantics=sem, vmem_limit_bytes=VMEM_LIMIT)


def _ffn_kernel(x_ref, w1_ref, w3_ref, w2_ref, g_ref, b_ref, o_ref, a_ref, *, tf):
    xb = x_ref[...].astype(BF16)
    n_slabs = w1_ref.shape[1] // tf

    def up(j):
        return _dot(xb, w1_ref[:, j * tf:(j + 1) * tf]), _dot(xb, w3_ref[:, j * tf:(j + 1) * tf])

    hu = up(0)
    for j in range(n_slabs):
        nxt = up(j + 1) if j + 1 < n_slabs else None
        h, u = hu
        a_ref[:, j * tf:(j + 1) * tf] = (h * jax.nn.sigmoid(h) * u).astype(BF16)
        hu = nxt
    y = DN_ALPHA * x_ref[...] + 0.5 * _dot(a_ref[...], w2_ref[...])
    o_ref[...] = _layer_norm(y, g_ref[...], b_ref[...])


def _ffn_ln(x2, w1, w3, w2, g, b, *, tm, tf):
    n, d = x2.shape
    dff = w1.shape[1]
    const = lambda a: pl.BlockSpec(a.shape, lambda i: (0, 0))
    return pl.pallas_call(
        functools.partial(_ffn_kernel, tf=tf),
        out_shape=jax.ShapeDtypeStruct((n, d), F32),
        grid=(n // tm,),
        in_specs=[pl.BlockSpec((tm, d), lambda i: (i, 0)), const(w1), const(w3), const(w2), const(g), const(b)],
        out_specs=pl.BlockSpec((tm, d), lambda i: (i, 0)),
        scratch_shapes=[pltpu.VMEM((tm, dff), BF16)],
        compiler_params=_params("parallel"),
        name="ffn_ln",
    )(x2, w1, w3, w2, g, b)


_MAIN_ORDER = (
    ('mla_cq', 256), ('nsa_q', 256), ('dsa_q', 256), ('idx_q', 256),
    ('sb_q', 256), ('sb_k', 256), ('sb_v', 256),
    ('slab_k', 256),
    ('slab_v', 256),
    ('mla_ckv', 128),
    ('misc', 128),
)
_MAIN_OFF = {}
_o = 0
for _n, _w in _MAIN_ORDER:
    _MAIN_OFF[_n] = (_o, _o + _w)
    _o += _w
N_MAIN = _o

_IN_SPLITS = (
    ('mla_cq', MLA_Q_RANK), ('mla_ckv', MLA_KV_RANK), ('mla_krope', MLA_D_ROPE),
    ('nsa_q', D_GROUP), ('nsa_k_cmp', HEAD_DIM), ('nsa_v_cmp', HEAD_DIM),
    ('nsa_k_slc', HEAD_DIM), ('nsa_v_slc', HEAD_DIM), ('nsa_k_win', HEAD_DIM),
    ('nsa_v_win', HEAD_DIM), ('nsa_gate', 3 * HEADS),
    ('dsa_q', D_GROUP), ('dsa_k', HEAD_DIM), ('dsa_v', HEAD_DIM),
    ('idx_q', DSA_IDX_HEADS * DSA_IDX_DIM), ('idx_k', DSA_IDX_DIM), ('idx_w', DSA_IDX_HEADS),
    ('sb_q', D_GROUP), ('sb_k', D_GROUP), ('sb_v', D_GROUP),
)
_SRC = {}
_o = 0
for _n, _w in _IN_SPLITS:
    _SRC[_n] = np.arange(_o, _o + _w)
    _o += _w
D_IN = _o


def _swap_halves(width, dim):
    idx = np.arange(width)
    return (idx // dim) * dim + (idx % dim + dim // 2) % dim


def _column_maps():
    pieces = {
        'mla_cq': [_SRC['mla_cq']], 'nsa_q': [_SRC['nsa_q']], 'dsa_q': [_SRC['dsa_q']],
        'idx_q': [_SRC['idx_q']], 'sb_q': [_SRC['sb_q']], 'sb_k': [_SRC['sb_k']], 'sb_v': [_SRC['sb_v']],
        'slab_k': [_SRC['nsa_k_cmp'], _SRC['nsa_k_slc'], _SRC['nsa_k_win'], _SRC['dsa_k']],
        'slab_v': [_SRC['nsa_v_cmp'], _SRC['nsa_v_slc'], _SRC['nsa_v_win'], _SRC['dsa_v']],
        'mla_ckv': [_SRC['mla_ckv']],
        'misc': [_SRC['mla_krope'], _SRC['idx_k'], _SRC['idx_w'], _SRC['nsa_gate'],
                 -np.ones(LANES - MISC_GATE - 3 * HEADS, np.int64)],
    }
    return np.concatenate([np.concatenate(pieces[n]) for n, _ in _MAIN_ORDER])


def _gather_cols(w, cols):
    cols = [int(c) for c in cols]
    parts, i = [], 0
    while i < len(cols):
        j = i + 1
        if cols[i] < 0:
            while j < len(cols) and cols[j] < 0:
                j += 1
            parts.append(jnp.zeros((w.shape[0], j - i), w.dtype))
        else:
            while j < len(cols) and cols[j] == cols[j - 1] + 1:
                j += 1
            parts.append(w[:, cols[i]:cols[i] + (j - i)])
        i = j
    return jnp.concatenate(parts, axis=1)


def _rope_tables(seq):
    def base(dim):
        inv = ROPE_THETA ** (-jnp.arange(0, dim, 2, dtype=F32) / dim)
        ang = jnp.arange(seq, dtype=F32)[:, None] * inv[None, :]
        c, s = jnp.cos(ang), jnp.sin(ang)
        return jnp.concatenate([c, c], -1), jnp.concatenate([-s, s], -1)

    c64, s64 = base(HEAD_DIM)
    c32, s32 = base(DSA_IDX_DIM)
    t64 = (jnp.tile(c64, (1, 4)), jnp.tile(s64, (1, 4)))
    t32 = (jnp.tile(c32, (1, 8)), jnp.tile(s32, (1, 8)))
    ones = jnp.ones((seq, LANES - 64), F32)
    td = (jnp.concatenate([c32, c32, ones], -1), jnp.concatenate([s32, s32, 0.0 * ones], -1))
    scale = (MLA_D_NOPE + MLA_D_ROPE) ** -0.5 * LOG2_E
    cq = jnp.concatenate([jnp.ones((seq, 64), F32), c32, jnp.zeros((seq, 32), F32)], -1) * scale
    sq = jnp.concatenate([jnp.zeros((seq, 64), F32), s32, jnp.zeros((seq, 32), F32)], -1) * scale
    tq = (jnp.tile(cq, (1, 4)), jnp.tile(sq, (1, 4)))
    return t64, t32, td, tq


def _placement_constants():
    p_hi = np.zeros((256, 1024), np.float32)
    p_lo = np.zeros((256, 1024), np.float32)
    for h in range(DSA_IDX_HEADS):
        for d in range(DSA_IDX_DIM):
            p_hi[h * 32 + d, h * 128 + d] = 1
            p_hi[h * 32 + d, h * 128 + 64 + d] = 1
            p_lo[h * 32 + d, h * 128 + 32 + d] = 1
    pk_hi = np.zeros((128, 128), np.float32)
    pk_lo = np.zeros((128, 128), np.float32)
    for d in range(DSA_IDX_DIM):
        pk_hi[MISC_IDXK + d, d] = 1
        pk_hi[MISC_IDXK + d, 32 + d] = 1
        pk_lo[MISC_IDXK + d, 64 + d] = 1
    pk_pe = np.zeros((128, 512), np.float32)
    for h in range(HEADS):
        for d in range(MLA_D_ROPE):
            pk_pe[MISC_KROPE + d, h * 128 + 64 + d] = 1
    return tuple(jnp.asarray(a, BF16) for a in (p_hi, p_lo, pk_hi, pk_lo, pk_pe))


def _mla_weights(w_uq, w_ukv):
    dq = MLA_D_NOPE + MLA_D_ROPE
    cols_q = -np.ones(512, np.int64)
    cols_qr = -np.ones(512, np.int64)
    cols_k = -np.ones(512, np.int64)
    cols_v = np.zeros(256, np.int64)
    for h in range(HEADS):
        cols_q[h * 128:h * 128 + dq] = h * dq + np.arange(dq)
        pe = h * dq + MLA_D_NOPE + np.arange(MLA_D_ROPE)
        cols_qr[h * 128 + 64:h * 128 + 96] = pe[_swap_halves(32, 32)]
        cols_k[h * 128:h * 128 + 64] = h * 128 + np.arange(64)
        cols_v[h * 64:(h + 1) * 64] = h * 128 + 64 + np.arange(64)
    return (_gather_cols(w_uq, cols_q).astype(BF16), _gather_cols(w_uq, cols_qr).astype(BF16),
            _gather_cols(w_ukv, cols_k).astype(BF16), _gather_cols(w_ukv, cols_v).T.astype(BF16))


def _rotate_half(z, dim):
    n = z.shape[1]
    half = dim // 2
    lane = lax.broadcasted_iota(jnp.int32, z.shape, 1)
    in_first_half = (lane & (dim - 1)) < half
    return jnp.where(in_first_half, pltpu.roll(z, n - half, 1), pltpu.roll(z, half, 1))


def _inproj_kernel(x_ref, wm_ref, wt_ref, c64_ref, s64_ref, c32_ref, s32_ref, cd_ref, sd_ref,
                   cq_ref, sq_ref, qg_ref, kvg_ref, wuq_ref, wuqr_ref, wuk_ref, wuvt_ref,
                   phi_ref, plo_ref, pkhi_ref, pklo_ref, pkpe_ref,
                   mq_ref, mk_ref, mvt_ref, nq_ref, dq_ref, sbq_ref, sbk_ref, sbvt_ref,
                   kcmp_ref, kslc_ref, kwin_ref, dk_ref, vcmp_ref, vslct_ref, vwint_ref, dvt_ref,
                   qi_ref, ki_ref, misc_ref):
    xb = x_ref[0].astype(BF16)

    def main(name):
        a, b = _MAIN_OFF[name]
        return _dot(xb, wm_ref[:, a:b])

    def roped(name, c_ref, s_ref, dim):
        z = main(name)
        return z * c_ref[...] + _rotate_half(z, dim) * s_ref[...]

    def store_heads(ref, val, width):
        for h in range(HEADS):
            ref[0, h] = val[:, h * width:(h + 1) * width].astype(ref.dtype)

    cqn = _rms_norm(main('mla_cq'), qg_ref[...]).astype(BF16)
    q = _dot(cqn, wuq_ref[...]) * cq_ref[...] + _dot(cqn, wuqr_ref[...]) * sq_ref[...]
    store_heads(mq_ref, q, 128)
    ckv_misc = _dot(xb, wm_ref[:, _MAIN_OFF['mla_ckv'][0]:_MAIN_OFF['misc'][1]])
    z_misc = ckv_misc[:, LANES:]
    misc = z_misc * cd_ref[...] + _rotate_half(z_misc, DSA_IDX_DIM) * sd_ref[...]
    misc_ref[0] = misc
    m_hi, m_lo = _split_bf16(misc)
    ckvn = _rms_norm(ckv_misc[:, :LANES], kvg_ref[...]).astype(BF16)
    k = _dot(ckvn, wuk_ref[...]) + _dot(m_hi, pkpe_ref[...])
    store_heads(mk_ref, k, 128)
    mvt = _dot_nt(wuvt_ref[...], ckvn)
    for h in range(HEADS):
        mvt_ref[0, h] = mvt[h * 64:(h + 1) * 64, :].astype(BF16)
    qscale = HEAD_DIM ** -0.5 * LOG2_E
    store_heads(nq_ref, roped('nsa_q', c64_ref, s64_ref, HEAD_DIM) * qscale, 64)
    store_heads(dq_ref, roped('dsa_q', c64_ref, s64_ref, HEAD_DIM) * qscale, 64)
    store_heads(sbq_ref, main('sb_q') * qscale, 64)
    store_heads(sbk_ref, main('sb_k'), 64)
    sk = roped('slab_k', c64_ref, s64_ref, HEAD_DIM)
    for j, ref in enumerate((kcmp_ref, kslc_ref, kwin_ref, dk_ref)):
        ref[0] = sk[:, j * 64:(j + 1) * 64].astype(ref.dtype)
    vt = _dot_nt(wt_ref[...], xb)
    vcmp_ref[0] = vt[0:LANES, :].T[:, 0:HEAD_DIM]
    for j, ref in ((1, vslct_ref), (2, vwint_ref), (3, dvt_ref)):
        ref[0] = vt[j * 64:(j + 1) * 64, :].astype(BF16)
    for h in range(HEADS):
        sbvt_ref[0, h] = vt[256 + h * 64:256 + (h + 1) * 64, :].astype(BF16)
    qi_hi, qi_lo = _split_bf16(roped('idx_q', c32_ref, s32_ref, DSA_IDX_DIM))
    qi = _dot(qi_hi, phi_ref[...]) + _dot(qi_lo, plo_ref[...])
    for h in range(DSA_IDX_HEADS):
        qi_ref[0, h] = qi[:, h * 128:(h + 1) * 128].astype(BF16)
    ki_ref[0] = (_dot(m_hi, pkhi_ref[...]) + _dot(m_lo, pklo_ref[...])).astype(BF16)


def _inproj(x, wm, wt, tabs, qg, kvg, mla_w, place, *, tm):
    B, S, D = x.shape
    (c64, s64), (c32, s32), (cd, sd), (cq, sq) = tabs
    wuq, wuqr, wuk, wuvt = mla_w

    def const(a):
        return pl.BlockSpec(a.shape, lambda b, s: (0,) * a.ndim)

    def tab(a):
        return pl.BlockSpec((tm, a.shape[1]), lambda b, s: (s, 0))

    def heads_out(w, dt=BF16):
        return (jax.ShapeDtypeStruct((B, HEADS, S, w), dt),
                pl.BlockSpec((1, HEADS, tm, w), lambda b, s: (b, 0, s, 0)))

    def flat_out(w, dt):
        return (jax.ShapeDtypeStruct((B, S, w), dt), pl.BlockSpec((1, tm, w), lambda b, s: (b, s, 0)))

    heads_t = (jax.ShapeDtypeStruct((B, HEADS, HEAD_DIM, S), BF16),
               pl.BlockSpec((1, HEADS, HEAD_DIM, tm), lambda b, s: (b, 0, 0, s)))
    flat_t = (jax.ShapeDtypeStruct((B, HEAD_DIM, S), BF16),
              pl.BlockSpec((1, HEAD_DIM, tm), lambda b, s: (b, 0, s)))
    qi_out = (jax.ShapeDtypeStruct((B, DSA_IDX_HEADS, S, 128), BF16),
              pl.BlockSpec((1, DSA_IDX_HEADS, tm, 128), lambda b, s: (b, 0, s, 0)))
    outs = [heads_out(128), heads_out(128), heads_t,
            heads_out(64), heads_out(64),
            heads_out(64), heads_out(64), heads_t,
            flat_out(64, F32), flat_out(64, BF16), flat_out(64, BF16), flat_out(64, BF16),
            flat_out(64, F32), flat_t, flat_t, flat_t,
            qi_out, flat_out(128, BF16), flat_out(128, F32)]
    ins = [x, wm, wt, c64, s64, c32, s32, cd, sd, cq, sq, qg, kvg, wuq, wuqr, wuk, wuvt, *place]
    in_specs = [pl.BlockSpec((1, tm, D), lambda b, s: (b, s, 0)), const(wm), const(wt),
                tab(c64), tab(s64), tab(c32), tab(s32), tab(cd), tab(sd), tab(cq), tab(sq),
                const(qg), const(kvg), const(wuq), const(wuqr), const(wuk), const(wuvt),
                *[const(p) for p in place]]
    return pl.pallas_call(
        _inproj_kernel,
        out_shape=[o[0] for o in outs],
        grid=(B, S // tm),
        in_specs=in_specs,
        out_specs=[o[1] for o in outs],
        compiler_params=_params("parallel", "arbitrary"),
        name="in_proj",
    )(*ins)


V_ROWS = HEAD_DIM


def _acc_init(n_heads, qb=QB):
    return jnp.zeros((2 * V_ROWS, n_heads * qb), F32)


def _m_init(qb=QB):
    return jnp.full((1, qb), NEG_INF, F32)


def _with_ones(vt):
    return jnp.concatenate([vt, jnp.ones((V_ROWS, vt.shape[1]), vt.dtype)], axis=0)


def _softmax_step_t(ms, s_all, mask):
    new_ms, alphas, es = [], [], []
    qb = mask.shape[1]
    for h, m in enumerate(ms):
        s = jnp.where(mask, s_all[:, h * qb:(h + 1) * qb], NEG_INF)
        m_new = jnp.maximum(m, jnp.max(s, axis=0, keepdims=True))
        alphas.append(jnp.exp2(m - m_new))
        es.append(jnp.where(mask, jnp.exp2(s - m_new), 0.0).astype(BF16))
        new_ms.append(m_new)
    return new_ms, jnp.concatenate(alphas, axis=1), jnp.concatenate(es, axis=1)


def _finish_t(acc, h, qb=QB):
    blk = acc[:, h * qb:(h + 1) * qb]
    return blk[:V_ROWS] / jnp.maximum(blk[V_ROWS:V_ROWS + 1], 1e-30)


def _store_heads_t(o_ref, outs_t):
    for p in range(HEADS // 2):
        pair = jnp.concatenate([outs_t[2 * p], outs_t[2 * p + 1]], axis=0)
        o_ref[0, :, p * 128:(p + 1) * 128] = pair.T


def _chunk_loop(n, scores, consume, init):
    return lax.fori_loop(0, n, lambda c, state: consume(c, scores(c), state), init)


def _n_chunks(q0, qb=QB, ck=CK):
    return (q0 + qb + ck - 1) // ck


def _key_pos(ks, n, qb=QB):
    return ks + lax.broadcasted_iota(jnp.int32, (n, qb), 0)


def _query_pos(q0, n, qb=QB):
    return q0 + lax.broadcasted_iota(jnp.int32, (n, qb), 1)


QB_WIDE = 256


def _mla_kernel(q_ref, k_ref, vt_ref, o_ref):
    qb = QB_WIDE
    q0 = pl.program_id(1) * qb
    tpos = _query_pos(q0, CK, qb)

    n = _n_chunks(q0, qb)

    def scores(c):
        ks = pl.multiple_of(c * CK, CK)
        return tuple(_dot_nt(k_ref[0, h, pl.ds(ks, CK), :], q_ref[0, h]) for h in range(HEADS))

    def consume(c, ss, carry):
        ms, accs = carry
        ks = pl.multiple_of(c * CK, CK)
        mask = _key_pos(ks, CK, qb) <= tpos
        steps = [_softmax_step_t([ms[h]], ss[h], mask) for h in range(HEADS)]
        new_accs = [steps[h][1] * accs[h] + _dot(_with_ones(vt_ref[0, h, :, pl.ds(ks, CK)]), steps[h][2])
                    for h in range(HEADS)]
        return tuple(st[0][0] for st in steps), tuple(new_accs)

    init = (tuple(_m_init(qb) for _ in range(HEADS)), tuple(_acc_init(1, qb) for _ in range(HEADS)))
    _, accs = _chunk_loop(n, scores, consume, init)
    _store_heads_t(o_ref, [_finish_t(accs[h], 0, qb) for h in range(HEADS)])


def _mla(q, k, vt):
    B, H, S, _ = q.shape
    qb = QB_WIDE
    return pl.pallas_call(
        _mla_kernel,
        out_shape=jax.ShapeDtypeStruct((B, S, D_GROUP), F32),
        grid=(B, S // qb),
        in_specs=[pl.BlockSpec((1, H, qb, 128), lambda b, i: (b, 0, i, 0)),
                  pl.BlockSpec((1, H, S, 128), lambda b, i: (b, 0, 0, 0)),
                  pl.BlockSpec((1, H, HEAD_DIM, S), lambda b, i: (b, 0, 0, 0))],
        out_specs=pl.BlockSpec((1, qb, D_GROUP), lambda b, i: (b, i, 0)),
        compiler_params=_params("parallel", "arbitrary"),
        name="mla_attn",
    )(q, k, vt)


SB_TOT_ROWS = 16


SB_CK = 128


def _sb_kernel(q_ref, k_ref, vt_ref, tri_ref, o_ref):
    qb, kc = QB_WIDE, SB_CK
    n_sub = 2
    step = n_sub * kc
    q0 = pl.program_id(1) * qb
    n_steps = (q0 + qb) // step
    tri = tri_ref[...]

    def key_start(r):
        return pl.multiple_of(q0 + qb - (r + 1) * step, step)

    def scores(r):
        ks = key_start(r)
        return tuple(_dot_nt(k_ref[0, h, pl.ds(ks, step), :], q_ref[0, h]) for h in range(HEADS))

    def consume(r, zs, carry, masked):
        runs, accs = carry
        ks = key_start(r)
        if masked:
            mask = _key_pos(ks, step, qb) < _query_pos(q0, step, qb)
        sps, parts = [], []
        for h in range(HEADS):
            sp = jnp.maximum(zs[h], 0.0) + jnp.log2(1.0 + jnp.exp2(-jnp.abs(zs[h])))
            sps.append(sp)
            parts.append(_split_bf16(jnp.where(mask, sp, 0.0) if masked else sp))
        stacked = jnp.concatenate(
            [jnp.concatenate([parts[h][j][b * kc:(b + 1) * kc] for j in range(2)], axis=0)
             for b in range(n_sub) for h in range(HEADS)], axis=1)
        res = _dot(tri, stacked)
        new_runs, new_accs = [], []
        for h in range(HEADS):
            run = runs[h]
            afters = [None] * n_sub
            for b in reversed(range(n_sub)):
                col = (b * HEADS + h) * qb
                afters[b] = res[:kc, col:col + qb] + run
                run = run + res[kc:kc + 1, col:col + qb]
            a = jnp.exp2(zs[h] - sps[h] + jnp.concatenate(afters, axis=0))
            if masked:
                a = jnp.where(mask, a, 0.0)
            new_accs.append(accs[h] + _dot(vt_ref[0, h, :, pl.ds(ks, step)], a.astype(BF16)))
            new_runs.append(run)
        return tuple(new_runs), tuple(new_accs)

    init = (tuple(jnp.zeros((1, qb), F32) for _ in range(HEADS)),
            tuple(jnp.zeros((HEAD_DIM, qb), F32) for _ in range(HEADS)))
    carry = consume(0, scores(0), init, True)
    _, accs = lax.fori_loop(1, n_steps, lambda r, c: consume(r, scores(r), c, False), carry)
    _store_heads_t(o_ref, accs)


def _sb(q, k, vt):
    B, H, S, _ = q.shape
    qb, kc = QB_WIDE, SB_CK
    u = np.triu(np.ones((kc, kc), np.float32), 1)
    u = np.concatenate([u, np.ones((SB_TOT_ROWS, kc), np.float32)], 0)
    tri = jnp.asarray(-np.concatenate([u, u], 1), BF16)
    return pl.pallas_call(
        _sb_kernel,
        out_shape=jax.ShapeDtypeStruct((B, S, D_GROUP), F32),
        grid=(B, S // qb),
        in_specs=[pl.BlockSpec((1, H, qb, 64), lambda b, i: (b, 0, i, 0)),
                  pl.BlockSpec((1, H, S, 64), lambda b, i: (b, 0, 0, 0)),
                  pl.BlockSpec((1, H, HEAD_DIM, S), lambda b, i: (b, 0, 0, 0)),
                  pl.BlockSpec(tri.shape, lambda b, i: (0, 0))],
        out_specs=pl.BlockSpec((1, qb, D_GROUP), lambda b, i: (b, i, 0)),
        compiler_params=_params("parallel", "arbitrary"),
        name="sb_attn",
    )(q, k, vt, tri)


def _ordered_to_f32(c):
    bits = c ^ ((c >> 31) & jnp.int32(0x7FFFFFFF))
    return pltpu.bitcast(bits, F32)


def _dsa_kernel(q_ref, k_ref, vt_ref, qi_ref, ki_ref, misc_ref, o_ref, sc_ref, m_ref, t_ref, cnt_ref,
                lo_ref, tie_ref, *, seq, n_top):
    q0 = pl.program_id(1) * QB
    ck = DSA_CK
    hk = ck // 2
    nck = _n_chunks(q0, QB, ck)
    w_t = misc_ref[0].T[MISC_IDXW:MISC_IDXW + DSA_IDX_HEADS, :] * (DSA_IDX_HEADS * DSA_IDX_DIM) ** -0.5
    qi_all = qi_ref[0].reshape(DSA_IDX_HEADS * QB, 128)
    q_all = q_ref[0].reshape(HEADS * QB, HEAD_DIM)

    def halves(c):
        ks = c * ck
        return [pl.multiple_of(ks + j * hk, hk) for j in range(2)]

    def score_body(c, _):
        starts = halves(c)
        rs = [_dot_nt(ki_ref[0, pl.ds(ks, hk), :], qi_all) for ks in starts]
        for ks, r in zip(starts, rs):
            sc = jnp.zeros((hk, QB), F32)
            for h in range(DSA_IDX_HEADS):
                sc = sc + w_t[h:h + 1, :] * jnp.maximum(r[:, h * QB:(h + 1) * QB], 0.0)
            sc = jnp.where(sc == 0.0, 0.0, sc)
            sc_ref[pl.ds(ks, hk), :] = jnp.where(_key_pos(ks, hk) <= _query_pos(q0, hk), sc, NEG_INF)
        return 0

    lax.fori_loop(0, nck, score_body, 0)

    n_beyond = (seq - nck * ck).astype(F32)
    n_part = 32

    def count(pred_fn):
        def cb(c, acc):
            ks = pl.multiple_of(c * ck, ck)
            hit = jnp.where(pred_fn(sc_ref[pl.ds(ks, ck), :], ks), 1.0, 0.0)
            return acc + jnp.sum(hit.reshape(ck // n_part, n_part, QB), axis=0)
        acc = lax.fori_loop(0, nck, cb, jnp.zeros((n_part, QB), F32))
        return jnp.sum(acc, axis=0, keepdims=True)

    def count_ge(thr):
        return count(lambda s, ks: s >= thr) + jnp.where(thr <= NEG_INF, n_beyond, 0.0)

    def search(n_static):
        def count_ge_static(thr):
            acc = jnp.zeros((n_part, QB), F32)
            for c in range(n_static):
                hit = jnp.where(sc_ref[c * ck:(c + 1) * ck, :] >= thr, 1.0, 0.0)
                acc = acc + jnp.sum(hit.reshape(ck // n_part, n_part, QB), axis=0)
            cnt = jnp.sum(acc, axis=0, keepdims=True)
            return cnt + jnp.where(thr <= NEG_INF, float(seq - n_static * ck), 0.0)

        def thr_body(it, carry):
            t_int, cnt_t = carry
            cand = t_int ^ lax.shift_left(jnp.int32(1), 31 - it)
            cnt = count_ge_static(_ordered_to_f32(cand))
            ok = cnt >= n_top
            return jnp.where(ok, cand, t_int), jnp.where(ok, cnt, cnt_t)

        t_int, cnt_t = lax.fori_loop(0, 32, thr_body, (jnp.full((1, QB), -2 ** 31, jnp.int32),
                                                       jnp.full((1, QB), float(seq), F32)))
        t_ref[...] = jnp.broadcast_to(t_int, t_ref.shape)
        cnt_ref[...] = jnp.broadcast_to(cnt_t, cnt_ref.shape)

    few_keys = q0 + QB <= n_top
    t_ref[...] = jnp.zeros(t_ref.shape, jnp.int32)
    cnt_ref[...] = jnp.zeros(cnt_ref.shape, F32)
    for n_static in range(1, seq // ck + 1):
        pl.when((nck == n_static) & jnp.logical_not(few_keys))(functools.partial(search, n_static))
    t_int = t_ref[0:1, :]
    cnt_t = cnt_ref[0:1, :]
    thr = jnp.where(few_keys, NEG_INF, _ordered_to_f32(t_int))
    thr_up = jnp.where(few_keys, NEG_INF, _ordered_to_f32(t_int + 1))
    m_ref[...] = jnp.full(m_ref.shape, seq, jnp.int32)
    lo_ref[...] = jnp.broadcast_to(thr, lo_ref.shape)

    @pl.when(jnp.max(cnt_t) > n_top)
    def _():
        need = n_top - count_ge(thr_up)
        tie_beyond = thr <= NEG_INF

        far = float(2 * seq)

        def mark(c, _):
            ks = pl.multiple_of(c * ck, ck)
            s = sc_ref[pl.ds(ks, ck), :]
            kp = _key_pos(ks, ck).astype(F32)
            tie_ref[pl.ds(ks, ck), :] = jnp.where(s >= thr, jnp.where(s < thr_up, kp, far), far)
            return 0

        lax.fori_loop(0, nck, mark, 0)

        def idx_body(it, m):
            cand = m + lax.shift_left(jnp.int32(1), (seq.bit_length() - 2) - it)
            cand_f = cand.astype(F32)
            below = count(lambda s, ks: tie_ref[pl.ds(ks, ck), :] < cand_f)
            below = below + jnp.where(tie_beyond, jnp.maximum(cand - nck * ck, 0).astype(F32), 0.0)
            return jnp.where(below < need, cand, m)

        m_idx = lax.fori_loop(0, seq.bit_length() - 1, idx_body, jnp.zeros((1, QB), jnp.int32))

        def fold(pick, reduce, init):
            def cb(c, acc):
                ks = pl.multiple_of(c * ck, ck)
                vals = pick(sc_ref[pl.ds(ks, ck), :], tie_ref[pl.ds(ks, ck), :])
                return reduce(acc, vals.reshape(ck // 8, 8, QB))
            return lax.fori_loop(0, nck, cb, jnp.full((8, QB), init, F32))

        best = jnp.max(fold(lambda s, tie: jnp.where(tie < far, s, -jnp.inf),
                            lambda a, v: jnp.maximum(a, jnp.max(v, axis=0)), -jnp.inf), axis=0, keepdims=True)
        first = jnp.min(fold(lambda s, tie: jnp.where(s >= best, tie, far),
                             lambda a, v: jnp.minimum(a, jnp.min(v, axis=0)), far), axis=0, keepdims=True)
        first = jnp.minimum(first, float(seq))
        single = need == 1.0
        m_ref[...] = jnp.broadcast_to(jnp.where(single, first.astype(jnp.int32), m_idx), m_ref.shape)
        lo_ref[...] = jnp.broadcast_to(jnp.where(single, best, thr), lo_ref.shape)

    m_idx = m_ref[0:1, :]
    tie_lo = lo_ref[0:1, :]

    tpos = _query_pos(q0, hk)

    def attn_scores(c):
        return tuple(_dot_nt(k_ref[0, pl.ds(ks, hk), :], q_all)
                     for ks in halves(c))

    def attn_consume(c, ss, carry):
        ms, acc = carry
        for ks, s_all in zip(halves(c), ss):
            kp = _key_pos(ks, hk)
            s_idx = sc_ref[pl.ds(ks, hk), :]
            bound = jnp.where(kp <= m_idx, tie_lo, thr_up)
            mask = s_idx >= jnp.where(kp <= tpos, bound, jnp.inf)
            ms, alpha, p = _softmax_step_t(ms, s_all, mask)
            acc = alpha * acc + _dot(_with_ones(vt_ref[0, :, pl.ds(ks, hk)]), p)
        return tuple(ms), acc

    _, acc = _chunk_loop(nck, attn_scores, attn_consume,
                             (tuple(_m_init() for _ in range(HEADS)), _acc_init(HEADS)))
    _store_heads_t(o_ref, [_finish_t(acc, h) for h in range(HEADS)])


def _dsa(q, k, vt, qi, ki, misc):
    B, H, S, _ = q.shape
    n_top = min(DSA_TOPK, S // 4)
    return pl.pallas_call(
        functools.partial(_dsa_kernel, seq=S, n_top=n_top),
        out_shape=jax.ShapeDtypeStruct((B, S, D_GROUP), F32),
        grid=(B, S // QB),
        in_specs=[pl.BlockSpec((1, H, QB, 64), lambda b, i: (b, 0, i, 0)),
                  pl.BlockSpec((1, S, 64), lambda b, i: (b, 0, 0)),
                  pl.BlockSpec((1, HEAD_DIM, S), lambda b, i: (b, 0, 0)),
                  pl.BlockSpec((1, DSA_IDX_HEADS, QB, 128), lambda b, i: (b, 0, i, 0)),
                  pl.BlockSpec((1, S, 128), lambda b, i: (b, 0, 0)),
                  pl.BlockSpec((1, QB, 128), lambda b, i: (b, i, 0))],
        out_specs=pl.BlockSpec((1, QB, D_GROUP), lambda b, i: (b, i, 0)),
        scratch_shapes=[pltpu.VMEM((S, QB), F32), pltpu.VMEM((8, QB), jnp.int32),
                        pltpu.VMEM((8, QB), jnp.int32), pltpu.VMEM((8, QB), F32), pltpu.VMEM((8, QB), F32),
                        pltpu.VMEM((S, QB), F32)],
        compiler_params=_params("parallel", "arbitrary"),
        name="dsa_attn",
    )(q, k, vt, qi, ki, misc)


def _gelu_tanh(x):
    return 0.5 * x * (1.0 + jnp.tanh(np.sqrt(2.0 / np.pi) * (x + 0.044715 * (x * x * x))))


def _nsa_cmp_kernel(kg_ref, vg_ref, pos_ref, wk1_ref, wk2_ref, wv1_ref, wv2_ref, kc_ref, vct_ref):
    def compress(g_ref, w1_ref, w2_ref):
        top = _dot((g_ref[0] + pos_ref[0:1, :]).astype(BF16), w1_ref[0])
        bot = _dot((g_ref[0] + pos_ref[1:2, :]).astype(BF16), w1_ref[1])
        pre = top + pltpu.roll(bot, bot.shape[0] - 1, 0)
        return _dot(_gelu_tanh(pre).astype(BF16), w2_ref[...])

    kc_ref[0] = compress(kg_ref, wk1_ref, wk2_ref).astype(BF16)
    vct_ref[0] = compress(vg_ref, wv1_ref, wv2_ref).T[:HEAD_DIM].astype(BF16)


def _nsa_compress(kcmp, vcmp, pos, wk1, wk2, wv1, wv2):
    B, S, _ = kcmp.shape
    ng = S // NSA_CMP_STRIDE
    gw = NSA_CMP_STRIDE * HEAD_DIM
    kg = kcmp.reshape(B, ng, gw)
    vg = vcmp.reshape(B, ng, gw)

    def const(a):
        return pl.BlockSpec(a.shape, lambda b: (0,) * a.ndim)

    wv2p = jnp.concatenate([wv2, jnp.zeros((HEAD_DIM, LANES - HEAD_DIM), wv2.dtype)], axis=1)
    ins = [kg, vg, pos.reshape(2, gw), wk1.reshape(2, gw, HEAD_DIM).astype(BF16), wk2.astype(BF16),
           wv1.reshape(2, gw, HEAD_DIM).astype(BF16), wv2p.astype(BF16)]
    blk = pl.BlockSpec((1, ng, gw), lambda b: (b, 0, 0))
    return pl.pallas_call(
        _nsa_cmp_kernel,
        out_shape=[jax.ShapeDtypeStruct((B, ng, HEAD_DIM), BF16), jax.ShapeDtypeStruct((B, HEAD_DIM, ng), BF16)],
        grid=(B,),
        in_specs=[blk, blk] + [const(a) for a in ins[2:]],
        out_specs=[pl.BlockSpec((1, ng, HEAD_DIM), lambda b: (b, 0, 0)),
                   pl.BlockSpec((1, HEAD_DIM, ng), lambda b: (b, 0, 0))],
        compiler_params=_params("parallel"),
        name="nsa_compress",
    )(*ins)


def _nsa_kernel(q_ref, kc_ref, vct_ref, ks_ref, vst_ref, kw_ref, vwt_ref, misc_ref, ovl_ref, o_ref,
                *, seq):
    i = pl.program_id(1)
    q0 = i * QB
    n_slc = seq // NSA_SEL_LEN
    n_cmp = kc_ref.shape[1]
    q_all = q_ref[0].reshape(HEADS * QB, HEAD_DIM)

    last_tok = lax.broadcasted_iota(jnp.int32, (n_cmp, QB), 0) * NSA_CMP_STRIDE + (NSA_CMP_LEN - 1)
    cmask = last_tok <= _query_pos(q0, n_cmp)
    s_cmp = _dot_nt(kc_ref[0], q_all)
    ws = jnp.maximum(q0 - NSA_WINDOW, 0)
    win_parts = [(pl.multiple_of(ws + off, QB), n) for off, n in NSA_WIN_PARTS]
    s_wins = [_dot_nt(kw_ref[0, pl.ds(start, n), :], q_all) for start, n in win_parts]
    ps = []
    p_sum = jnp.zeros((n_cmp, QB), F32)
    for h in range(HEADS):
        s = jnp.where(cmask, s_cmp[:, h * QB:(h + 1) * QB], NEG_INF)
        e = jnp.where(cmask, jnp.exp2(s - jnp.max(s, axis=0, keepdims=True)), 0.0)
        p = e / jnp.maximum(jnp.sum(e, axis=0, keepdims=True), 1e-30)
        p_sum = p_sum + p
        ps.append(p.astype(BF16))
    o_cmp = _dot(vct_ref[0], jnp.concatenate(ps, axis=1))

    hi, lo = _split_bf16(p_sum)
    imp = _dot(ovl_ref[...], jnp.concatenate([hi, lo], axis=0))
    jb = lax.broadcasted_iota(jnp.int32, (n_slc, 1), 0)
    cur = (q0 + lax.broadcasted_iota(jnp.int32, (1, QB), 1)) >> SEL_SHIFT
    forced = (jb < NSA_N_INIT) | ((jb <= cur) & (jb > cur - NSA_N_LOCAL))
    imp = jnp.where(forced, FORCE_SCORE, jnp.where(jb <= cur, imp, NEG_INF))
    sub = 8
    ranks = []
    for g in range(n_slc // sub):
        blk = imp[g * sub:(g + 1) * sub]
        jg = jb[g * sub:(g + 1) * sub]
        rank_g = jnp.zeros((sub, QB), F32)
        for r in range(n_slc):
            row = imp[r:r + 1, :]
            ge = jnp.where(row >= blk, 1.0, 0.0)
            gt = jnp.where(row > blk, 1.0, 0.0)
            if g * sub > r:
                rank_g = rank_g + ge
            elif (g + 1) * sub - 1 <= r:
                rank_g = rank_g + gt
            else:
                rank_g = rank_g + jnp.where(jg > r, ge, gt)
        ranks.append(rank_g)
    rank = jnp.concatenate(ranks, axis=0)
    sel_t = jnp.where(rank < min(NSA_N_SEL, n_slc), 1.0, 0.0)
    sel_t = jnp.concatenate([sel_t, jnp.zeros((LANES - n_slc, QB), F32)], axis=0).astype(BF16)

    ms_win, acc_win = [_m_init() for _ in range(HEADS)], _acc_init(HEADS)
    for (start, n), s_win in zip(win_parts, s_wins):
        dist = _query_pos(q0, n) - _key_pos(start, n)
        wmask = (dist >= 0) & (dist < NSA_WINDOW)
        ms_win, alpha, p_win = _softmax_step_t(ms_win, s_win, wmask)
        acc_win = alpha * acc_win + _dot(_with_ones(vwt_ref[0, :, pl.ds(start, n)]), p_win)

    ck = DSA_CK
    hk = ck // 2
    tpos = _query_pos(q0, hk)

    n_slc_chunks = _n_chunks(q0, QB, ck)

    def slc_starts(c):
        return [pl.multiple_of(c * ck + j * hk, hk) for j in range(2)]

    def slc_scores(c):
        out = []
        for ks in slc_starts(c):
            blk = (ks + lax.broadcasted_iota(jnp.int32, (hk, LANES), 0)) >> SEL_SHIFT
            expand = jnp.where(blk == lax.broadcasted_iota(jnp.int32, (hk, LANES), 1), 1.0, 0.0)
            out.append(_dot(expand.astype(BF16), sel_t))
        return tuple(out) + tuple(_dot_nt(ks_ref[0, pl.ds(ks, hk), :], q_all) for ks in slc_starts(c))

    def slc_consume(c, ss, carry):
        ms, acc = carry
        for ks, chosen, s_all in zip(slc_starts(c), ss[:2], ss[2:]):
            mask = chosen > jnp.where(_key_pos(ks, hk) <= tpos, 0.5, 2.0)
            ms, alpha, p = _softmax_step_t(ms, s_all, mask)
            acc = alpha * acc + _dot(_with_ones(vst_ref[0, :, pl.ds(ks, hk)]), p)
        return tuple(ms), acc

    _, acc_slc = _chunk_loop(n_slc_chunks, slc_scores, slc_consume,
                                 (tuple(_m_init() for _ in range(HEADS)), _acc_init(HEADS)))

    gate_rows = 16
    gate = jax.nn.sigmoid(misc_ref[0].T[MISC_GATE:MISC_GATE + gate_rows, :])
    outs = []
    for h in range(HEADS):
        outs.append(gate[h:h + 1] * o_cmp[:, h * QB:(h + 1) * QB]
                    + gate[HEADS + h:HEADS + h + 1] * _finish_t(acc_slc, h)
                    + gate[2 * HEADS + h:2 * HEADS + h + 1] * _finish_t(acc_win, h))
    _store_heads_t(o_ref, outs)


def _nsa_overlap(seq):
    ng = seq // NSA_CMP_STRIDE
    n_slc = seq // NSA_SEL_LEN
    n = np.arange(ng)
    first, last = n * NSA_CMP_STRIDE, n * NSA_CMP_STRIDE + NSA_CMP_LEN - 1
    start = np.arange(n_slc) * NSA_SEL_LEN
    ovl = ((first[None, :] <= start[:, None] + NSA_SEL_LEN - 1) & (last[None, :] >= start[:, None]))
    ovl = ovl & (last[None, :] < seq)
    ovl = ovl.astype(np.float32)
    return jnp.asarray(np.concatenate([ovl, ovl], 1), BF16)


def _nsa(q, kc, vct, k_slc, vt_slc, k_win, vt_win, misc):
    B, H, S, _ = q.shape
    ng = kc.shape[1]
    ovl = _nsa_overlap(S)
    keys = pl.BlockSpec((1, S, HEAD_DIM), lambda b, i: (b, 0, 0))
    vals_t = pl.BlockSpec((1, HEAD_DIM, S), lambda b, i: (b, 0, 0))
    return pl.pallas_call(
        functools.partial(_nsa_kernel, seq=S),
        out_shape=jax.ShapeDtypeStruct((B, S, D_GROUP), F32),
        grid=(B, S // QB),
        in_specs=[pl.BlockSpec((1, H, QB, 64), lambda b, i: (b, 0, i, 0)),
                  pl.BlockSpec((1, ng, HEAD_DIM), lambda b, i: (b, 0, 0)),
                  pl.BlockSpec((1, HEAD_DIM, ng), lambda b, i: (b, 0, 0)),
                  keys, vals_t, keys, vals_t,
                  pl.BlockSpec((1, QB, 128), lambda b, i: (b, i, 0)),
                  pl.BlockSpec(ovl.shape, lambda b, i: (0, 0))],
        out_specs=pl.BlockSpec((1, QB, D_GROUP), lambda b, i: (b, i, 0)),
        compiler_params=_params("parallel", "arbitrary"),
        name="nsa_attn",
    )(q, kc, vct, k_slc, vt_slc, k_win, vt_win, misc, ovl)


def _outproj_kernel(x_ref, ya_ref, yb_ref, yc_ref, yd_ref, gg_ref, wo_ref, g_ref, b_ref, o_ref):
    acc = None
    for gi, y_ref in enumerate((ya_ref, yb_ref, yc_ref, yd_ref)):
        lo, hi = gi * D_GROUP, (gi + 1) * D_GROUP
        yn = _rms_norm(y_ref[...], gg_ref[:, lo:hi]).astype(BF16)
        part = _dot(yn, wo_ref[lo:hi, :])
        acc = part if acc is None else acc + part
    o_ref[...] = _layer_norm(DN_ALPHA * x_ref[...] + acc, g_ref[...], b_ref[...])


def _outproj(x2, ys, gg, wo, g, b, *, tm):
    n, d = x2.shape
    row = lambda w: pl.BlockSpec((tm, w), lambda i: (i, 0))
    const = lambda a: pl.BlockSpec(a.shape, lambda i: (0, 0))
    return pl.pallas_call(
        _outproj_kernel,
        out_shape=jax.ShapeDtypeStruct((n, d), F32),
        grid=(n // tm,),
        in_specs=[row(d)] + [row(D_GROUP)] * 4 + [const(gg), const(wo), const(g), const(b)],
        out_specs=row(d),
        compiler_params=_params("parallel"),
        name="out_proj",
    )(x2, *ys, gg, wo, g, b)


def _mixer(x, w_in, q_norm, w_uq, kv_norm, w_ukv, cmp_pos, wk1, wk2, wv1, wv2, tabs, place, maps):
    B, S, D = x.shape
    wm = _gather_cols(w_in, maps).astype(BF16)
    wt = jnp.concatenate([wm[:, slice(*_MAIN_OFF['slab_v'])], wm[:, slice(*_MAIN_OFF['sb_v'])]], axis=1).T
    (mq, mk, mvt, nq, dq, sbq, sbk, sbvt, kcmp, kslc, kwin, dk, vcmp, vslct, vwint, dvt,
     qi, ki, misc) = _inproj(x, wm, wt, tabs, q_norm[None, :], kv_norm[None, :],
                             _mla_weights(w_uq, w_ukv), place, tm=min(512, S))
    y_a = _mla(mq, mk, mvt)
    kc, vct = _nsa_compress(kcmp, vcmp, cmp_pos, wk1, wk2, wv1, wv2)
    y_b = _nsa(nq, kc, vct, kslc, vslct, kwin, vwint, misc)
    y_c = _dsa(dq, dk, dvt, qi, ki, misc)
    y_d = _sb(sbq, sbk, sbvt)
    return y_a, y_b, y_c, y_d


def kernel(x, ln1_g, ln1_b, ffn1_w1, ffn1_w3, ffn1_w2, w_in, mla_q_norm, mla_w_uq, mla_kv_norm,
           mla_w_ukv, nsa_cmp_pos, nsa_cmp_wk1, nsa_cmp_wk2, nsa_cmp_wv1, nsa_cmp_wv2, group_norm_g,
           w_out, ln2_g, ln2_b, ffn2_w1, ffn2_w3, ffn2_w2, ln3_g, ln3_b):
    B, S, D = x.shape
    n = B * S
    tabs = _rope_tables(S)
    place = _placement_constants()
    maps = _column_maps()
    tm_ffn = min(512, n)
    x2 = x.reshape(n, D)
    for l in range(DEPTH):
        x2 = _ffn_ln(x2, ffn1_w1[l].astype(BF16), ffn1_w3[l].astype(BF16), ffn1_w2[l].astype(BF16),
                     ln1_g[l][None, :], ln1_b[l][None, :], tm=tm_ffn, tf=256)
        ys = _mixer(x2.reshape(B, S, D), w_in[l], mla_q_norm[l], mla_w_uq[l], mla_kv_norm[l],
                    mla_w_ukv[l], nsa_cmp_pos[l], nsa_cmp_wk1[l], nsa_cmp_wk2[l], nsa_cmp_wv1[l],
                    nsa_cmp_wv2[l], tabs, place, maps)
        x2 = _outproj(x2, [y.reshape(n, D_GROUP) for y in ys], group_norm_g[l][None, :],
                      w_out[l].astype(BF16), ln2_g[l][None, :], ln2_b[l][None, :], tm=min(1024, n))
        x2 = _ffn_ln(x2, ffn2_w1[l].astype(BF16), ffn2_w3[l].astype(BF16), ffn2_w2[l].astype(BF16),
                     ln3_g[l][None, :], ln3_b[l][None, :], tm=tm_ffn, tf=256)
    return x2.reshape(B, S, D)
```
